```python
import jax
import jax.numpy as jnp
from jax import lax
import numpy as np

D_MODEL = 1024
BATCH = 1
SEQ = 16384
DEPTH = 4

A_HEADS = 8
A_HEAD_DIM = 64
A_WIDTH = A_HEADS * A_HEAD_DIM
A_DECAY_LORA = 64
A_ICLR_LORA = 64
A_VRES_LORA = 32
A_GATE_LORA = 128
A_GN_EPS = 64e-5
B_HEADS = 8
B_NOPE_DIM = 64
B_ROPE_DIM = 32
B_V_DIM = 64
B_Q_RANK = 384
B_KV_RANK = 256
B_WIDTH = B_HEADS * B_V_DIM
ROPE_THETA = 10000.0
Q_BLOCK = 128
C_HEADS = 4
C_EXPAND = 128
C_HEAD_DIM = 128
C_FDIM = C_HEADS * C_EXPAND
C_WIDTH = C_HEADS * C_HEAD_DIM
C_CHUNK = 64
C_MIN_FORGET = 1e-6
D_FF = 2816
N_BRANCH = 3
MACARON_WEIGHT = 0.5
NORM_EPS = 1e-6

A_SPLITS = (A_WIDTH, A_WIDTH, A_WIDTH, A_DECAY_LORA, A_ICLR_LORA, A_GATE_LORA)
REST_SPLITS = (B_Q_RANK, B_KV_RANK, B_ROPE_DIM, C_FDIM, C_FDIM, C_WIDTH, C_WIDTH, N_BRANCH * D_MODEL)
A_COLS = sum(A_SPLITS)
N_IN = A_COLS + sum(REST_SPLITS)

kernel_name = 'hybrid_rwkv7_mla_hgrn2_macaron'


def split_cols(t, sizes):
    return jnp.split(t, np.cumsum(sizes)[:-1].tolist(), axis=-1)


def rms_norm(x, g):
    x32 = x.astype(jnp.float32)
    y = x32 * lax.rsqrt(jnp.mean(x32 * x32, axis=-1, keepdims=True) + NORM_EPS)
    return (y * g.astype(jnp.float32)).astype(x.dtype)


def token_shift(p):
    return jnp.pad(p, ((0, 0), (1, 0), (0, 0)))[:, :-1]


def shift_mix(p, mu):
    return p + (token_shift(p) - p) * mu


def swiglu(x, w_gate, w_up, w_down):
    return (jax.nn.silu(x @ w_gate) * (x @ w_up)) @ w_down


def rope_tables(positions):
    inv_freq = ROPE_THETA ** (-jnp.arange(0, B_ROPE_DIM, 2, dtype=jnp.float32) / B_ROPE_DIM)
    ang = positions.astype(jnp.float32)[..., None] * inv_freq
    return jnp.cos(ang), jnp.sin(ang)


def apply_rope(x, cos, sin):
    x1, x2 = jnp.split(x.astype(jnp.float32), 2, axis=-1)
    return jnp.concatenate([x1 * cos - x2 * sin, x2 * cos + x1 * sin], axis=-1).astype(x.dtype)


def rwkv7_scan(r, w, k, v, kk, a):
    bsz, _, n_h, n = r.shape

    def step(state, inp):
        r_t, w_t, k_t, v_t, kk_t, a_t = inp
        sa = jnp.einsum('bhvk,bhk->bhv', state, -kk_t)
        state = (state * w_t[:, :, None, :]
                 + sa[..., None] * (kk_t * a_t)[:, :, None, :]
                 + v_t[..., None] * k_t[:, :, None, :])
        return state, jnp.einsum('bhvk,bhk->bhv', state, r_t)

    xs = tuple(jnp.moveaxis(t, 1, 0) for t in (r, w, k, v, kk, a))
    state0 = jnp.zeros((bsz, n_h, n, n), jnp.float32)
    _, y = lax.scan(step, state0, xs)
    return jnp.moveaxis(y, 0, 1)


def rwkv7_branch(r, k, v, wl, al, gl, w0, w_up, a0, a_up, g_up, k_k, k_a, r_k, gn_g, gn_b):
    bsz, seq, _ = r.shape
    f32 = jnp.float32

    def heads(t):
        return t.astype(f32).reshape(bsz, seq, A_HEADS, A_HEAD_DIM)

    w = -jax.nn.softplus(-(w0 + jnp.tanh(wl) @ w_up)) - 0.5
    decay = jnp.exp(-jnp.exp(w.astype(f32)))
    a = jax.nn.sigmoid(a0 + al @ a_up)
    g = jax.nn.sigmoid(gl) @ g_up
    kk = heads(k * k_k)
    kk = kk / jnp.maximum(jnp.sqrt(jnp.sum(kk * kk, axis=-1, keepdims=True)), 1e-12)
    k = k * (1.0 + (a - 1.0) * k_a)
    rh, kh, vh, ah = heads(r), heads(k), heads(v), heads(a)
    y = rwkv7_scan(rh, heads(decay), kh, vh, kk, ah)
    mean = jnp.mean(y, axis=-1, keepdims=True)
    var = jnp.mean(jnp.square(y - mean), axis=-1, keepdims=True)
    y = ((y - mean) * lax.rsqrt(var + A_GN_EPS)).reshape(bsz, seq, A_WIDTH) * gn_g.astype(f32) + gn_b.astype(f32)
    bonus = jnp.sum(rh * kh * r_k.astype(f32), axis=-1, keepdims=True) * vh
    y = y + bonus.reshape(bsz, seq, A_WIDTH)
    return y.astype(r.dtype) * g


def causal_block_attention(q, k, v):
    bsz, seq, n_h, d_qk = q.shape
    n_blk = seq // Q_BLOCK
    scale = d_qk ** -0.5
    q_blocks = jnp.moveaxis(q.reshape(bsz, n_blk, Q_BLOCK, n_h, d_qk), 1, 0)
    k_pos = jnp.arange(seq)
    neg = jnp.finfo(jnp.float32).min

    def one_block(args):
        q_blk, blk = args
        s = jnp.einsum('bqhd,bkhd->bhqk', q_blk, k).astype(jnp.float32) * scale
        q_pos = blk * Q_BLOCK + jnp.arange(Q_BLOCK)
        s = jnp.where(k_pos[None, :] <= q_pos[:, None], s, neg)
        p = jax.nn.softmax(s, axis=-1)
        return jnp.einsum('bhqk,bkhd->bqhd', p.astype(v.dtype), v)

    o = lax.map(one_block, (q_blocks, jnp.arange(n_blk)))
    return jnp.moveaxis(o, 0, 1).reshape(bsz, seq, n_h, v.shape[-1])


def mla_branch(cq, ckv, kr, cos, sin, q_norm_g, w_uq, kv_norm_g, w_ukv):
    bsz, seq, _ = cq.shape
    q = (rms_norm(cq, q_norm_g) @ w_uq).reshape(bsz, seq, B_HEADS, B_NOPE_DIM + B_ROPE_DIM)
    q_nope, q_rope = q[..., :B_NOPE_DIM], q[..., B_NOPE_DIM:]
    q_rope = apply_rope(q_rope, cos[:, :, None], sin[:, :, None])
    kv = (rms_norm(ckv, kv_norm_g) @ w_ukv).reshape(bsz, seq, B_HEADS, B_NOPE_DIM + B_V_DIM)
    k_nope, v = kv[..., :B_NOPE_DIM], kv[..., B_NOPE_DIM:]
    k_rope = apply_rope(kr, cos, sin)[:, :, None, :]
    k = jnp.concatenate([k_nope, jnp.broadcast_to(k_rope, (bsz, seq, B_HEADS, B_ROPE_DIM))], axis=-1)
    q = jnp.concatenate([q_nope, q_rope], axis=-1)
    o = causal_block_attention(q, k, v)
    return o.reshape(bsz, seq, B_WIDTH)


def hgrn2_chunk_scan(q, k, log_f, v):
    bsz, seq, n_h, d_k = q.shape
    d_v = v.shape[-1]
    n_chunk = seq // C_CHUNK

    def to_chunks(t):
        return jnp.moveaxis(t.reshape(bsz, n_chunk, C_CHUNK, *t.shape[2:]), 1, 0)

    causal = jnp.tril(jnp.ones((C_CHUNK, C_CHUNK), dtype=bool))[None, :, :, None, None]

    def step(state, inp):
        qc, kc, lfc, vc = inp
        b = jnp.cumsum(lfc, axis=1)
        diff = b[:, :, None] - b[:, None, :]
        decay = jnp.where(causal, jnp.exp(jnp.where(causal, diff, 0.0)), 0.0)
        attn = jnp.einsum('bthk,btshk,bshk->btsh', qc, decay, kc)
        o = (jnp.einsum('btsh,bshv->bthv', attn, vc)
             + jnp.einsum('bthk,bhkv->bthv', qc * jnp.exp(b), state))
        b_last = b[:, -1]
        state = (jnp.exp(b_last)[..., None] * state
                 + jnp.einsum('bshk,bshv->bhkv', kc * jnp.exp(b_last[:, None] - b), vc))
        return state, o

    state0 = jnp.zeros((bsz, n_h, d_k, d_v), jnp.float32)
    _, o = lax.scan(step, state0, tuple(to_chunks(t) for t in (q, k, log_f, v)))
    return jnp.moveaxis(o, 0, 1).reshape(bsz, seq, n_h, d_v)


def hgrn2_branch(cq, cf, ci, cg, lb, norm_g):
    bsz, seq, _ = cq.shape
    f32 = jnp.float32
    fz = cf.astype(f32)
    lb = lb.astype(f32)
    f = lb + (1.0 - lb) * jax.nn.sigmoid(fz)
    log_f = jnp.log(jnp.maximum(f, C_MIN_FORGET))
    k = (1.0 - lb) * jax.nn.sigmoid(-fz)
    q = jax.nn.silu(cq.astype(f32))

    def fheads(t):
        return t.reshape(bsz, seq, C_HEADS, C_EXPAND)

    o = hgrn2_chunk_scan(fheads(q), fheads(k), fheads(log_f),
                         ci.astype(f32).reshape(bsz, seq, C_HEADS, C_HEAD_DIM))
    o = rms_norm(o, norm_g) * jax.nn.silu(cg.astype(f32)).reshape(bsz, seq, C_HEADS, C_HEAD_DIM)
    return o.reshape(bsz, seq, C_WIDTH).astype(cq.dtype)


def setup_inputs(seed: int = 0) -> dict:
    key = jax.random.key(seed)
    it = iter(list(jax.random.split(key, 48)))
    f32 = jnp.float32
    L = DEPTH

    def dense(shape):
        return jax.random.normal(next(it), shape, f32) * (shape[-2] ** -0.5)

    def gain(shape):
        return 1.0 + 0.05 * jax.random.normal(next(it), shape, f32)

    def noise(shape, scale, offset=0.0):
        return offset + scale * jax.random.normal(next(it), shape, f32)

    def mix_coef(shape):
        return jax.random.uniform(next(it), shape, f32, 0.05, 0.95)

    return {
        'x': jax.random.normal(next(it), (BATCH, SEQ, D_MODEL), f32),
        'positions': jnp.broadcast_to(jnp.arange(SEQ, dtype=jnp.int32), (BATCH, SEQ)),
        'ffn1_pre_g': gain((L, D_MODEL)),
        'ffn1_post_g': gain((L, D_MODEL)),
        'ffn1_w_gate': dense((L, D_MODEL, D_FF)),
        'ffn1_w_up': dense((L, D_MODEL, D_FF)),
        'ffn1_w_down': dense((L, D_FF, D_MODEL)),
        'mix_pre_g': gain((L, D_MODEL)),
        'mix_post_g': gain((L, D_MODEL)),
        'w_in': dense((L, D_MODEL, N_IN)),
        'rwkv_mu': mix_coef((L, A_COLS)),
        'rwkv_w0': noise((L, A_WIDTH), 0.5),
        'rwkv_w_up': dense((L, A_DECAY_LORA, A_WIDTH)),
        'rwkv_a0': noise((L, A_WIDTH), 0.1),
        'rwkv_a_up': dense((L, A_ICLR_LORA, A_WIDTH)),
        'rwkv_g_up': dense((L, A_GATE_LORA, A_WIDTH)),
        'rwkv_k_k': noise((L, A_WIDTH), 0.05, 0.85),
        'rwkv_k_a': gain((L, A_WIDTH)),
        'rwkv_r_k': noise((L, A_HEADS, A_HEAD_DIM), 0.1),
        'rwkv_gn_g': gain((L, A_WIDTH)),
        'rwkv_gn_b': noise((L, A_WIDTH), 0.02),
        'rwkv_vres_down': dense((L - 1, D_MODEL, A_VRES_LORA)),
        'rwkv_vres_mu': mix_coef((L - 1, A_VRES_LORA)),
        'rwkv_vres_up': dense((L - 1, A_VRES_LORA, A_WIDTH)),
        'rwkv_v0': noise((L - 1, A_WIDTH), 0.1, 1.0),
        'rwkv_out': dense((L, A_WIDTH, D_MODEL)),
        'mla_q_norm_g': gain((L, B_Q_RANK)),
        'mla_w_uq': dense((L, B_Q_RANK, B_HEADS * (B_NOPE_DIM + B_ROPE_DIM))),
        'mla_kv_norm_g': gain((L, B_KV_RANK)),
        'mla_w_ukv': dense((L, B_KV_RANK, B_HEADS * (B_NOPE_DIM + B_V_DIM))),
        'mla_out': dense((L, B_WIDTH, D_MODEL)),
        'hgrn_lower_bounds': gain((L, C_FDIM)),
        'hgrn_norm_g': gain((L, C_HEAD_DIM)),
        'hgrn_out': dense((L, C_WIDTH, D_MODEL)),
        'w_o': dense((L, D_MODEL, D_MODEL)),
        'ffn2_pre_g': gain((L, D_MODEL)),
        'ffn2_post_g': gain((L, D_MODEL)),
        'ffn2_w_gate': dense((L, D_MODEL, D_FF)),
        'ffn2_w_up': dense((L, D_MODEL, D_FF)),
        'ffn2_w_down': dense((L, D_FF, D_MODEL)),
    }


def reference(x, positions, ffn1_pre_g, ffn1_post_g, ffn1_w_gate, ffn1_w_up, ffn1_w_down,
              mix_pre_g, mix_post_g, w_in,
              rwkv_mu, rwkv_w0, rwkv_w_up, rwkv_a0, rwkv_a_up, rwkv_g_up, rwkv_k_k, rwkv_k_a, rwkv_r_k,
              rwkv_gn_g, rwkv_gn_b, rwkv_vres_down, rwkv_vres_mu, rwkv_vres_up, rwkv_v0, rwkv_out,
              mla_q_norm_g, mla_w_uq, mla_kv_norm_g, mla_w_ukv, mla_out,
              hgrn_lower_bounds, hgrn_norm_g, hgrn_out,
              w_o, ffn2_pre_g, ffn2_post_g, ffn2_w_gate, ffn2_w_up, ffn2_w_down):
    bsz, seq, _ = x.shape
    cos, sin = rope_tables(positions)
    lb_p = jax.nn.softmax(hgrn_lower_bounds.astype(jnp.float32), axis=0)
    lower_bounds = jnp.cumsum(lb_p, axis=0) - lb_p[0]
    h = x
    v_first = None
    for l in range(DEPTH):
        y = swiglu(rms_norm(h, ffn1_pre_g[l]), ffn1_w_gate[l], ffn1_w_up[l], ffn1_w_down[l])
        h = h + MACARON_WEIGHT * rms_norm(y, ffn1_post_g[l])

        u = rms_norm(h, mix_pre_g[l])
        if l == 0:
            proj = u @ w_in[l]
        else:
            proj = u @ jnp.concatenate([w_in[l], rwkv_vres_down[l - 1]], axis=1)
        a_r, a_k, a_v, a_wl, a_al, a_gl = split_cols(shift_mix(proj[..., :A_COLS], rwkv_mu[l]), A_SPLITS)
        b_cq, b_ckv, b_kr, c_q, c_f, c_i, c_g, gate_logits = split_cols(proj[..., A_COLS:N_IN], REST_SPLITS)
        if l == 0:
            v_first = a_v
        else:
            a_vl = shift_mix(proj[..., N_IN:], rwkv_vres_mu[l - 1])
            a_v = a_v + (v_first - a_v) * jax.nn.sigmoid(rwkv_v0[l - 1] + a_vl @ rwkv_vres_up[l - 1])

        y_a = rwkv7_branch(a_r, a_k, a_v, a_wl, a_al, a_gl, rwkv_w0[l], rwkv_w_up[l], rwkv_a0[l],
                           rwkv_a_up[l], rwkv_g_up[l], rwkv_k_k[l], rwkv_k_a[l], rwkv_r_k[l],
                           rwkv_gn_g[l], rwkv_gn_b[l])
        y_b = mla_branch(b_cq, b_ckv, b_kr, cos, sin, mla_q_norm_g[l], mla_w_uq[l],
                         mla_kv_norm_g[l], mla_w_ukv[l])
        y_c = hgrn2_branch(c_q, c_f, c_i, c_g, lower_bounds[l], hgrn_norm_g[l])

        gates = jax.nn.sigmoid(gate_logits).reshape(bsz, seq, N_BRANCH, D_MODEL)
        merged = (gates[:, :, 0] * (y_a @ rwkv_out[l])
                  + gates[:, :, 1] * (y_b @ mla_out[l])
                  + gates[:, :, 2] * (y_c @ hgrn_out[l]))
        h = h + rms_norm(merged @ w_o[l], mix_post_g[l])

        y = swiglu(rms_norm(h, ffn2_pre_g[l]), ffn2_w_gate[l], ffn2_w_up[l], ffn2_w_down[l])
        h = h + MACARON_WEIGHT * rms_norm(y, ffn2_post_g[l])
    return h
```

```python
import functools

import jax
import jax.numpy as jnp
import numpy as np
from jax import lax
from jax.experimental import pallas as pl
from jax.experimental.pallas import tpu as pltpu

F32 = jnp.float32
BF16 = jnp.bfloat16

D_MODEL = 1024
D_FF = 2816
NORM_EPS = 1e-6
MACARON_WEIGHT = 0.5

A_HEADS = 8
A_HEAD_DIM = 64
A_WIDTH = 512
A_GN_EPS = 64e-5
A_CHUNK = 64
A_COLS = 1792
A_IN = 2048

B_HEADS = 8
B_NOPE = 64
B_ROPE = 32
B_V = 64
B_Q_RANK = 384
B_KV_RANK = 256
B_IN = 896
ROPE_THETA = 10000.0
LANES = 128

C_HEADS = 4
C_DIM = 128
C_WIDTH = 512
C_IN = 2048
C_CHUNK = 128
C_SUB = 16
C_MIN_FORGET = 1e-6

N_BRANCH = 3
ROW_TILE = 512
VMEM_LIMIT = 56 * 1024 * 1024

_HI = lax.Precision.HIGHEST


def _bdot(a, b):
    return jnp.dot(a.astype(BF16), b.astype(BF16), preferred_element_type=F32)


def _bdot_nt(a, b):
    return lax.dot_general(a.astype(BF16), b.astype(BF16), (((1,), (1,)), ((), ())),
                           preferred_element_type=F32)


def _hdot(a, b):
    return jnp.dot(a, b, precision=_HI, preferred_element_type=F32)


def _rms(x, g):
    return x * lax.rsqrt(jnp.mean(x * x, axis=-1, keepdims=True) + NORM_EPS) * g


def _sigmoid(x):
    return 1.0 / (1.0 + jnp.exp(-x))


def _params(*sem):
    return pltpu.CompilerParams(dimension_semantics=sem, vmem_limit_bytes=VMEM_LIMIT)


def _ffn_body(x_ref, pre_g_ref, wg_ref, wu_ref, wd_ref, post_g_ref, o_ref, xn_ref, acc_ref):
    j = pl.program_id(1)

    @pl.when(j == 0)
    def _():
        xn_ref[...] = _rms(x_ref[...], pre_g_ref[...]).astype(BF16)
        acc_ref[...] = jnp.zeros_like(acc_ref)

    xn = xn_ref[...]
    gate = jnp.dot(xn, wg_ref[...], preferred_element_type=F32)
    up = jnp.dot(xn, wu_ref[...], preferred_element_type=F32)
    mid = (gate * _sigmoid(gate) * up).astype(BF16)
    acc_ref[...] += jnp.dot(mid, wd_ref[...], preferred_element_type=F32)

    @pl.when(j == pl.num_programs(1) - 1)
    def _():
        o_ref[...] = x_ref[...] + MACARON_WEIGHT * _rms(acc_ref[...], post_g_ref[...])


def _ffn(h, pre_g, wg, wu, wd, post_g):
    s = h.shape[0]
    tf = D_FF // 2
    return pl.pallas_call(
        _ffn_body,
        grid=(s // ROW_TILE, D_FF // tf),
        in_specs=[
            pl.BlockSpec((ROW_TILE, D_MODEL), lambda i, j: (i, 0)),
            pl.BlockSpec((1, D_MODEL), lambda i, j: (0, 0)),
            pl.BlockSpec((D_MODEL, tf), lambda i, j: (0, j)),
            pl.BlockSpec((D_MODEL, tf), lambda i, j: (0, j)),
            pl.BlockSpec((tf, D_MODEL), lambda i, j: (j, 0)),
            pl.BlockSpec((1, D_MODEL), lambda i, j: (0, 0)),
        ],
        out_specs=pl.BlockSpec((ROW_TILE, D_MODEL), lambda i, j: (i, 0)),
        out_shape=jax.ShapeDtypeStruct((s, D_MODEL), F32),
        scratch_shapes=[pltpu.VMEM((ROW_TILE, D_MODEL), BF16), pltpu.VMEM((ROW_TILE, D_MODEL), F32)],
        compiler_params=_params("arbitrary", "arbitrary"),
        name="ffn",
    )(h, pre_g, wg, wu, wd, post_g)


def _proj_body(x_ref, g_ref, w_ref, o_ref):
    xn = _rms(x_ref[...], g_ref[...]).astype(BF16)
    o_ref[...] = jnp.dot(xn, w_ref[...], preferred_element_type=F32)


def _proj_shift_body(x_ref, g_ref, w_ref, mu_ref, o_ref, carry_ref):
    @pl.when(pl.program_id(0) == 0)
    def _():
        carry_ref[...] = jnp.zeros_like(carry_ref)

    xn = _rms(x_ref[...], g_ref[...]).astype(BF16)
    p = jnp.dot(xn, w_ref[...], preferred_element_type=F32)
    rows = p.shape[0]
    prev = pltpu.roll(p, 1, axis=0)
    first = lax.broadcasted_iota(jnp.int32, p.shape, 0) == 0
    prev = jnp.where(first, carry_ref[0:1, :], prev)
    carry_ref[0:1, :] = p[rows - 1:rows, :]
    o_ref[...] = p + (prev - p) * mu_ref[...]


def _proj(h, pre_g, w, mu=None):
    s = h.shape[0]
    n = w.shape[1]
    x_spec = pl.BlockSpec((ROW_TILE, D_MODEL), lambda i: (i, 0))
    g_spec = pl.BlockSpec((1, D_MODEL), lambda i: (0, 0))
    w_spec = pl.BlockSpec((D_MODEL, n), lambda i: (0, 0))
    o_spec = pl.BlockSpec((ROW_TILE, n), lambda i: (i, 0))
    out_shape = jax.ShapeDtypeStruct((s, n), F32)
    if mu is None:
        return pl.pallas_call(
            _proj_body, grid=(s // ROW_TILE,), in_specs=[x_spec, g_spec, w_spec], out_specs=o_spec,
            out_shape=out_shape, compiler_params=_params("arbitrary"), name="proj",
        )(h, pre_g, w)
    return pl.pallas_call(
        _proj_shift_body, grid=(s // ROW_TILE,),
        in_specs=[x_spec, g_spec, w_spec, pl.BlockSpec((1, n), lambda i: (0, 0))], out_specs=o_spec,
        out_shape=out_shape, scratch_shapes=[pltpu.VMEM((8, n), F32)],
        compiler_params=_params("arbitrary"), name="proj_shift",
    )(h, pre_g, w, mu)


def _tri_inverse(a_strict, eye, blk16, lvl1, lvl2):
    d = a_strict * blk16
    d2 = _hdot(d, d)
    d4 = _hdot(d2, d2)
    d8 = _hdot(d4, d4)
    t = eye + d
    t = t + _hdot(t, d2)
    t = t + _hdot(t, d4)
    t = t + _hdot(t, d8)
    t = t + _hdot(_hdot(t, a_strict * lvl1), t)
    t = t + _hdot(_hdot(t, a_strict * lvl2), t)
    return t


def _rwkv_body(has_vres, *refs):
    if has_vres:
        (x_ref, vf_ref, w0_ref, wup_ref, a0_ref, aup_ref, gup_ref, kk_ref, ka_ref, rk_ref, gng_ref, gnb_ref,
         v0_ref, vup_ref, y_ref, state_ref) = refs
    else:
        (x_ref, w0_ref, wup_ref, a0_ref, aup_ref, gup_ref, kk_ref, ka_ref, rk_ref, gng_ref, gnb_ref,
         y_ref, vf_out_ref, state_ref) = refs

    @pl.when(pl.program_id(0) == 0)
    def _():
        state_ref[...] = jnp.zeros_like(state_ref)

    c = A_CHUNK
    w = A_WIDTH
    x = x_ref[...]
    r = x[:, 0:w]
    k = x[:, w:2 * w]
    v = x[:, 2 * w:3 * w]
    wl = x[:, 3 * w:3 * w + LANES]
    al = x[:, 3 * w + LANES:3 * w + 2 * LANES]
    gl = x[:, 3 * w + 2 * LANES:3 * w + 3 * LANES]

    z = w0_ref[...] + _bdot(jnp.tanh(wl), wup_ref[...])
    softplus_neg = jnp.maximum(-z, 0.0) + jnp.log(1.0 + jnp.exp(-jnp.abs(z)))
    logw = -jnp.exp(-softplus_neg - 0.5)
    a = _sigmoid(a0_ref[...] + _bdot(al, aup_ref[...]))
    g = _bdot(_sigmoid(gl), gup_ref[...])
    if has_vres:
        vl = x[:, 3 * w + 3 * LANES:3 * w + 4 * LANES]
        v = v + (vf_ref[...] - v) * _sigmoid(v0_ref[...] + _bdot(vl, vup_ref[...]))
    else:
        vf_out_ref[...] = v

    li = lax.broadcasted_iota(jnp.int32, (LANES, LANES), 0)
    lj = lax.broadcasted_iota(jnp.int32, (LANES, LANES), 1)
    pair_bd = ((li < A_HEAD_DIM) == (lj < A_HEAD_DIM)).astype(F32)
    lane = lax.broadcasted_iota(jnp.int32, (1, LANES), 1)
    head_mask = [(lane < A_HEAD_DIM).astype(F32), (lane >= A_HEAD_DIM).astype(F32)]
    first_head = lane < A_HEAD_DIM

    def head_sum(t):
        return jnp.concatenate([_hdot(t[:, p * LANES:(p + 1) * LANES], pair_bd) for p in range(w // LANES)],
                               axis=1)

    kkr = k * kk_ref[...]
    kk = kkr / jnp.maximum(jnp.sqrt(head_sum(kkr * kkr)), 1e-12)
    k = k * (1.0 + (a - 1.0) * ka_ref[...])

    ti = lax.broadcasted_iota(jnp.int32, (c, c), 0)
    tj = lax.broadcasted_iota(jnp.int32, (c, c), 1)
    incl = (tj <= ti).astype(F32)
    strict = (tj < ti).astype(F32)
    eye = (tj == ti).astype(F32)
    blk16 = ((ti // 16) == (tj // 16)).astype(F32)
    lvl1 = (((ti // 32) == (tj // 32)) & ((ti // 16) != (tj // 16))).astype(F32)
    lvl2 = ((ti // 32) != (tj // 32)).astype(F32)

    lb = _hdot(incl, logw)
    lb_last = lb[c - 1:c, :]
    e_pos = jnp.exp(lb)
    e_neg = jnp.exp(-lb)
    e_end = jnp.exp(lb_last - lb)
    alpha_t = -kk * jnp.exp(lb - logw)
    beta = kk * a
    beta_h = beta * e_neg
    k_h = k * e_neg
    r_t = r * e_pos
    beta_e = beta * e_end
    k_e = k * e_end
    gamma_c = jnp.exp(lb_last)

    y_pairs = []
    for p in range(w // LANES):
        sl = slice(p * LANES, (p + 1) * LANES)
        xa, xr, xb, xk, vp = alpha_t[:, sl], r_t[:, sl], beta_h[:, sl], k_h[:, sl], v[:, sl]
        wt_h, ut_h, a_rb, a_rk = [], [], [], []
        for hh in range(2):
            m = head_mask[hh]
            a_ab = _bdot_nt(xa * m, xb) * strict
            a_ak = _bdot_nt(xa * m, xk) * strict
            a_rb.append(_bdot_nt(xr * m, xb) * incl)
            a_rk.append(_bdot_nt(xr * m, xk) * incl)
            t = _tri_inverse(a_ab, eye, blk16, lvl1, lvl2)
            wt_h.append(_hdot(t, xa))
            ut_h.append(_hdot(t, _bdot(a_ak, vp)))
        wt = jnp.where(first_head, wt_h[0], wt_h[1])
        ut = jnp.where(first_head, ut_h[0], ut_h[1])
        st = state_ref[p]
        u = _bdot_nt(wt, st) + ut
        y0 = _bdot(a_rk[0], vp) + _bdot(a_rb[0], u)
        y1 = _bdot(a_rk[1], vp) + _bdot(a_rb[1], u)
        y_pairs.append(_bdot_nt(xr, st) + jnp.where(first_head, y0, y1))
        vu_t = jnp.concatenate([vp, u], axis=0).T
        ke_be = jnp.concatenate([k_e[:, sl], beta_e[:, sl]], axis=0)
        state_ref[p] = st * gamma_c[:, sl] + pair_bd * _bdot(vu_t, ke_be)
    y = jnp.concatenate(y_pairs, axis=1)

    inv_n = 1.0 / A_HEAD_DIM
    mean = head_sum(y) * inv_n
    yc = y - mean
    var = head_sum(yc * yc) * inv_n
    yn = yc * lax.rsqrt(var + A_GN_EPS) * gng_ref[...] + gnb_ref[...]
    bonus = head_sum(r * k * rk_ref[...]) * v
    y_ref[...] = (yn + bonus) * g


def _rwkv(x, p, v_first):
    s = x.shape[0]
    c = A_CHUNK
    has_vres = v_first is not None
    row = lambda n: pl.BlockSpec((c, n), lambda i: (i, 0))
    const = lambda a: pl.BlockSpec(a.shape, lambda i: (0,) * a.ndim)
    names = ["w0", "w_up", "a0", "a_up", "g_up", "k_k", "k_a", "r_k", "gn_g", "gn_b"]
    if has_vres:
        names += ["v0", "vres_up"]
    consts = [p[n] for n in names]
    ins = [x] + ([v_first] if has_vres else []) + consts
    in_specs = [row(A_IN)] + ([row(A_WIDTH)] if has_vres else []) + [const(a) for a in consts]
    y_shape = jax.ShapeDtypeStruct((s, A_WIDTH), F32)
    out = pl.pallas_call(
        functools.partial(_rwkv_body, has_vres),
        grid=(s // c,), in_specs=in_specs,
        out_specs=row(A_WIDTH) if has_vres else [row(A_WIDTH), row(A_WIDTH)],
        out_shape=y_shape if has_vres else [y_shape, y_shape],
        scratch_shapes=[pltpu.VMEM((A_WIDTH // LANES, LANES, LANES), F32)],
        compiler_params=_params("arbitrary"), name="rwkv7",
    )(*ins)
    return (out, v_first) if has_vres else (out[0], out[1])


def _mla_prep_body(x_ref, cq_ref, sq_ref, ck_ref, sk_ref, qg_ref, wq_ref, wqs_ref, kg_ref, wk_ref, wv_ref,
                   q_ref, k_ref, v_ref):
    x = x_ref[...]
    cq = x[:, 0:B_Q_RANK]
    ckv = x[:, B_Q_RANK:B_Q_RANK + B_KV_RANK]
    kr = x[:, B_Q_RANK + B_KV_RANK:B_Q_RANK + B_KV_RANK + LANES]
    kr_sw = x[:, B_Q_RANK + B_KV_RANK + LANES:B_Q_RANK + B_KV_RANK + 2 * LANES]
    cqn = _rms(cq, qg_ref[...]).astype(BF16)
    ckn = _rms(ckv, kg_ref[...]).astype(BF16)
    q = jnp.dot(cqn, wq_ref[...], preferred_element_type=F32)
    q_sw = jnp.dot(cqn, wqs_ref[...], preferred_element_type=F32)
    k_nope = jnp.dot(ckn, wk_ref[...], preferred_element_type=F32)
    val = jnp.dot(ckn, wv_ref[...], preferred_element_type=F32)
    k_rope = kr * ck_ref[...] + kr_sw * sk_ref[...]
    cq_t, sq_t = cq_ref[...], sq_ref[...]
    for hd in range(B_HEADS):
        sl = slice(hd * LANES, (hd + 1) * LANES)
        q_ref[hd] = (q[:, sl] * cq_t + q_sw[:, sl] * sq_t).astype(BF16)
        k_ref[hd] = (k_nope[:, sl] + k_rope).astype(BF16)
        v_ref[hd] = val[:, sl].astype(BF16)


def _mla_prep(x, tabs, p):
    s = x.shape[0]
    tm = ROW_TILE
    row = lambda n: pl.BlockSpec((tm, n), lambda i: (i, 0))
    const = lambda a: pl.BlockSpec(a.shape, lambda i: (0,) * a.ndim)
    consts = [p["q_norm_g"], p["w_q"], p["w_q_swap"], p["kv_norm_g"], p["w_k"], p["w_v"]]
    head_spec = pl.BlockSpec((B_HEADS, tm, LANES), lambda i: (0, i, 0))
    head_shape = jax.ShapeDtypeStruct((B_HEADS, s, LANES), BF16)
    return pl.pallas_call(
        _mla_prep_body, grid=(s // tm,),
        in_specs=[row(B_IN)] + [row(LANES)] * 4 + [const(a) for a in consts],
        out_specs=[head_spec] * 3, out_shape=[head_shape] * 3,
        compiler_params=_params("arbitrary"), name="mla_prep",
    )(x, *tabs, *consts)


def _flash_body(q_ref, k_ref, v_ref, o_ref, m_ref, l_ref, acc_ref):
    i = pl.program_id(1)
    j = pl.program_id(2)

    @pl.when(j == 0)
    def _():
        m_ref[...] = jnp.full_like(m_ref, -jnp.inf)
        l_ref[...] = jnp.zeros_like(l_ref)
        acc_ref[...] = jnp.zeros_like(acc_ref)

    @pl.when(j <= i)
    def _():
        s = lax.dot_general(q_ref[0], k_ref[0], (((1,), (1,)), ((), ())), preferred_element_type=F32)
        rows = lax.broadcasted_iota(jnp.int32, s.shape, 0)
        cols = lax.broadcasted_iota(jnp.int32, s.shape, 1)
        s = jnp.where(cols + j * s.shape[1] <= rows + i * s.shape[0], s, -jnp.inf)
        m_old = m_ref[...]
        m_new = jnp.maximum(m_old, jnp.max(s, axis=-1, keepdims=True))
        p = jnp.exp(s - m_new)
        scale = jnp.exp(m_old - m_new)
        l_ref[...] = scale * l_ref[...] + jnp.sum(p, axis=-1, keepdims=True)
        acc_ref[...] = scale * acc_ref[...] + jnp.dot(p.astype(BF16), v_ref[0], preferred_element_type=F32)
        m_ref[...] = m_new

    @pl.when(j == i)
    def _():
        o_ref[...] = acc_ref[...] / l_ref[...]


def _flash(q, k, v, tile):
    n_h, s, _ = q.shape
    n_t = s // tile
    return pl.pallas_call(
        _flash_body, grid=(n_h, n_t, n_t),
        in_specs=[
            pl.BlockSpec((1, tile, LANES), lambda h, i, j: (h, i, 0)),
            pl.BlockSpec((1, tile, LANES), lambda h, i, j: (h, jnp.minimum(i, j), 0)),
            pl.BlockSpec((1, tile, LANES), lambda h, i, j: (h, jnp.minimum(i, j), 0)),
        ],
        out_specs=pl.BlockSpec((tile, LANES), lambda h, i, j: (i, h)),
        out_shape=jax.ShapeDtypeStruct((s, n_h * LANES), F32),
        scratch_shapes=[pltpu.VMEM((tile, 1), F32), pltpu.VMEM((tile, 1), F32), pltpu.VMEM((tile, LANES), F32)],
        compiler_params=_params("arbitrary", "arbitrary", "arbitrary"), name="mla_flash",
    )(q, k, v)


def _hgrn_body(x_ref, lb_ref, ng_ref, o_ref, state_ref):
    @pl.when(pl.program_id(0) == 0)
    def _():
        state_ref[...] = jnp.zeros_like(state_ref)

    c = C_CHUNK
    sb = C_SUB
    w = C_WIDTH
    x = x_ref[...]
    lower = lb_ref[...]
    fz = x[:, w:2 * w]
    f = lower + (1.0 - lower) * _sigmoid(fz)
    logf = jnp.log(jnp.maximum(f, C_MIN_FORGET))
    kf = (1.0 - lower) * _sigmoid(-fz)
    cq = x[:, 0:w]
    q = cq * _sigmoid(cq)
    val = x[:, 2 * w:3 * w]
    cg = x[:, 3 * w:4 * w]

    ti = lax.broadcasted_iota(jnp.int32, (c, c), 0)
    tj = lax.broadcasted_iota(jnp.int32, (c, c), 1)
    b = _hdot((tj <= ti).astype(F32), logf)
    b_last = b[c - 1:c, :]
    q_in = q * jnp.exp(b)
    k_end = kf * jnp.exp(b_last - b)
    g_end = jnp.exp(b_last)

    s3 = lax.broadcasted_iota(jnp.int32, (sb, sb, 1), 1)
    t3 = lax.broadcasted_iota(jnp.int32, (sb, sb, 1), 0)
    causal3 = s3 <= t3

    outs = []
    for hd in range(C_HEADS):
        sl = slice(hd * C_DIM, (hd + 1) * C_DIM)
        bh, qh, kh, vh = b[:, sl], q[:, sl], kf[:, sl], val[:, sl]
        st = state_ref[hd]
        o_blocks = []
        for i in range(c // sb):
            rs = slice(i * sb, (i + 1) * sb)
            bi, qi, ki, vi = bh[rs], qh[rs], kh[rs], vh[rs]
            diff = bi[:, None, :] - bi[None, :, :]
            dec = jnp.where(causal3, jnp.exp(jnp.where(causal3, diff, 0.0)), 0.0)
            wts = jnp.sum(qi[:, None, :] * ki[None, :, :] * dec, axis=-1, keepdims=True)
            o_i = jnp.sum(wts * vi[None, :, :], axis=1)
            if i > 0:
                ref = bh[i * sb - 1:i * sb]
                q_rel = qi * jnp.exp(bi - ref)
                k_rel = kh[0:i * sb] * jnp.exp(ref - bh[0:i * sb])
                o_i = o_i + _bdot(_bdot_nt(q_rel, k_rel), vh[0:i * sb])
            o_blocks.append(o_i)
        o = jnp.concatenate(o_blocks, axis=0) + _bdot_nt(q_in[:, sl], st)
        state_ref[hd] = st * g_end[:, sl] + _bdot(vh.T, k_end[:, sl])
        o = _rms(o, ng_ref[...])
        g = cg[:, sl]
        outs.append(o * (g * _sigmoid(g)))
    o_ref[...] = jnp.concatenate(outs, axis=1)


def _hgrn(x, lower, norm_g):
    s = x.shape[0]
    c = C_CHUNK
    return pl.pallas_call(
        _hgrn_body, grid=(s // c,),
        in_specs=[pl.BlockSpec((c, C_IN), lambda i: (i, 0)), pl.BlockSpec((1, C_WIDTH), lambda i: (0, 0)),
                  pl.BlockSpec((1, C_DIM), lambda i: (0, 0))],
        out_specs=pl.BlockSpec((c, C_WIDTH), lambda i: (i, 0)),
        out_shape=jax.ShapeDtypeStruct((s, C_WIDTH), F32),
        scratch_shapes=[pltpu.VMEM((C_HEADS, C_DIM, C_DIM), F32)],
        compiler_params=_params("arbitrary"), name="hgrn2",
    )(x, lower, norm_g)


def _merge_body(h_ref, ya_ref, yb_ref, yc_ref, gl_ref, wa_ref, wb_ref, wc_ref, wo_ref, pg_ref, o_ref):
    d = D_MODEL
    gl = gl_ref[...]
    merged = (_sigmoid(gl[:, 0:d]) * _bdot(ya_ref[...], wa_ref[...])
              + _sigmoid(gl[:, d:2 * d]) * _bdot(yb_ref[...], wb_ref[...])
              + _sigmoid(gl[:, 2 * d:3 * d]) * _bdot(yc_ref[...], wc_ref[...]))
    o_ref[...] = h_ref[...] + _rms(_bdot(merged, wo_ref[...]), pg_ref[...])


def _merge(h, ya, yb, yc, gl, wa, wb, wc, wo, post_g):
    s = h.shape[0]
    tm = ROW_TILE
    row = lambda n: pl.BlockSpec((tm, n), lambda i: (i, 0))
    const = lambda a: pl.BlockSpec(a.shape, lambda i: (0,) * a.ndim)
    consts = [wa, wb, wc, wo, post_g]
    return pl.pallas_call(
        _merge_body, grid=(s // tm,),
        in_specs=[row(D_MODEL), row(A_WIDTH), row(B_HEADS * LANES), row(C_WIDTH), row(N_BRANCH * D_MODEL)]
        + [const(a) for a in consts],
        out_specs=row(D_MODEL), out_shape=jax.ShapeDtypeStruct((s, D_MODEL), F32),
        compiler_params=_params("arbitrary"), name="merge",
    )(h, ya, yb, yc, gl, *consts)


def _pad_cols(a, n):
    return jnp.pad(a, ((0, 0), (0, n - a.shape[1])))


def _pad_rows(a, n):
    return jnp.pad(a, ((0, n - a.shape[0]), (0, 0)))


def _rope_half_swap(a):
    half = B_ROPE // 2
    return jnp.concatenate([a[:, half:], a[:, :half]], axis=1)


def _layer_weights(l, w_in, rwkv_mu, vres_down, vres_mu, w_uq, w_ukv, mla_out):
    wi = w_in[l]
    d = D_MODEL
    o = 3 * A_WIDTH
    lora = [wi[:, o:o + 64], wi[:, o + 64:o + 128], wi[:, o + 128:o + 256]]
    mus = [rwkv_mu[l][o:o + 64], rwkv_mu[l][o + 64:o + 128], rwkv_mu[l][o + 128:o + 256]]
    if l > 0:
        lora.append(vres_down[l - 1])
        mus.append(vres_mu[l - 1])
    else:
        lora.append(jnp.zeros((d, 0), F32))
        mus.append(jnp.zeros((0,), F32))
    w_a = jnp.concatenate([wi[:, :o]] + [_pad_cols(t, LANES) for t in lora], axis=1)
    mu_a = jnp.concatenate([rwkv_mu[l][:o]] + [jnp.pad(t, (0, LANES - t.shape[0])) for t in mus])[None, :]

    o = A_COLS
    w_kr = wi[:, o + B_Q_RANK + B_KV_RANK:o + B_Q_RANK + B_KV_RANK + B_ROPE]
    place = lambda t: jnp.pad(t, ((0, 0), (B_NOPE, LANES - B_NOPE - B_ROPE)))
    w_b = jnp.concatenate([wi[:, o:o + B_Q_RANK + B_KV_RANK], place(w_kr), place(_rope_half_swap(w_kr))], axis=1)
    o += B_Q_RANK + B_KV_RANK + B_ROPE
    w_c = wi[:, o:o + C_IN]
    w_g = wi[:, o + C_IN:o + C_IN + N_BRANCH * d]

    scale = (B_NOPE + B_ROPE) ** -0.5
    uq = (w_uq[l] * scale).reshape(B_Q_RANK, B_HEADS, B_NOPE + B_ROPE)
    uq_sw = jnp.concatenate([jnp.zeros_like(uq[..., :B_NOPE]), uq[..., B_NOPE + B_ROPE // 2:],
                             uq[..., B_NOPE:B_NOPE + B_ROPE // 2]], axis=-1)
    pad_head = lambda t: jnp.pad(t, ((0, 0), (0, 0), (0, LANES - t.shape[-1]))).reshape(t.shape[0], B_HEADS * LANES)
    ukv = w_ukv[l].reshape(B_KV_RANK, B_HEADS, B_NOPE + B_V)
    mo = jnp.pad(mla_out[l].reshape(B_HEADS, B_V, d), ((0, 0), (0, LANES - B_V), (0, 0))).reshape(B_HEADS * LANES, d)
    bf = lambda t: t.astype(BF16)
    return dict(w_a=bf(w_a), mu_a=mu_a, w_b=bf(w_b), w_c=bf(w_c), w_g=bf(w_g),
                w_q=bf(pad_head(uq)), w_q_swap=bf(pad_head(uq_sw)),
                w_k=bf(pad_head(ukv[..., :B_NOPE])), w_v=bf(pad_head(ukv[..., B_NOPE:])), mla_out=bf(mo))


def _rope_tables(positions):
    inv_freq = ROPE_THETA ** (-jnp.arange(0, B_ROPE, 2, dtype=F32) / B_ROPE)
    ang = positions.astype(F32)[:, None] * inv_freq
    cos, sin = jnp.cos(ang), jnp.sin(ang)
    s = positions.shape[0]
    pad = jnp.zeros((s, LANES - B_NOPE - B_ROPE), F32)
    cos_q = jnp.concatenate([jnp.ones((s, B_NOPE), F32), cos, cos, pad], axis=1)
    sin_t = jnp.concatenate([jnp.zeros((s, B_NOPE), F32), -sin, sin, pad], axis=1)
    cos_k = jnp.concatenate([jnp.zeros((s, B_NOPE), F32), cos, cos, pad], axis=1)
    return cos_q, sin_t, cos_k, sin_t


def kernel(x, positions, ffn1_pre_g, ffn1_post_g, ffn1_w_gate, ffn1_w_up, ffn1_w_down, mix_pre_g, mix_post_g, w_in, rwkv_mu, rwkv_w0, rwkv_w_up, rwkv_a0, rwkv_a_up, rwkv_g_up, rwkv_k_k, rwkv_k_a, rwkv_r_k, rwkv_gn_g, rwkv_gn_b, rwkv_vres_down, rwkv_vres_mu, rwkv_vres_up, rwkv_v0, rwkv_out, mla_q_norm_g, mla_w_uq, mla_kv_norm_g, mla_w_ukv, mla_out, hgrn_lower_bounds, hgrn_norm_g, hgrn_out, w_o, ffn2_pre_g, ffn2_post_g, ffn2_w_gate, ffn2_w_up, ffn2_w_down):
    bsz, seq, d = x.shape
    assert bsz == 1 and d == D_MODEL and seq % max(ROW_TILE, C_CHUNK, A_CHUNK) == 0
    depth = w_in.shape[0]
    flash_tile = min(1024, seq)
    tabs = _rope_tables(positions[0])
    lb_p = jax.nn.softmax(hgrn_lower_bounds.astype(F32), axis=0)
    lower_bounds = jnp.cumsum(lb_p, axis=0) - lb_p[0]
    row = lambda t: t[None, :]
    bf = lambda t: t.astype(BF16)

    h = x[0]
    v_first = None
    for l in range(depth):
        h = _ffn(h, row(ffn1_pre_g[l]), bf(ffn1_w_gate[l]), bf(ffn1_w_up[l]), bf(ffn1_w_down[l]),
                 row(ffn1_post_g[l]))

        lw = _layer_weights(l, w_in, rwkv_mu, rwkv_vres_down, rwkv_vres_mu, mla_w_uq, mla_w_ukv, mla_out)
        pre_g = row(mix_pre_g[l])
        x_a = _proj(h, pre_g, lw["w_a"], lw["mu_a"])
        x_b = _proj(h, pre_g, lw["w_b"])
        x_c = _proj(h, pre_g, lw["w_c"])
        gate_logits = _proj(h, pre_g, lw["w_g"])

        pa = dict(w0=row(rwkv_w0[l]), w_up=bf(_pad_rows(rwkv_w_up[l], LANES)), a0=row(rwkv_a0[l]),
                  a_up=bf(_pad_rows(rwkv_a_up[l], LANES)), g_up=bf(rwkv_g_up[l]), k_k=row(rwkv_k_k[l]),
                  k_a=row(rwkv_k_a[l]), r_k=row(rwkv_r_k[l].reshape(-1)), gn_g=row(rwkv_gn_g[l]),
                  gn_b=row(rwkv_gn_b[l]))
        if l > 0:
            pa.update(v0=row(rwkv_v0[l - 1]), vres_up=bf(_pad_rows(rwkv_vres_up[l - 1], LANES)))
        y_a, v_first = _rwkv(x_a, pa, v_first)

        pb = dict(q_norm_g=row(mla_q_norm_g[l]), kv_norm_g=row(mla_kv_norm_g[l]), w_q=lw["w_q"],
                  w_q_swap=lw["w_q_swap"], w_k=lw["w_k"], w_v=lw["w_v"])
        q_h, k_h, v_h = _mla_prep(x_b, tabs, pb)
        y_b = _flash(q_h, k_h, v_h, flash_tile)

        y_c = _hgrn(x_c, row(lower_bounds[l]), row(hgrn_norm_g[l]))

        h = _merge(h, y_a, y_b, y_c, gate_logits, bf(rwkv_out[l]), lw["mla_out"], bf(hgrn_out[l]), bf(w_o[l]),
                   row(mix_post_g[l]))

        h = _ffn(h, row(ffn2_pre_g[l]), bf(ffn2_w_gate[l]), bf(ffn2_w_up[l]), bf(ffn2_w_down[l]),
                 row(ffn2_post_g[l]))
    return h[None]
```

```python
import functools

import jax
import jax.numpy as jnp
import numpy as np
from jax import lax
from jax.experimental import pallas as pl
from jax.experimental.pallas import tpu as pltpu

F32 = jnp.float32
BF16 = jnp.bfloat16

D_MODEL = 1024
D_FF = 2816
NORM_EPS = 1e-6
MACARON_WEIGHT = 0.5

A_HEADS = 8
A_HEAD_DIM = 64
A_WIDTH = 512
A_GN_EPS = 64e-5
A_CHUNK = 64
A_CHUNKS_PER_STEP = 2
A_COLS = 1792
A_IN = 2048

B_HEADS = 8
B_NOPE = 64
B_ROPE = 32
B_V = 64
B_Q_RANK = 384
B_KV_RANK = 256
B_IN = 896
ROPE_THETA = 10000.0
LANES = 128

C_HEADS = 4
C_DIM = 128
C_WIDTH = 512
C_IN = 2048
C_CHUNK = 128
C_SUB = 16
C_MIN_FORGET = 1e-6

N_BRANCH = 3
ROW_TILE = 512
VMEM_LIMIT = 56 * 1024 * 1024

def _bdot(a, b):
    return jnp.dot(a.astype(BF16), b.astype(BF16), preferred_element_type=F32)


def _bdot_nt(a, b):
    return lax.dot_general(a.astype(BF16), b.astype(BF16), (((1,), (1,)), ((), ())),
                           preferred_element_type=F32)


def _split_terms(data, parts):
    terms, rem = [], data
    for _ in range(parts):
        piece = rem.astype(BF16)
        terms.append(piece)
        rem = rem - piece.astype(F32)
    return terms


def _sel_dot(sel, data, parts):
    sel = sel.astype(BF16)
    return sum(jnp.dot(sel, t, preferred_element_type=F32) for t in _split_terms(data, parts))


def _dot_sel(data, sel, parts):
    sel = sel.astype(BF16)
    return sum(jnp.dot(t, sel, preferred_element_type=F32) for t in _split_terms(data, parts))


def _rms(x, g):
    return x * lax.rsqrt(jnp.mean(x * x, axis=-1, keepdims=True) + NORM_EPS) * g


def _sigmoid(x):
    return 1.0 / (1.0 + jnp.exp(-x))


def _params(*sem):
    return pltpu.CompilerParams(dimension_semantics=sem, vmem_limit_bytes=VMEM_LIMIT)


def _ffn_body(x_ref, pre_g_ref, wg_ref, wu_ref, wd_ref, post_g_ref, o_ref, xn_ref, acc_ref):
    j = pl.program_id(1)

    @pl.when(j == 0)
    def _():
        xn_ref[...] = _rms(x_ref[...], pre_g_ref[...]).astype(BF16)
        acc_ref[...] = jnp.zeros_like(acc_ref)

    xn = xn_ref[...]
    gate = jnp.dot(xn, wg_ref[...], preferred_element_type=F32)
    up = jnp.dot(xn, wu_ref[...], preferred_element_type=F32)
    mid = (gate * _sigmoid(gate) * up).astype(BF16)
    acc_ref[...] += jnp.dot(mid, wd_ref[...], preferred_element_type=F32)

    @pl.when(j == pl.num_programs(1) - 1)
    def _():
        o_ref[...] = x_ref[...] + MACARON_WEIGHT * _rms(acc_ref[...], post_g_ref[...])


def _ffn(h, pre_g, wg, wu, wd, post_g):
    s = h.shape[0]
    tf = D_FF // 2
    return pl.pallas_call(
        _ffn_body,
        grid=(s // ROW_TILE, D_FF // tf),
        in_specs=[
            pl.BlockSpec((ROW_TILE, D_MODEL), lambda i, j: (i, 0)),
            pl.BlockSpec((1, D_MODEL), lambda i, j: (0, 0)),
            pl.BlockSpec((D_MODEL, tf), lambda i, j: (0, j)),
            pl.BlockSpec((D_MODEL, tf), lambda i, j: (0, j)),
            pl.BlockSpec((tf, D_MODEL), lambda i, j: (j, 0)),
            pl.BlockSpec((1, D_MODEL), lambda i, j: (0, 0)),
        ],
        out_specs=pl.BlockSpec((ROW_TILE, D_MODEL), lambda i, j: (i, 0)),
        out_shape=jax.ShapeDtypeStruct((s, D_MODEL), F32),
        scratch_shapes=[pltpu.VMEM((ROW_TILE, D_MODEL), BF16), pltpu.VMEM((ROW_TILE, D_MODEL), F32)],
        compiler_params=_params("arbitrary", "arbitrary"),
        name="ffn",
    )(h, pre_g, wg, wu, wd, post_g)


def _proj_body(x_ref, g_ref, w_ref, o_ref):
    xn = _rms(x_ref[...], g_ref[...]).astype(BF16)
    o_ref[...] = jnp.dot(xn, w_ref[...], preferred_element_type=F32)


def _proj_shift_body(x_ref, g_ref, w_ref, mu_ref, o_ref, carry_ref):
    @pl.when(pl.program_id(0) == 0)
    def _():
        carry_ref[...] = jnp.zeros_like(carry_ref)

    xn = _rms(x_ref[...], g_ref[...]).astype(BF16)
    p = jnp.dot(xn, w_ref[...], preferred_element_type=F32)
    rows = p.shape[0]
    prev = pltpu.roll(p, 1, axis=0)
    first = lax.broadcasted_iota(jnp.int32, p.shape, 0) == 0
    prev = jnp.where(first, carry_ref[0:1, :], prev)
    carry_ref[0:1, :] = p[rows - 1:rows, :]
    o_ref[...] = p + (prev - p) * mu_ref[...]


def _proj(h, pre_g, w, mu=None):
    s = h.shape[0]
    n = w.shape[1]
    x_spec = pl.BlockSpec((ROW_TILE, D_MODEL), lambda i: (i, 0))
    g_spec = pl.BlockSpec((1, D_MODEL), lambda i: (0, 0))
    w_spec = pl.BlockSpec((D_MODEL, n), lambda i: (0, 0))
    o_spec = pl.BlockSpec((ROW_TILE, n), lambda i: (i, 0))
    out_shape = jax.ShapeDtypeStruct((s, n), F32)
    if mu is None:
        return pl.pallas_call(
            _proj_body, grid=(s // ROW_TILE,), in_specs=[x_spec, g_spec, w_spec], out_specs=o_spec,
            out_shape=out_shape, compiler_params=_params("arbitrary"), name="proj",
        )(h, pre_g, w)
    return pl.pallas_call(
        _proj_shift_body, grid=(s // ROW_TILE,),
        in_specs=[x_spec, g_spec, w_spec, pl.BlockSpec((1, n), lambda i: (0, 0))], out_specs=o_spec,
        out_shape=out_shape, scratch_shapes=[pltpu.VMEM((8, n), F32)],
        compiler_params=_params("arbitrary"), name="proj_shift",
    )(h, pre_g, w, mu)


def _tri_inverse(a_strict, eye, blk16, lvl1, lvl2):
    d = a_strict * blk16
    d2 = _bdot(d, d)
    d4 = _bdot(d2, d2)
    d8 = _bdot(d4, d4)
    t = eye + d
    t = t + _bdot(t, d2)
    t = t + _bdot(t, d4)
    t = t + _bdot(t, d8)
    t = t + _bdot(_bdot(t, a_strict * lvl1), t)
    t = t + _bdot(_bdot(t, a_strict * lvl2), t)
    return t


def _rwkv_body(has_vres, *refs):
    if has_vres:
        (x_ref, vf_ref, w0_ref, wup_ref, a0_ref, aup_ref, gup_ref, kk_ref, ka_ref, rk_ref, gng_ref, gnb_ref,
         v0_ref, vup_ref, y_ref, state_ref) = refs
    else:
        (x_ref, w0_ref, wup_ref, a0_ref, aup_ref, gup_ref, kk_ref, ka_ref, rk_ref, gng_ref, gnb_ref,
         y_ref, vf_out_ref, state_ref) = refs

    @pl.when(pl.program_id(0) == 0)
    def _():
        state_ref[...] = jnp.zeros_like(state_ref)

    c = A_CHUNK
    w = A_WIDTH
    hd = A_HEAD_DIM
    x = x_ref[...]
    rows = x.shape[0]
    r = x[:, 0:w]
    k = x[:, w:2 * w]
    v = x[:, 2 * w:3 * w]
    wl = x[:, 3 * w:3 * w + LANES]
    al = x[:, 3 * w + LANES:3 * w + 2 * LANES]
    gl = x[:, 3 * w + 2 * LANES:3 * w + 3 * LANES]

    z = w0_ref[...] + _bdot(jnp.tanh(wl), wup_ref[...])
    softplus_neg = jnp.maximum(-z, 0.0) + jnp.log(1.0 + jnp.exp(-jnp.abs(z)))
    logw = -jnp.exp(-softplus_neg - 0.5)
    a = _sigmoid(a0_ref[...] + _bdot(al, aup_ref[...]))
    g = _bdot(_sigmoid(gl), gup_ref[...])
    if has_vres:
        vl = x[:, 3 * w + 3 * LANES:3 * w + 4 * LANES]
        v = v + (vf_ref[...] - v) * _sigmoid(v0_ref[...] + _bdot(vl, vup_ref[...]))
    else:
        vf_out_ref[...] = v

    li = lax.broadcasted_iota(jnp.int32, (LANES, LANES), 0)
    lj = lax.broadcasted_iota(jnp.int32, (LANES, LANES), 1)
    pair_bd = ((li < hd) == (lj < hd)).astype(F32)
    lane = lax.broadcasted_iota(jnp.int32, (1, LANES), 1)
    m0 = (lane < hd).astype(F32)
    m1 = (lane >= hd).astype(F32)

    def head_sum(t):
        return jnp.concatenate([_dot_sel(t[:, p * LANES:(p + 1) * LANES], pair_bd, 2) for p in range(w // LANES)],
                               axis=1)

    def by_head(t):
        return jnp.concatenate([t * m0, t * m1], axis=0)

    kkr = k * kk_ref[...]
    kk = kkr / jnp.maximum(jnp.sqrt(head_sum(kkr * kkr)), 1e-12)
    k = k * (1.0 + (a - 1.0) * ka_ref[...])

    ti = lax.broadcasted_iota(jnp.int32, (2 * c, 2 * c), 0)
    tj = lax.broadcasted_iota(jnp.int32, (2 * c, 2 * c), 1)
    same_head = (ti // c) == (tj // c)
    incl = (same_head & (tj <= ti)).astype(F32)
    strict = (same_head & (tj < ti)).astype(F32)
    eye = (tj == ti).astype(F32)
    blk16 = ((ti // 16) == (tj // 16)).astype(F32)
    lvl1 = (((ti // 32) == (tj // 32)) & ((ti // 16) != (tj // 16))).astype(F32)
    lvl2 = (same_head & ((ti // 32) != (tj // 32))).astype(F32)

    ri = lax.broadcasted_iota(jnp.int32, (rows, rows), 0)
    rj = lax.broadcasted_iota(jnp.int32, (rows, rows), 1)
    chunk_incl = (((ri // c) == (rj // c)) & (rj <= ri)).astype(F32)
    lb = _sel_dot(chunk_incl, logw, 3)
    e_neg = jnp.exp(-lb)
    alpha_t = -kk * jnp.exp(lb - logw)
    beta = kk * a
    beta_h = beta * e_neg
    k_h = k * e_neg
    r_t = r * jnp.exp(lb)

    y_chunks = []
    for ci in range(rows // c):
        rs = slice(ci * c, (ci + 1) * c)
        lb_last = lb[(ci + 1) * c - 1:(ci + 1) * c, :]
        e_end = jnp.exp(lb_last - lb[rs])
        beta_e = beta[rs] * e_end
        k_e = k[rs] * e_end
        gamma_c = jnp.exp(lb_last)
        y_pairs = []
        for p in range(w // LANES):
            sl = slice(p * LANES, (p + 1) * LANES)
            xr, vp = r_t[rs, sl], v[rs, sl]
            xa2, v2 = by_head(alpha_t[rs, sl]), by_head(vp)
            gram = _bdot_nt(jnp.concatenate([xa2, by_head(xr)], axis=0),
                            jnp.concatenate([by_head(beta_h[rs, sl]), by_head(k_h[rs, sl])], axis=0))
            a_ab = gram[0:2 * c, 0:2 * c] * strict
            a_ak = gram[0:2 * c, 2 * c:4 * c] * strict
            a_rb = gram[2 * c:4 * c, 0:2 * c] * incl
            a_rk = gram[2 * c:4 * c, 2 * c:4 * c] * incl
            t = _tri_inverse(a_ab, eye, blk16, lvl1, lvl2)
            wu = _bdot(t, jnp.concatenate([xa2, _bdot(a_ak, v2)], axis=1))
            wt = wu[0:c, 0:LANES] + wu[c:2 * c, 0:LANES]
            ut = wu[0:c, LANES:2 * LANES] + wu[c:2 * c, LANES:2 * LANES]
            st = state_ref[p]
            u = _bdot_nt(wt, st) + ut
            y2 = _bdot(jnp.concatenate([a_rk, a_rb], axis=1), jnp.concatenate([v2, by_head(u)], axis=0))
            y_pairs.append(_bdot_nt(xr, st) + y2[0:c] + y2[c:2 * c])
            vu_t = jnp.concatenate([vp, u], axis=0).T
            ke_be = jnp.concatenate([k_e[:, sl], beta_e[:, sl]], axis=0)
            state_ref[p] = st * gamma_c[:, sl] + pair_bd * _bdot(vu_t, ke_be)
        y_chunks.append(jnp.concatenate(y_pairs, axis=1))
    y = jnp.concatenate(y_chunks, axis=0)

    inv_n = 1.0 / hd
    mean = head_sum(y) * inv_n
    yc = y - mean
    var = head_sum(yc * yc) * inv_n
    yn = yc * lax.rsqrt(var + A_GN_EPS) * gng_ref[...] + gnb_ref[...]
    bonus = head_sum(r * k * rk_ref[...]) * v
    y_ref[...] = (yn + bonus) * g


def _rwkv(x, p, v_first):
    s = x.shape[0]
    rows = A_CHUNK * A_CHUNKS_PER_STEP
    has_vres = v_first is not None
    row = lambda n: pl.BlockSpec((rows, n), lambda i: (i, 0))
    const = lambda a: pl.BlockSpec(a.shape, lambda i: (0,) * a.ndim)
    names = ["w0", "w_up", "a0", "a_up", "g_up", "k_k", "k_a", "r_k", "gn_g", "gn_b"]
    if has_vres:
        names += ["v0", "vres_up"]
    consts = [p[n] for n in names]
    ins = [x] + ([v_first] if has_vres else []) + consts
    in_specs = [row(A_IN)] + ([row(A_WIDTH)] if has_vres else []) + [const(a) for a in consts]
    y_shape = jax.ShapeDtypeStruct((s, A_WIDTH), F32)
    out = pl.pallas_call(
        functools.partial(_rwkv_body, has_vres),
        grid=(s // rows,), in_specs=in_specs,
        out_specs=row(A_WIDTH) if has_vres else [row(A_WIDTH), row(A_WIDTH)],
        out_shape=y_shape if has_vres else [y_shape, y_shape],
        scratch_shapes=[pltpu.VMEM((A_WIDTH // LANES, LANES, LANES), F32)],
        compiler_params=_params("arbitrary"), name="rwkv7",
    )(*ins)
    return (out, v_first) if has_vres else (out[0], out[1])


def _mla_prep_body(x_ref, cq_ref, sq_ref, ck_ref, sk_ref, qg_ref, wq_ref, wqs_ref, kg_ref, wk_ref, wv_ref,
                   q_ref, k_ref, v_ref):
    x = x_ref[...]
    cq = x[:, 0:B_Q_RANK]
    ckv = x[:, B_Q_RANK:B_Q_RANK + B_KV_RANK]
    kr = x[:, B_Q_RANK + B_KV_RANK:B_Q_RANK + B_KV_RANK + LANES]
    kr_sw = x[:, B_Q_RANK + B_KV_RANK + LANES:B_Q_RANK + B_KV_RANK + 2 * LANES]
    cqn = _rms(cq, qg_ref[...]).astype(BF16)
    ckn = _rms(ckv, kg_ref[...]).astype(BF16)
    q = jnp.dot(cqn, wq_ref[...], preferred_element_type=F32)
    q_sw = jnp.dot(cqn, wqs_ref[...], preferred_element_type=F32)
    k_nope = jnp.dot(ckn, wk_ref[...], preferred_element_type=F32)
    val = jnp.dot(ckn, wv_ref[...], preferred_element_type=F32)
    k_rope = kr * ck_ref[...] + kr_sw * sk_ref[...]
    cq_t, sq_t = cq_ref[...], sq_ref[...]
    ones_col = (lax.broadcasted_iota(jnp.int32, (1, LANES), 1) == B_V).astype(F32)
    for hd in range(B_HEADS):
        sl = slice(hd * LANES, (hd + 1) * LANES)
        q_ref[hd] = (q[:, sl] * cq_t + q_sw[:, sl] * sq_t).astype(BF16)
        k_ref[hd] = (k_nope[:, sl] + k_rope).astype(BF16)
        v_ref[hd] = (val[:, sl] + ones_col).astype(BF16)


def _mla_prep(x, tabs, p):
    s = x.shape[0]
    tm = ROW_TILE
    row = lambda n: pl.BlockSpec((tm, n), lambda i: (i, 0))
    const = lambda a: pl.BlockSpec(a.shape, lambda i: (0,) * a.ndim)
    consts = [p["q_norm_g"], p["w_q"], p["w_q_swap"], p["kv_norm_g"], p["w_k"], p["w_v"]]
    head_spec = pl.BlockSpec((B_HEADS, tm, LANES), lambda i: (0, i, 0))
    head_shape = jax.ShapeDtypeStruct((B_HEADS, s, LANES), BF16)
    return pl.pallas_call(
        _mla_prep_body, grid=(s // tm,),
        in_specs=[row(B_IN)] + [row(LANES)] * 4 + [const(a) for a in consts],
        out_specs=[head_spec] * 3, out_shape=[head_shape] * 3,
        compiler_params=_params("arbitrary"), name="mla_prep",
    )(x, *tabs, *consts)


def _flash_body(qi_ref, kj_ref, q_ref, k_ref, v_ref, o_ref, m_ref, acc_ref):
    t = pl.program_id(1)
    i = qi_ref[t]
    j = kj_ref[t]

    @pl.when(j == 0)
    def _():
        m_ref[...] = jnp.full_like(m_ref, -jnp.inf)
        acc_ref[...] = jnp.zeros_like(acc_ref)

    def step(on_diagonal):
        s = lax.dot_general(q_ref[0], k_ref[0], (((1,), (1,)), ((), ())), preferred_element_type=F32)
        if on_diagonal:
            rows = lax.broadcasted_iota(jnp.int32, s.shape, 0)
            cols = lax.broadcasted_iota(jnp.int32, s.shape, 1)
            s = jnp.where(cols <= rows, s, -jnp.inf)
        m_old = m_ref[...]
        m_new = jnp.maximum(m_old, jnp.max(s, axis=-1, keepdims=True))
        p = jnp.exp2(s - m_new).astype(BF16)
        acc_ref[...] = jnp.exp2(m_old - m_new) * acc_ref[...] + jnp.dot(p, v_ref[0], preferred_element_type=F32)
        m_ref[...] = m_new

    @pl.when(j < i)
    def _():
        step(False)

    @pl.when(j == i)
    def _():
        step(True)
        acc = acc_ref[...]
        o_ref[...] = acc / acc[:, B_V:B_V + 1]


def _flash(q, k, v, tile):
    n_h, s, _ = q.shape
    n_t = s // tile
    qi = np.array([i for i in range(n_t) for _ in range(i + 1)], np.int32)
    kj = np.array([j for i in range(n_t) for j in range(i + 1)], np.int32)
    grid_spec = pltpu.PrefetchScalarGridSpec(
        num_scalar_prefetch=2, grid=(n_h, len(qi)),
        in_specs=[
            pl.BlockSpec((1, tile, LANES), lambda h, t, qi, kj: (h, qi[t], 0)),
            pl.BlockSpec((1, tile, LANES), lambda h, t, qi, kj: (h, kj[t], 0)),
            pl.BlockSpec((1, tile, LANES), lambda h, t, qi, kj: (h, kj[t], 0)),
        ],
        out_specs=pl.BlockSpec((tile, LANES), lambda h, t, qi, kj: (qi[t], h)),
        scratch_shapes=[pltpu.VMEM((tile, 1), F32), pltpu.VMEM((tile, LANES), F32)],
    )
    return pl.pallas_call(
        _flash_body, grid_spec=grid_spec,
        out_shape=jax.ShapeDtypeStruct((s, n_h * LANES), F32),
        compiler_params=_params("arbitrary", "arbitrary"), name="mla_flash",
    )(jnp.asarray(qi), jnp.asarray(kj), q, k, v)


def _hgrn_body(x_ref, lb_ref, ng_ref, o_ref, state_ref):
    @pl.when(pl.program_id(0) == 0)
    def _():
        state_ref[...] = jnp.zeros_like(state_ref)

    c = C_CHUNK
    sb = C_SUB
    w = C_WIDTH
    x = x_ref[...]
    lower = lb_ref[...]
    fz = x[:, w:2 * w]
    f = lower + (1.0 - lower) * _sigmoid(fz)
    logf = jnp.log(jnp.maximum(f, C_MIN_FORGET))
    kf = (1.0 - lower) * _sigmoid(-fz)
    cq = x[:, 0:w]
    q = cq * _sigmoid(cq)
    val = x[:, 2 * w:3 * w]
    cg = x[:, 3 * w:4 * w]

    ti = lax.broadcasted_iota(jnp.int32, (c, c), 0)
    tj = lax.broadcasted_iota(jnp.int32, (c, c), 1)
    b = _sel_dot((tj <= ti).astype(F32), logf, 3)
    b_last = b[c - 1:c, :]
    q_in = q * jnp.exp(b)
    k_end = kf * jnp.exp(b_last - b)
    g_end = jnp.exp(b_last)

    s3 = lax.broadcasted_iota(jnp.int32, (sb, sb, 1), 1)
    t3 = lax.broadcasted_iota(jnp.int32, (sb, sb, 1), 0)
    causal3 = s3 <= t3

    outs = []
    for hd in range(C_HEADS):
        sl = slice(hd * C_DIM, (hd + 1) * C_DIM)
        bh, qh, kh, vh = b[:, sl], q[:, sl], kf[:, sl], val[:, sl]
        st = state_ref[hd]
        o_blocks = []
        for i in range(c // sb):
            rs = slice(i * sb, (i + 1) * sb)
            bi, qi, ki, vi = bh[rs], qh[rs], kh[rs], vh[rs]
            diff = bi[:, None, :] - bi[None, :, :]
            dec = jnp.where(causal3, jnp.exp(jnp.where(causal3, diff, 0.0)), 0.0)
            wts = jnp.sum(qi[:, None, :] * ki[None, :, :] * dec, axis=-1, keepdims=True)
            o_i = jnp.sum(wts * vi[None, :, :], axis=1)
            if i > 0:
                ref = bh[i * sb - 1:i * sb]
                q_rel = qi * jnp.exp(bi - ref)
                k_rel = kh[0:i * sb] * jnp.exp(ref - bh[0:i * sb])
                o_i = o_i + _bdot(_bdot_nt(q_rel, k_rel), vh[0:i * sb])
            o_blocks.append(o_i)
        o = jnp.concatenate(o_blocks, axis=0) + _bdot_nt(q_in[:, sl], st)
        state_ref[hd] = st * g_end[:, sl] + _bdot(vh.T, k_end[:, sl])
        o = _rms(o, ng_ref[...])
        g = cg[:, sl]
        outs.append(o * (g * _sigmoid(g)))
    o_ref[...] = jnp.concatenate(outs, axis=1)


def _hgrn(x, lower, norm_g):
    s = x.shape[0]
    c = C_CHUNK
    return pl.pallas_call(
        _hgrn_body, grid=(s // c,),
        in_specs=[pl.BlockSpec((c, C_IN), lambda i: (i, 0)), pl.BlockSpec((1, C_WIDTH), lambda i: (0, 0)),
                  pl.BlockSpec((1, C_DIM), lambda i: (0, 0))],
        out_specs=pl.BlockSpec((c, C_WIDTH), lambda i: (i, 0)),
        out_shape=jax.ShapeDtypeStruct((s, C_WIDTH), F32),
        scratch_shapes=[pltpu.VMEM((C_HEADS, C_DIM, C_DIM), F32)],
        compiler_params=_params("arbitrary"), name="hgrn2",
    )(x, lower, norm_g)


def _merge_body(h_ref, ya_ref, yb_ref, yc_ref, gl_ref, wa_ref, wb_ref, wc_ref, wo_ref, pg_ref, o_ref):
    d = D_MODEL
    gl = gl_ref[...]
    merged = (_sigmoid(gl[:, 0:d]) * _bdot(ya_ref[...], wa_ref[...])
              + _sigmoid(gl[:, d:2 * d]) * _bdot(yb_ref[...], wb_ref[...])
              + _sigmoid(gl[:, 2 * d:3 * d]) * _bdot(yc_ref[...], wc_ref[...]))
    o_ref[...] = h_ref[...] + _rms(_bdot(merged, wo_ref[...]), pg_ref[...])


def _merge(h, ya, yb, yc, gl, wa, wb, wc, wo, post_g):
    s = h.shape[0]
    tm = ROW_TILE
    row = lambda n: pl.BlockSpec((tm, n), lambda i: (i, 0))
    const = lambda a: pl.BlockSpec(a.shape, lambda i: (0,) * a.ndim)
    consts = [wa, wb, wc, wo, post_g]
    return pl.pallas_call(
        _merge_body, grid=(s // tm,),
        in_specs=[row(D_MODEL), row(A_WIDTH), row(B_HEADS * LANES), row(C_WIDTH), row(N_BRANCH * D_MODEL)]
        + [const(a) for a in consts],
        out_specs=row(D_MODEL), out_shape=jax.ShapeDtypeStruct((s, D_MODEL), F32),
        compiler_params=_params("arbitrary"), name="merge",
    )(h, ya, yb, yc, gl, *consts)


def _pad_cols(a, n):
    return jnp.pad(a, ((0, 0), (0, n - a.shape[1])))


def _pad_rows(a, n):
    return jnp.pad(a, ((0, n - a.shape[0]), (0, 0)))


def _rope_half_swap(a):
    half = B_ROPE // 2
    return jnp.concatenate([a[:, half:], a[:, :half]], axis=1)


def _layer_weights(l, w_in, rwkv_mu, vres_down, vres_mu, w_uq, w_ukv, mla_out):
    wi = w_in[l]
    d = D_MODEL
    o = 3 * A_WIDTH
    lora = [wi[:, o:o + 64], wi[:, o + 64:o + 128], wi[:, o + 128:o + 256]]
    mus = [rwkv_mu[l][o:o + 64], rwkv_mu[l][o + 64:o + 128], rwkv_mu[l][o + 128:o + 256]]
    if l > 0:
        lora.append(vres_down[l - 1])
        mus.append(vres_mu[l - 1])
    else:
        lora.append(jnp.zeros((d, 0), F32))
        mus.append(jnp.zeros((0,), F32))
    w_a = jnp.concatenate([wi[:, :o]] + [_pad_cols(t, LANES) for t in lora], axis=1)
    mu_a = jnp.concatenate([rwkv_mu[l][:o]] + [jnp.pad(t, (0, LANES - t.shape[0])) for t in mus])[None, :]

    o = A_COLS
    w_kr = wi[:, o + B_Q_RANK + B_KV_RANK:o + B_Q_RANK + B_KV_RANK + B_ROPE]
    place = lambda t: jnp.pad(t, ((0, 0), (B_NOPE, LANES - B_NOPE - B_ROPE)))
    w_b = jnp.concatenate([wi[:, o:o + B_Q_RANK + B_KV_RANK], place(w_kr), place(_rope_half_swap(w_kr))], axis=1)
    o += B_Q_RANK + B_KV_RANK + B_ROPE
    w_c = wi[:, o:o + C_IN]
    w_g = wi[:, o + C_IN:o + C_IN + N_BRANCH * d]

    scale = (B_NOPE + B_ROPE) ** -0.5 * np.log2(np.e)
    uq = (w_uq[l] * scale).reshape(B_Q_RANK, B_HEADS, B_NOPE + B_ROPE)
    uq_sw = jnp.concatenate([jnp.zeros_like(uq[..., :B_NOPE]), uq[..., B_NOPE + B_ROPE // 2:],
                             uq[..., B_NOPE:B_NOPE + B_ROPE // 2]], axis=-1)
    pad_head = lambda t: jnp.pad(t, ((0, 0), (0, 0), (0, LANES - t.shape[-1]))).reshape(t.shape[0], B_HEADS * LANES)
    ukv = w_ukv[l].reshape(B_KV_RANK, B_HEADS, B_NOPE + B_V)
    mo = jnp.pad(mla_out[l].reshape(B_HEADS, B_V, d), ((0, 0), (0, LANES - B_V), (0, 0))).reshape(B_HEADS * LANES, d)
    bf = lambda t: t.astype(BF16)
    return dict(w_a=bf(w_a), mu_a=mu_a, w_b=bf(w_b), w_c=bf(w_c), w_g=bf(w_g),
                w_q=bf(pad_head(uq)), w_q_swap=bf(pad_head(uq_sw)),
                w_k=bf(pad_head(ukv[..., :B_NOPE])), w_v=bf(pad_head(ukv[..., B_NOPE:])), mla_out=bf(mo))


def _rope_tables(positions):
    inv_freq = ROPE_THETA ** (-jnp.arange(0, B_ROPE, 2, dtype=F32) / B_ROPE)
    ang = positions.astype(F32)[:, None] * inv_freq
    cos, sin = jnp.cos(ang), jnp.sin(ang)
    s = positions.shape[0]
    pad = jnp.zeros((s, LANES - B_NOPE - B_ROPE), F32)
    cos_q = jnp.concatenate([jnp.ones((s, B_NOPE), F32), cos, cos, pad], axis=1)
    sin_t = jnp.concatenate([jnp.zeros((s, B_NOPE), F32), -sin, sin, pad], axis=1)
    cos_k = jnp.concatenate([jnp.zeros((s, B_NOPE), F32), cos, cos, pad], axis=1)
    return cos_q, sin_t, cos_k, sin_t


def kernel(x, positions, ffn1_pre_g, ffn1_post_g, ffn1_w_gate, ffn1_w_up, ffn1_w_down, mix_pre_g, mix_post_g, w_in, rwkv_mu, rwkv_w0, rwkv_w_up, rwkv_a0, rwkv_a_up, rwkv_g_up, rwkv_k_k, rwkv_k_a, rwkv_r_k, rwkv_gn_g, rwkv_gn_b, rwkv_vres_down, rwkv_vres_mu, rwkv_vres_up, rwkv_v0, rwkv_out, mla_q_norm_g, mla_w_uq, mla_kv_norm_g, mla_w_ukv, mla_out, hgrn_lower_bounds, hgrn_norm_g, hgrn_out, w_o, ffn2_pre_g, ffn2_post_g, ffn2_w_gate, ffn2_w_up, ffn2_w_down):
    bsz, seq, d = x.shape
    assert bsz == 1 and d == D_MODEL and seq % max(ROW_TILE, C_CHUNK, A_CHUNK) == 0
    depth = w_in.shape[0]
    flash_tile = min(1024, seq)
    tabs = _rope_tables(positions[0])
    lb_p = jax.nn.softmax(hgrn_lower_bounds.astype(F32), axis=0)
    lower_bounds = jnp.cumsum(lb_p, axis=0) - lb_p[0]
    row = lambda t: t[None, :]
    bf = lambda t: t.astype(BF16)

    h = x[0]
    v_first = None
    for l in range(depth):
        h = _ffn(h, row(ffn1_pre_g[l]), bf(ffn1_w_gate[l]), bf(ffn1_w_up[l]), bf(ffn1_w_down[l]),
                 row(ffn1_post_g[l]))

        lw = _layer_weights(l, w_in, rwkv_mu, rwkv_vres_down, rwkv_vres_mu, mla_w_uq, mla_w_ukv, mla_out)
        pre_g = row(mix_pre_g[l])
        x_a = _proj(h, pre_g, lw["w_a"], lw["mu_a"])
        x_b = _proj(h, pre_g, lw["w_b"])
        x_c = _proj(h, pre_g, lw["w_c"])
        gate_logits = _proj(h, pre_g, lw["w_g"])

        pa = dict(w0=row(rwkv_w0[l]), w_up=bf(_pad_rows(rwkv_w_up[l], LANES)), a0=row(rwkv_a0[l]),
                  a_up=bf(_pad_rows(rwkv_a_up[l], LANES)), g_up=bf(rwkv_g_up[l]), k_k=row(rwkv_k_k[l]),
                  k_a=row(rwkv_k_a[l]), r_k=row(rwkv_r_k[l].reshape(-1)), gn_g=row(rwkv_gn_g[l]),
                  gn_b=row(rwkv_gn_b[l]))
        if l > 0:
            pa.update(v0=row(rwkv_v0[l - 1]), vres_up=bf(_pad_rows(rwkv_vres_up[l - 1], LANES)))
        y_a, v_first = _rwkv(x_a, pa, v_first)

        pb = dict(q_norm_g=row(mla_q_norm_g[l]), kv_norm_g=row(mla_kv_norm_g[l]), w_q=lw["w_q"],
                  w_q_swap=lw["w_q_swap"], w_k=lw["w_k"], w_v=lw["w_v"])
        q_h, k_h, v_h = _mla_prep(x_b, tabs, pb)
        y_b = _flash(q_h, k_h, v_h, flash_tile)

        y_c = _hgrn(x_c, row(lower_bounds[l]), row(hgrn_norm_g[l]))

        h = _merge(h, y_a, y_b, y_c, gate_logits, bf(rwkv_out[l]), lw["mla_out"], bf(hgrn_out[l]), bf(w_o[l]),
                   row(mix_post_g[l]))

        h = _ffn(h, row(ffn2_pre_g[l]), bf(ffn2_w_gate[l]), bf(ffn2_w_up[l]), bf(ffn2_w_down[l]),
                 row(ffn2_post_g[l]))
    return h[None]
```

```python
import functools

import jax
import jax.numpy as jnp
import numpy as np
from jax import lax
from jax.experimental import pallas as pl
from jax.experimental.pallas import tpu as pltpu

F32 = jnp.float32
BF16 = jnp.bfloat16

D_MODEL = 1024
D_FF = 2816
NORM_EPS = 1e-6
MACARON_WEIGHT = 0.5

A_HEADS = 8
A_HEAD_DIM = 64
A_WIDTH = 512
A_GN_EPS = 64e-5
A_CHUNK = 64
A_CHUNKS_PER_STEP = 4
A_COLS = 1792
A_IN = 2048

B_HEADS = 8
B_NOPE = 64
B_ROPE = 32
B_V = 64
B_Q_RANK = 384
B_KV_RANK = 256
B_IN = 896
ROPE_THETA = 10000.0
B_SUBTILES = 4
LANES = 128

C_HEADS = 4
C_DIM = 128
C_WIDTH = 512
C_IN = 2048
C_CHUNK = 128
C_SUB = 16
C_MIN_FORGET = 1e-6

N_BRANCH = 3
ROW_TILE = 512
VMEM_LIMIT = 56 * 1024 * 1024

def _bdot(a, b):
    return jnp.dot(a.astype(BF16), b.astype(BF16), preferred_element_type=F32)


def _bdot_nt(a, b):
    return lax.dot_general(a.astype(BF16), b.astype(BF16), (((1,), (1,)), ((), ())),
                           preferred_element_type=F32)


def _bmm(a, b):
    return lax.dot_general(a.astype(BF16), b.astype(BF16), (((2,), (1,)), ((0,), (0,))),
                           preferred_element_type=F32)


def _bmm_nt(a, b):
    return lax.dot_general(a.astype(BF16), b.astype(BF16), (((2,), (2,)), ((0,), (0,))),
                           preferred_element_type=F32)


def _split_terms(data, parts):
    terms, rem = [], data
    for _ in range(parts):
        piece = rem.astype(BF16)
        terms.append(piece)
        rem = rem - piece.astype(F32)
    return terms


def _sel_dot(sel, data, parts):
    sel = sel.astype(BF16)
    return sum(jnp.dot(sel, t, preferred_element_type=F32) for t in _split_terms(data, parts))


def _dot_sel(data, sel, parts):
    sel = sel.astype(BF16)
    return sum(jnp.dot(t, sel, preferred_element_type=F32) for t in _split_terms(data, parts))


def _rms(x, g):
    return x * lax.rsqrt(jnp.mean(x * x, axis=-1, keepdims=True) + NORM_EPS) * g


def _sigmoid(x):
    return 1.0 / (1.0 + jnp.exp(-x))


def _params(*sem):
    return pltpu.CompilerParams(dimension_semantics=sem, vmem_limit_bytes=VMEM_LIMIT)


def _ffn_body(x_ref, pre_g_ref, wg_ref, wu_ref, wd_ref, post_g_ref, o_ref, xn_ref, acc_ref):
    j = pl.program_id(1)

    @pl.when(j == 0)
    def _():
        xn_ref[...] = _rms(x_ref[...], pre_g_ref[...]).astype(BF16)
        acc_ref[...] = jnp.zeros_like(acc_ref)

    xn = xn_ref[...]
    gate = jnp.dot(xn, wg_ref[...], preferred_element_type=F32)
    up = jnp.dot(xn, wu_ref[...], preferred_element_type=F32)
    mid = (gate * _sigmoid(gate) * up).astype(BF16)
    acc_ref[...] += jnp.dot(mid, wd_ref[...], preferred_element_type=F32)

    @pl.when(j == pl.num_programs(1) - 1)
    def _():
        o_ref[...] = x_ref[...] + MACARON_WEIGHT * _rms(acc_ref[...], post_g_ref[...])


def _ffn(h, pre_g, wg, wu, wd, post_g):
    s = h.shape[0]
    tf = D_FF // 2
    return pl.pallas_call(
        _ffn_body,
        grid=(s // ROW_TILE, D_FF // tf),
        in_specs=[
            pl.BlockSpec((ROW_TILE, D_MODEL), lambda i, j: (i, 0)),
            pl.BlockSpec((1, D_MODEL), lambda i, j: (0, 0)),
            pl.BlockSpec((D_MODEL, tf), lambda i, j: (0, j)),
            pl.BlockSpec((D_MODEL, tf), lambda i, j: (0, j)),
            pl.BlockSpec((tf, D_MODEL), lambda i, j: (j, 0)),
            pl.BlockSpec((1, D_MODEL), lambda i, j: (0, 0)),
        ],
        out_specs=pl.BlockSpec((ROW_TILE, D_MODEL), lambda i, j: (i, 0)),
        out_shape=jax.ShapeDtypeStruct((s, D_MODEL), F32),
        scratch_shapes=[pltpu.VMEM((ROW_TILE, D_MODEL), BF16), pltpu.VMEM((ROW_TILE, D_MODEL), F32)],
        compiler_params=_params("arbitrary", "arbitrary"),
        name="ffn",
    )(h, pre_g, wg, wu, wd, post_g)


def _proj_body(x_ref, g_ref, w_ref, o_ref):
    xn = _rms(x_ref[...], g_ref[...]).astype(BF16)
    o_ref[...] = jnp.dot(xn, w_ref[...], preferred_element_type=F32)


def _proj_shift_body(x_ref, g_ref, w_ref, mu_ref, o_ref, carry_ref):
    @pl.when(pl.program_id(0) == 0)
    def _():
        carry_ref[...] = jnp.zeros_like(carry_ref)

    xn = _rms(x_ref[...], g_ref[...]).astype(BF16)
    p = jnp.dot(xn, w_ref[...], preferred_element_type=F32)
    rows = p.shape[0]
    prev = pltpu.roll(p, 1, axis=0)
    first = lax.broadcasted_iota(jnp.int32, p.shape, 0) == 0
    prev = jnp.where(first, carry_ref[0:1, :], prev)
    carry_ref[0:1, :] = p[rows - 1:rows, :]
    o_ref[...] = p + (prev - p) * mu_ref[...]


def _proj(h, pre_g, w, mu=None):
    s = h.shape[0]
    n = w.shape[1]
    x_spec = pl.BlockSpec((ROW_TILE, D_MODEL), lambda i: (i, 0))
    g_spec = pl.BlockSpec((1, D_MODEL), lambda i: (0, 0))
    w_spec = pl.BlockSpec((D_MODEL, n), lambda i: (0, 0))
    o_spec = pl.BlockSpec((ROW_TILE, n), lambda i: (i, 0))
    out_shape = jax.ShapeDtypeStruct((s, n), F32)
    if mu is None:
        return pl.pallas_call(
            _proj_body, grid=(s // ROW_TILE,), in_specs=[x_spec, g_spec, w_spec], out_specs=o_spec,
            out_shape=out_shape, compiler_params=_params("arbitrary"), name="proj",
        )(h, pre_g, w)
    return pl.pallas_call(
        _proj_shift_body, grid=(s // ROW_TILE,),
        in_specs=[x_spec, g_spec, w_spec, pl.BlockSpec((1, n), lambda i: (0, 0))], out_specs=o_spec,
        out_shape=out_shape, scratch_shapes=[pltpu.VMEM((8, n), F32)],
        compiler_params=_params("arbitrary"), name="proj_shift",
    )(h, pre_g, w, mu)


def _tri_inverse(a_strict, eye, blk16, lvl1, lvl2):
    d = a_strict * blk16
    d2 = _bmm(d, d)
    d4 = _bmm(d2, d2)
    d8 = _bmm(d4, d4)
    t = eye + d
    t = t + _bmm(t, d2)
    t = t + _bmm(t, d4)
    t = t + _bmm(t, d8)
    t = t + _bmm(_bmm(t, a_strict * lvl1), t)
    t = t + _bmm(_bmm(t, a_strict * lvl2), t)
    return t


def _rwkv_body(has_vres, *refs):
    if has_vres:
        (x_ref, vf_ref, w0_ref, wup_ref, a0_ref, aup_ref, gup_ref, kk_ref, ka_ref, rk_ref, gng_ref, gnb_ref,
         v0_ref, vup_ref, y_ref, state_ref) = refs
    else:
        (x_ref, w0_ref, wup_ref, a0_ref, aup_ref, gup_ref, kk_ref, ka_ref, rk_ref, gng_ref, gnb_ref,
         y_ref, vf_out_ref, state_ref) = refs

    @pl.when(pl.program_id(0) == 0)
    def _():
        state_ref[...] = jnp.zeros_like(state_ref)

    c = A_CHUNK
    w = A_WIDTH
    hd = A_HEAD_DIM
    x = x_ref[...]
    rows = x.shape[0]
    r = x[:, 0:w]
    k = x[:, w:2 * w]
    v = x[:, 2 * w:3 * w]
    wl = x[:, 3 * w:3 * w + LANES]
    al = x[:, 3 * w + LANES:3 * w + 2 * LANES]
    gl = x[:, 3 * w + 2 * LANES:3 * w + 3 * LANES]

    z = w0_ref[...] + _bdot(jnp.tanh(wl), wup_ref[...])
    softplus_neg = jnp.maximum(-z, 0.0) + jnp.log(1.0 + jnp.exp(-jnp.abs(z)))
    logw = -jnp.exp(-softplus_neg - 0.5)
    a = _sigmoid(a0_ref[...] + _bdot(al, aup_ref[...]))
    g = _bdot(_sigmoid(gl), gup_ref[...])
    if has_vres:
        vl = x[:, 3 * w + 3 * LANES:3 * w + 4 * LANES]
        v = v + (vf_ref[...] - v) * _sigmoid(v0_ref[...] + _bdot(vl, vup_ref[...]))
    else:
        vf_out_ref[...] = v

    li = lax.broadcasted_iota(jnp.int32, (LANES, LANES), 0)
    lj = lax.broadcasted_iota(jnp.int32, (LANES, LANES), 1)
    pair_bd = ((li < hd) == (lj < hd)).astype(F32)
    lane = lax.broadcasted_iota(jnp.int32, (1, LANES), 1)
    m0 = (lane < hd).astype(F32)
    m1 = (lane >= hd).astype(F32)

    def head_sum(t):
        return jnp.concatenate([_dot_sel(t[:, p * LANES:(p + 1) * LANES], pair_bd, 2) for p in range(w // LANES)],
                               axis=1)

    def by_head(t):
        return jnp.concatenate([t * m0, t * m1], axis=0)

    kkr = k * kk_ref[...]
    kk = kkr / jnp.maximum(jnp.sqrt(head_sum(kkr * kkr)), 1e-12)
    k = k * (1.0 + (a - 1.0) * ka_ref[...])

    ti = lax.broadcasted_iota(jnp.int32, (2 * c, 2 * c), 0)
    tj = lax.broadcasted_iota(jnp.int32, (2 * c, 2 * c), 1)
    same_head = (ti // c) == (tj // c)
    incl = (same_head & (tj <= ti)).astype(F32)
    strict = (same_head & (tj < ti)).astype(F32)
    eye = (tj == ti).astype(F32)
    blk16 = ((ti // 16) == (tj // 16)).astype(F32)
    lvl1 = (((ti // 32) == (tj // 32)) & ((ti // 16) != (tj // 16))).astype(F32)
    lvl2 = (same_head & ((ti // 32) != (tj // 32))).astype(F32)

    ri = lax.broadcasted_iota(jnp.int32, (rows, rows), 0)
    rj = lax.broadcasted_iota(jnp.int32, (rows, rows), 1)
    chunk_incl = (((ri // c) == (rj // c)) & (rj <= ri)).astype(F32)
    lb = _sel_dot(chunk_incl, logw, 3)
    e_neg = jnp.exp(-lb)
    alpha_t = -kk * jnp.exp(lb - logw)
    beta = kk * a
    beta_h = beta * e_neg
    k_h = k * e_neg
    r_t = r * jnp.exp(lb)

    n_ch = rows // c
    n_pair = w // LANES

    def slabs(t):
        return [t[ci * c:(ci + 1) * c, p * LANES:(p + 1) * LANES] for ci in range(n_ch) for p in range(n_pair)]

    def stack_by_head(t):
        return jnp.stack([by_head(s) for s in slabs(t)])

    xa2, v2 = stack_by_head(alpha_t), stack_by_head(v)
    gram = _bmm_nt(jnp.concatenate([xa2, stack_by_head(r_t)], axis=1),
                   jnp.concatenate([stack_by_head(beta_h), stack_by_head(k_h)], axis=1))
    a_ab = gram[:, 0:2 * c, 0:2 * c] * strict
    a_ak = gram[:, 0:2 * c, 2 * c:4 * c] * strict
    a_r = jnp.concatenate([gram[:, 2 * c:4 * c, 2 * c:4 * c] * incl, gram[:, 2 * c:4 * c, 0:2 * c] * incl], axis=2)
    t = _tri_inverse(a_ab, eye, blk16, lvl1, lvl2)
    wu = _bmm(t, jnp.concatenate([xa2, _bmm(a_ak, v2)], axis=2))
    wt = wu[:, 0:c, 0:LANES] + wu[:, c:2 * c, 0:LANES]
    ut = wu[:, 0:c, LANES:2 * LANES] + wu[:, c:2 * c, LANES:2 * LANES]
    xr = jnp.stack(slabs(r_t))
    vp = jnp.stack(slabs(v))

    st = state_ref[...]
    y_chunks = []
    for ci in range(n_ch):
        bs = slice(ci * n_pair, (ci + 1) * n_pair)
        lb_last = lb[(ci + 1) * c - 1:(ci + 1) * c, :]
        e_end = jnp.exp(lb_last - lb[ci * c:(ci + 1) * c])
        k_e = k[ci * c:(ci + 1) * c] * e_end
        beta_e = beta[ci * c:(ci + 1) * c] * e_end
        gamma_c = jnp.exp(lb_last)
        u = _bmm_nt(wt[bs], st) + ut[bs]
        u2 = jnp.concatenate([u * m0, u * m1], axis=1)
        y2 = _bmm(a_r[bs], jnp.concatenate([v2[bs], u2], axis=1))
        y = _bmm_nt(xr[bs], st) + y2[:, 0:c] + y2[:, c:2 * c]
        y_chunks.append(jnp.concatenate([y[p] for p in range(n_pair)], axis=1))
        vu_t = jnp.stack([jnp.concatenate([vp[ci * n_pair + p], u[p]], axis=0).T for p in range(n_pair)])
        ke_be = jnp.stack([jnp.concatenate([k_e[:, p * LANES:(p + 1) * LANES], beta_e[:, p * LANES:(p + 1) * LANES]],
                                           axis=0) for p in range(n_pair)])
        decay = jnp.stack([gamma_c[:, p * LANES:(p + 1) * LANES] for p in range(n_pair)])
        st = st * decay + pair_bd * _bmm(vu_t, ke_be)
    state_ref[...] = st
    y = jnp.concatenate(y_chunks, axis=0)

    inv_n = 1.0 / hd
    mean = head_sum(y) * inv_n
    yc = y - mean
    var = head_sum(yc * yc) * inv_n
    yn = yc * lax.rsqrt(var + A_GN_EPS) * gng_ref[...] + gnb_ref[...]
    bonus = head_sum(r * k * rk_ref[...]) * v
    y_ref[...] = (yn + bonus) * g


def _rwkv(x, p, v_first):
    s = x.shape[0]
    rows = A_CHUNK * A_CHUNKS_PER_STEP
    has_vres = v_first is not None
    row = lambda n: pl.BlockSpec((rows, n), lambda i: (i, 0))
    const = lambda a: pl.BlockSpec(a.shape, lambda i: (0,) * a.ndim)
    names = ["w0", "w_up", "a0", "a_up", "g_up", "k_k", "k_a", "r_k", "gn_g", "gn_b"]
    if has_vres:
        names += ["v0", "vres_up"]
    consts = [p[n] for n in names]
    ins = [x] + ([v_first] if has_vres else []) + consts
    in_specs = [row(A_IN)] + ([row(A_WIDTH)] if has_vres else []) + [const(a) for a in consts]
    y_shape = jax.ShapeDtypeStruct((s, A_WIDTH), F32)
    out = pl.pallas_call(
        functools.partial(_rwkv_body, has_vres),
        grid=(s // rows,), in_specs=in_specs,
        out_specs=row(A_WIDTH) if has_vres else [row(A_WIDTH), row(A_WIDTH)],
        out_shape=y_shape if has_vres else [y_shape, y_shape],
        scratch_shapes=[pltpu.VMEM((A_WIDTH // LANES, LANES, LANES), F32)],
        compiler_params=_params("arbitrary"), name="rwkv7",
    )(*ins)
    return (out, v_first) if has_vres else (out[0], out[1])


def _mla_prep_body(x_ref, cq_ref, sq_ref, ck_ref, sk_ref, qg_ref, wq_ref, wqs_ref, kg_ref, wk_ref, wv_ref,
                   q_ref, k_ref, v_ref):
    x = x_ref[...]
    cq = x[:, 0:B_Q_RANK]
    ckv = x[:, B_Q_RANK:B_Q_RANK + B_KV_RANK]
    kr = x[:, B_Q_RANK + B_KV_RANK:B_Q_RANK + B_KV_RANK + LANES]
    kr_sw = x[:, B_Q_RANK + B_KV_RANK + LANES:B_Q_RANK + B_KV_RANK + 2 * LANES]
    cqn = _rms(cq, qg_ref[...]).astype(BF16)
    ckn = _rms(ckv, kg_ref[...]).astype(BF16)
    q = jnp.dot(cqn, wq_ref[...], preferred_element_type=F32)
    q_sw = jnp.dot(cqn, wqs_ref[...], preferred_element_type=F32)
    k_nope = jnp.dot(ckn, wk_ref[...], preferred_element_type=F32)
    val = jnp.dot(ckn, wv_ref[...], preferred_element_type=F32)
    k_rope = kr * ck_ref[...] + kr_sw * sk_ref[...]
    cq_t, sq_t = cq_ref[...], sq_ref[...]
    ones_col = (lax.broadcasted_iota(jnp.int32, (1, LANES), 1) == B_V).astype(F32)
    for hd in range(B_HEADS):
        sl = slice(hd * LANES, (hd + 1) * LANES)
        q_ref[hd] = (q[:, sl] * cq_t + q_sw[:, sl] * sq_t).astype(BF16)
        k_ref[hd] = (k_nope[:, sl] + k_rope).astype(BF16)
        v_ref[hd] = (val[:, sl] + ones_col).astype(BF16)


def _mla_prep(x, tabs, p):
    s = x.shape[0]
    tm = ROW_TILE
    row = lambda n: pl.BlockSpec((tm, n), lambda i: (i, 0))
    const = lambda a: pl.BlockSpec(a.shape, lambda i: (0,) * a.ndim)
    consts = [p["q_norm_g"], p["w_q"], p["w_q_swap"], p["kv_norm_g"], p["w_k"], p["w_v"]]
    head_spec = pl.BlockSpec((B_HEADS, tm, LANES), lambda i: (0, i, 0))
    head_shape = jax.ShapeDtypeStruct((B_HEADS, s, LANES), BF16)
    return pl.pallas_call(
        _mla_prep_body, grid=(s // tm,),
        in_specs=[row(B_IN)] + [row(LANES)] * 4 + [const(a) for a in consts],
        out_specs=[head_spec] * 3, out_shape=[head_shape] * 3,
        compiler_params=_params("arbitrary"), name="mla_prep",
    )(x, *tabs, *consts)


def _flash_body(qi_ref, kj_ref, q_ref, k_ref, v_ref, o_ref, m_ref, acc_ref):
    t = pl.program_id(1)
    i = qi_ref[t]
    j = kj_ref[t]

    @pl.when(j == 0)
    def _():
        m_ref[...] = jnp.full_like(m_ref, -jnp.inf)
        acc_ref[...] = jnp.zeros_like(acc_ref)

    def step(on_diagonal):
        k = k_ref[0]
        v = v_ref[0]
        sub = q_ref.shape[1] // B_SUBTILES
        scores = []
        for a in range(B_SUBTILES):
            s = lax.dot_general(q_ref[0, a * sub:(a + 1) * sub, :], k, (((1,), (1,)), ((), ())),
                                preferred_element_type=F32)
            if on_diagonal:
                rows = lax.broadcasted_iota(jnp.int32, s.shape, 0) + a * sub
                cols = lax.broadcasted_iota(jnp.int32, s.shape, 1)
                s = jnp.where(cols <= rows, s, -jnp.inf)
            scores.append(s)
        for a, s in enumerate(scores):
            rs = slice(a * sub, (a + 1) * sub)
            m_old = m_ref[rs, :]
            m_new = jnp.maximum(m_old, jnp.max(s, axis=-1, keepdims=True))
            p = jnp.exp2(s - m_new).astype(BF16)
            acc_ref[rs, :] = jnp.exp2(m_old - m_new) * acc_ref[rs, :] + jnp.dot(p, v, preferred_element_type=F32)
            m_ref[rs, :] = m_new

    @pl.when(j < i)
    def _():
        step(False)

    @pl.when(j == i)
    def _():
        step(True)
        acc = acc_ref[...]
        o_ref[...] = acc / acc[:, B_V:B_V + 1]


def _flash(q, k, v, tile):
    n_h, s, _ = q.shape
    n_t = s // tile
    qi = np.array([i for i in range(n_t) for _ in range(i + 1)], np.int32)
    kj = np.array([j for i in range(n_t) for j in range(i + 1)], np.int32)
    grid_spec = pltpu.PrefetchScalarGridSpec(
        num_scalar_prefetch=2, grid=(n_h, len(qi)),
        in_specs=[
            pl.BlockSpec((1, tile, LANES), lambda h, t, qi, kj: (h, qi[t], 0)),
            pl.BlockSpec((1, tile, LANES), lambda h, t, qi, kj: (h, kj[t], 0)),
            pl.BlockSpec((1, tile, LANES), lambda h, t, qi, kj: (h, kj[t], 0)),
        ],
        out_specs=pl.BlockSpec((tile, LANES), lambda h, t, qi, kj: (qi[t], h)),
        scratch_shapes=[pltpu.VMEM((tile, 1), F32), pltpu.VMEM((tile, LANES), F32)],
    )
    return pl.pallas_call(
        _flash_body, grid_spec=grid_spec,
        out_shape=jax.ShapeDtypeStruct((s, n_h * LANES), F32),
        compiler_params=_params("arbitrary", "arbitrary"), name="mla_flash",
    )(jnp.asarray(qi), jnp.asarray(kj), q, k, v)


def _hgrn_body(x_ref, lb_ref, ng_ref, o_ref, state_ref):
    @pl.when(pl.program_id(0) == 0)
    def _():
        state_ref[...] = jnp.zeros_like(state_ref)

    c = C_CHUNK
    sb = C_SUB
    w = C_WIDTH
    x = x_ref[...]
    lower = lb_ref[...]
    fz = x[:, w:2 * w]
    f = lower + (1.0 - lower) * _sigmoid(fz)
    logf = jnp.log(jnp.maximum(f, C_MIN_FORGET))
    kf = (1.0 - lower) * _sigmoid(-fz)
    cq = x[:, 0:w]
    q = cq * _sigmoid(cq)
    val = x[:, 2 * w:3 * w]
    cg = x[:, 3 * w:4 * w]

    ti = lax.broadcasted_iota(jnp.int32, (c, c), 0)
    tj = lax.broadcasted_iota(jnp.int32, (c, c), 1)
    b = _sel_dot((tj <= ti).astype(F32), logf, 3)
    b_last = b[c - 1:c, :]
    q_in = q * jnp.exp(b)
    k_end = kf * jnp.exp(b_last - b)
    g_end = jnp.exp(b_last)

    s3 = lax.broadcasted_iota(jnp.int32, (sb, sb, 1), 1)
    t3 = lax.broadcasted_iota(jnp.int32, (sb, sb, 1), 0)
    causal3 = s3 <= t3

    outs = []
    for hd in range(C_HEADS):
        sl = slice(hd * C_DIM, (hd + 1) * C_DIM)
        bh, qh, kh, vh = b[:, sl], q[:, sl], kf[:, sl], val[:, sl]
        st = state_ref[hd]
        o_blocks = []
        for i in range(c // sb):
            rs = slice(i * sb, (i + 1) * sb)
            bi, qi, ki, vi = bh[rs], qh[rs], kh[rs], vh[rs]
            diff = bi[:, None, :] - bi[None, :, :]
            dec = jnp.where(causal3, jnp.exp(jnp.where(causal3, diff, 0.0)), 0.0)
            wts = jnp.sum(qi[:, None, :] * ki[None, :, :] * dec, axis=-1, keepdims=True)
            o_i = jnp.sum(wts * vi[None, :, :], axis=1)
            if i > 0:
                ref = bh[i * sb - 1:i * sb]
                q_rel = qi * jnp.exp(bi - ref)
                k_rel = kh[0:i * sb] * jnp.exp(ref - bh[0:i * sb])
                o_i = o_i + _bdot(_bdot_nt(q_rel, k_rel), vh[0:i * sb])
            o_blocks.append(o_i)
        o = jnp.concatenate(o_blocks, axis=0) + _bdot_nt(q_in[:, sl], st)
        state_ref[hd] = st * g_end[:, sl] + _bdot(vh.T, k_end[:, sl])
        o = _rms(o, ng_ref[...])
        g = cg[:, sl]
        outs.append(o * (g * _sigmoid(g)))
    o_ref[...] = jnp.concatenate(outs, axis=1)


def _hgrn(x, lower, norm_g):
    s = x.shape[0]
    c = C_CHUNK
    return pl.pallas_call(
        _hgrn_body, grid=(s // c,),
        in_specs=[pl.BlockSpec((c, C_IN), lambda i: (i, 0)), pl.BlockSpec((1, C_WIDTH), lambda i: (0, 0)),
                  pl.BlockSpec((1, C_DIM), lambda i: (0, 0))],
        out_specs=pl.BlockSpec((c, C_WIDTH), lambda i: (i, 0)),
        out_shape=jax.ShapeDtypeStruct((s, C_WIDTH), F32),
        scratch_shapes=[pltpu.VMEM((C_HEADS, C_DIM, C_DIM), F32)],
        compiler_params=_params("arbitrary"), name="hgrn2",
    )(x, lower, norm_g)


def _merge_body(h_ref, ya_ref, yb_ref, yc_ref, gl_ref, wa_ref, wb_ref, wc_ref, wo_ref, pg_ref, o_ref):
    d = D_MODEL
    gl = gl_ref[...]
    merged = (_sigmoid(gl[:, 0:d]) * _bdot(ya_ref[...], wa_ref[...])
              + _sigmoid(gl[:, d:2 * d]) * _bdot(yb_ref[...], wb_ref[...])
              + _sigmoid(gl[:, 2 * d:3 * d]) * _bdot(yc_ref[...], wc_ref[...]))
    o_ref[...] = h_ref[...] + _rms(_bdot(merged, wo_ref[...]), pg_ref[...])


def _merge(h, ya, yb, yc, gl, wa, wb, wc, wo, post_g):
    s = h.shape[0]
    tm = ROW_TILE
    row = lambda n: pl.BlockSpec((tm, n), lambda i: (i, 0))
    const = lambda a: pl.BlockSpec(a.shape, lambda i: (0,) * a.ndim)
    consts = [wa, wb, wc, wo, post_g]
    return pl.pallas_call(
        _merge_body, grid=(s // tm,),
        in_specs=[row(D_MODEL), row(A_WIDTH), row(B_HEADS * LANES), row(C_WIDTH), row(N_BRANCH * D_MODEL)]
        + [const(a) for a in consts],
        out_specs=row(D_MODEL), out_shape=jax.ShapeDtypeStruct((s, D_MODEL), F32),
        compiler_params=_params("arbitrary"), name="merge",
    )(h, ya, yb, yc, gl, *consts)


def _pad_cols(a, n):
    return jnp.pad(a, ((0, 0), (0, n - a.shape[1])))


def _pad_rows(a, n):
    return jnp.pad(a, ((0, n - a.shape[0]), (0, 0)))


def _rope_half_swap(a):
    half = B_ROPE // 2
    return jnp.concatenate([a[:, half:], a[:, :half]], axis=1)


def _layer_weights(l, w_in, rwkv_mu, vres_down, vres_mu, w_uq, w_ukv, mla_out):
    wi = w_in[l]
    d = D_MODEL
    o = 3 * A_WIDTH
    lora = [wi[:, o:o + 64], wi[:, o + 64:o + 128], wi[:, o + 128:o + 256]]
    mus = [rwkv_mu[l][o:o + 64], rwkv_mu[l][o + 64:o + 128], rwkv_mu[l][o + 128:o + 256]]
    if l > 0:
        lora.append(vres_down[l - 1])
        mus.append(vres_mu[l - 1])
    else:
        lora.append(jnp.zeros((d, 0), F32))
        mus.append(jnp.zeros((0,), F32))
    w_a = jnp.concatenate([wi[:, :o]] + [_pad_cols(t, LANES) for t in lora], axis=1)
    mu_a = jnp.concatenate([rwkv_mu[l][:o]] + [jnp.pad(t, (0, LANES - t.shape[0])) for t in mus])[None, :]

    o = A_COLS
    w_kr = wi[:, o + B_Q_RANK + B_KV_RANK:o + B_Q_RANK + B_KV_RANK + B_ROPE]
    place = lambda t: jnp.pad(t, ((0, 0), (B_NOPE, LANES - B_NOPE - B_ROPE)))
    w_b = jnp.concatenate([wi[:, o:o + B_Q_RANK + B_KV_RANK], place(w_kr), place(_rope_half_swap(w_kr))], axis=1)
    o += B_Q_RANK + B_KV_RANK + B_ROPE
    w_c = wi[:, o:o + C_IN]
    w_g = wi[:, o + C_IN:o + C_IN + N_BRANCH * d]

    scale = (B_NOPE + B_ROPE) ** -0.5 * np.log2(np.e)
    uq = (w_uq[l] * scale).reshape(B_Q_RANK, B_HEADS, B_NOPE + B_ROPE)
    uq_sw = jnp.concatenate([jnp.zeros_like(uq[..., :B_NOPE]), uq[..., B_NOPE + B_ROPE // 2:],
                             uq[..., B_NOPE:B_NOPE + B_ROPE // 2]], axis=-1)
    pad_head = lambda t: jnp.pad(t, ((0, 0), (0, 0), (0, LANES - t.shape[-1]))).reshape(t.shape[0], B_HEADS * LANES)
    ukv = w_ukv[l].reshape(B_KV_RANK, B_HEADS, B_NOPE + B_V)
    mo = jnp.pad(mla_out[l].reshape(B_HEADS, B_V, d), ((0, 0), (0, LANES - B_V), (0, 0))).reshape(B_HEADS * LANES, d)
    bf = lambda t: t.astype(BF16)
    return dict(w_a=bf(w_a), mu_a=mu_a, w_b=bf(w_b), w_c=bf(w_c), w_g=bf(w_g),
                w_q=bf(pad_head(uq)), w_q_swap=bf(pad_head(uq_sw)),
                w_k=bf(pad_head(ukv[..., :B_NOPE])), w_v=bf(pad_head(ukv[..., B_NOPE:])), mla_out=bf(mo))


def _rope_tables(positions):
    inv_freq = ROPE_THETA ** (-jnp.arange(0, B_ROPE, 2, dtype=F32) / B_ROPE)
    ang = positions.astype(F32)[:, None] * inv_freq
    cos, sin = jnp.cos(ang), jnp.sin(ang)
    s = positions.shape[0]
    pad = jnp.zeros((s, LANES - B_NOPE - B_ROPE), F32)
    cos_q = jnp.concatenate([jnp.ones((s, B_NOPE), F32), cos, cos, pad], axis=1)
    sin_t = jnp.concatenate([jnp.zeros((s, B_NOPE), F32), -sin, sin, pad], axis=1)
    cos_k = jnp.concatenate([jnp.zeros((s, B_NOPE), F32), cos, cos, pad], axis=1)
    return cos_q, sin_t, cos_k, sin_t


def kernel(x, positions, ffn1_pre_g, ffn1_post_g, ffn1_w_gate, ffn1_w_up, ffn1_w_down, mix_pre_g, mix_post_g, w_in, rwkv_mu, rwkv_w0, rwkv_w_up, rwkv_a0, rwkv_a_up, rwkv_g_up, rwkv_k_k, rwkv_k_a, rwkv_r_k, rwkv_gn_g, rwkv_gn_b, rwkv_vres_down, rwkv_vres_mu, rwkv_vres_up, rwkv_v0, rwkv_out, mla_q_norm_g, mla_w_uq, mla_kv_norm_g, mla_w_ukv, mla_out, hgrn_lower_bounds, hgrn_norm_g, hgrn_out, w_o, ffn2_pre_g, ffn2_post_g, ffn2_w_gate, ffn2_w_up, ffn2_w_down):
    bsz, seq, d = x.shape
    assert bsz == 1 and d == D_MODEL and seq % max(ROW_TILE, C_CHUNK, A_CHUNK) == 0
    depth = w_in.shape[0]
    flash_tile = min(1024, seq)
    tabs = _rope_tables(positions[0])
    lb_p = jax.nn.softmax(hgrn_lower_bounds.astype(F32), axis=0)
    lower_bounds = jnp.cumsum(lb_p, axis=0) - lb_p[0]
    row = lambda t: t[None, :]
    bf = lambda t: t.astype(BF16)

    h = x[0]
    v_first = None
    for l in range(depth):
        h = _ffn(h, row(ffn1_pre_g[l]), bf(ffn1_w_gate[l]), bf(ffn1_w_up[l]), bf(ffn1_w_down[l]),
                 row(ffn1_post_g[l]))

        lw = _layer_weights(l, w_in, rwkv_mu, rwkv_vres_down, rwkv_vres_mu, mla_w_uq, mla_w_ukv, mla_out)
        pre_g = row(mix_pre_g[l])
        x_a = _proj(h, pre_g, lw["w_a"], lw["mu_a"])
        x_b = _proj(h, pre_g, lw["w_b"])
        x_c = _proj(h, pre_g, lw["w_c"])
        gate_logits = _proj(h, pre_g, lw["w_g"])

        pa = dict(w0=row(rwkv_w0[l]), w_up=bf(_pad_rows(rwkv_w_up[l], LANES)), a0=row(rwkv_a0[l]),
                  a_up=bf(_pad_rows(rwkv_a_up[l], LANES)), g_up=bf(rwkv_g_up[l]), k_k=row(rwkv_k_k[l]),
                  k_a=row(rwkv_k_a[l]), r_k=row(rwkv_r_k[l].reshape(-1)), gn_g=row(rwkv_gn_g[l]),
                  gn_b=row(rwkv_gn_b[l]))
        if l > 0:
            pa.update(v0=row(rwkv_v0[l - 1]), vres_up=bf(_pad_rows(rwkv_vres_up[l - 1], LANES)))
        y_a, v_first = _rwkv(x_a, pa, v_first)

        pb = dict(q_norm_g=row(mla_q_norm_g[l]), kv_norm_g=row(mla_kv_norm_g[l]), w_q=lw["w_q"],
                  w_q_swap=lw["w_q_swap"], w_k=lw["w_k"], w_v=lw["w_v"])
        q_h, k_h, v_h = _mla_prep(x_b, tabs, pb)
        y_b = _flash(q_h, k_h, v_h, flash_tile)

        y_c = _hgrn(x_c, row(lower_bounds[l]), row(hgrn_norm_g[l]))

        h = _merge(h, y_a, y_b, y_c, gate_logits, bf(rwkv_out[l]), lw["mla_out"], bf(hgrn_out[l]), bf(w_o[l]),
                   row(mix_post_g[l]))

        h = _ffn(h, row(ffn2_pre_g[l]), bf(ffn2_w_gate[l]), bf(ffn2_w_up[l]), bf(ffn2_w_down[l]),
                 row(ffn2_post_g[l]))
    return h[None]
```

```python
import functools

import jax
import jax.numpy as jnp
import numpy as np
from jax import lax
from jax.experimental import pallas as pl
from jax.experimental.pallas import tpu as pltpu

F32 = jnp.float32
BF16 = jnp.bfloat16

D_MODEL = 1024
D_FF = 2816
NORM_EPS = 1e-6
MACARON_WEIGHT = 0.5

A_HEADS = 8
A_HEAD_DIM = 64
A_WIDTH = 512
A_GN_EPS = 64e-5
A_CHUNK = 64
A_CHUNKS_PER_STEP = 4
A_COLS = 1792
A_IN = 2048

B_HEADS = 8
B_NOPE = 64
B_ROPE = 32
B_V = 64
B_Q_RANK = 384
B_KV_RANK = 256
B_IN = 896
ROPE_THETA = 10000.0
B_SUBTILES = 4
LANES = 128

C_HEADS = 4
C_DIM = 128
C_WIDTH = 512
C_IN = 2048
C_CHUNK = 128
C_SUB = 8
C_MIN_FORGET = 1e-6

N_BRANCH = 3
ROW_TILE = 512
VMEM_LIMIT = 56 * 1024 * 1024

def _bdot(a, b):
    return jnp.dot(a.astype(BF16), b.astype(BF16), preferred_element_type=F32)


def _bdot_nt(a, b):
    return lax.dot_general(a.astype(BF16), b.astype(BF16), (((1,), (1,)), ((), ())),
                           preferred_element_type=F32)


def _bmm(a, b):
    return lax.dot_general(a.astype(BF16), b.astype(BF16), (((2,), (1,)), ((0,), (0,))),
                           preferred_element_type=F32)


def _bmm_nt(a, b):
    return lax.dot_general(a.astype(BF16), b.astype(BF16), (((2,), (2,)), ((0,), (0,))),
                           preferred_element_type=F32)


def _split_terms(data, parts):
    terms, rem = [], data
    for _ in range(parts):
        piece = rem.astype(BF16)
        terms.append(piece)
        rem = rem - piece.astype(F32)
    return terms


def _sel_dot(sel, data, parts):
    sel = sel.astype(BF16)
    return sum(jnp.dot(sel, t, preferred_element_type=F32) for t in _split_terms(data, parts))


def _dot_sel(data, sel, parts):
    sel = sel.astype(BF16)
    return sum(jnp.dot(t, sel, preferred_element_type=F32) for t in _split_terms(data, parts))


def _rms(x, g):
    return x * lax.rsqrt(jnp.mean(x * x, axis=-1, keepdims=True) + NORM_EPS) * g


def _sigmoid(x):
    return 1.0 / (1.0 + jnp.exp(-x))


def _params(*sem):
    return pltpu.CompilerParams(dimension_semantics=sem, vmem_limit_bytes=VMEM_LIMIT)


def _ffn_body(x_ref, pre_g_ref, wg_ref, wu_ref, wd_ref, post_g_ref, o_ref, xn_ref, acc_ref):
    j = pl.program_id(1)

    @pl.when(j == 0)
    def _():
        xn_ref[...] = _rms(x_ref[...], pre_g_ref[...]).astype(BF16)
        acc_ref[...] = jnp.zeros_like(acc_ref)

    xn = xn_ref[...]
    gate = jnp.dot(xn, wg_ref[...], preferred_element_type=F32)
    up = jnp.dot(xn, wu_ref[...], preferred_element_type=F32)
    mid = (gate * _sigmoid(gate) * up).astype(BF16)
    acc_ref[...] += jnp.dot(mid, wd_ref[...], preferred_element_type=F32)

    @pl.when(j == pl.num_programs(1) - 1)
    def _():
        o_ref[...] = x_ref[...] + MACARON_WEIGHT * _rms(acc_ref[...], post_g_ref[...])


def _ffn(h, pre_g, wg, wu, wd, post_g):
    s = h.shape[0]
    tf = D_FF // 2
    return pl.pallas_call(
        _ffn_body,
        grid=(s // ROW_TILE, D_FF // tf),
        in_specs=[
            pl.BlockSpec((ROW_TILE, D_MODEL), lambda i, j: (i, 0)),
            pl.BlockSpec((1, D_MODEL), lambda i, j: (0, 0)),
            pl.BlockSpec((D_MODEL, tf), lambda i, j: (0, j)),
            pl.BlockSpec((D_MODEL, tf), lambda i, j: (0, j)),
            pl.BlockSpec((tf, D_MODEL), lambda i, j: (j, 0)),
            pl.BlockSpec((1, D_MODEL), lambda i, j: (0, 0)),
        ],
        out_specs=pl.BlockSpec((ROW_TILE, D_MODEL), lambda i, j: (i, 0)),
        out_shape=jax.ShapeDtypeStruct((s, D_MODEL), F32),
        scratch_shapes=[pltpu.VMEM((ROW_TILE, D_MODEL), BF16), pltpu.VMEM((ROW_TILE, D_MODEL), F32)],
        compiler_params=_params("arbitrary", "arbitrary"),
        name="ffn",
    )(h, pre_g, wg, wu, wd, post_g)


def _proj_body(x_ref, g_ref, w_ref, o_ref):
    xn = _rms(x_ref[...], g_ref[...]).astype(BF16)
    o_ref[...] = jnp.dot(xn, w_ref[...], preferred_element_type=F32)


def _proj_shift_body(x_ref, g_ref, w_ref, mu_ref, o_ref, carry_ref):
    @pl.when(pl.program_id(0) == 0)
    def _():
        carry_ref[...] = jnp.zeros_like(carry_ref)

    xn = _rms(x_ref[...], g_ref[...]).astype(BF16)
    p = jnp.dot(xn, w_ref[...], preferred_element_type=F32)
    rows = p.shape[0]
    prev = pltpu.roll(p, 1, axis=0)
    first = lax.broadcasted_iota(jnp.int32, p.shape, 0) == 0
    prev = jnp.where(first, carry_ref[0:1, :], prev)
    carry_ref[0:1, :] = p[rows - 1:rows, :]
    o_ref[...] = p + (prev - p) * mu_ref[...]


def _proj(h, pre_g, w, mu=None):
    s = h.shape[0]
    n = w.shape[1]
    x_spec = pl.BlockSpec((ROW_TILE, D_MODEL), lambda i: (i, 0))
    g_spec = pl.BlockSpec((1, D_MODEL), lambda i: (0, 0))
    w_spec = pl.BlockSpec((D_MODEL, n), lambda i: (0, 0))
    o_spec = pl.BlockSpec((ROW_TILE, n), lambda i: (i, 0))
    out_shape = jax.ShapeDtypeStruct((s, n), F32)
    if mu is None:
        return pl.pallas_call(
            _proj_body, grid=(s // ROW_TILE,), in_specs=[x_spec, g_spec, w_spec], out_specs=o_spec,
            out_shape=out_shape, compiler_params=_params("arbitrary"), name="proj",
        )(h, pre_g, w)
    return pl.pallas_call(
        _proj_shift_body, grid=(s // ROW_TILE,),
        in_specs=[x_spec, g_spec, w_spec, pl.BlockSpec((1, n), lambda i: (0, 0))], out_specs=o_spec,
        out_shape=out_shape, scratch_shapes=[pltpu.VMEM((8, n), F32)],
        compiler_params=_params("arbitrary"), name="proj_shift",
    )(h, pre_g, w, mu)


def _tri_inverse(a_strict, eye, blk16, lvl1, lvl2):
    d = a_strict * blk16
    d2 = _bmm(d, d)
    d4 = _bmm(d2, d2)
    d8 = _bmm(d4, d4)
    t = eye + d
    t = t + _bmm(t, d2)
    t = t + _bmm(t, d4)
    t = t + _bmm(t, d8)
    t = t + _bmm(_bmm(t, a_strict * lvl1), t)
    t = t + _bmm(_bmm(t, a_strict * lvl2), t)
    return t


def _rwkv_body(has_vres, *refs):
    if has_vres:
        (x_ref, vf_ref, w0_ref, wup_ref, a0_ref, aup_ref, gup_ref, kk_ref, ka_ref, rk_ref, gng_ref, gnb_ref,
         v0_ref, vup_ref, y_ref, state_ref) = refs
    else:
        (x_ref, w0_ref, wup_ref, a0_ref, aup_ref, gup_ref, kk_ref, ka_ref, rk_ref, gng_ref, gnb_ref,
         y_ref, vf_out_ref, state_ref) = refs

    @pl.when(pl.program_id(0) == 0)
    def _():
        state_ref[...] = jnp.zeros_like(state_ref)

    c = A_CHUNK
    w = A_WIDTH
    hd = A_HEAD_DIM
    x = x_ref[...]
    rows = x.shape[0]
    r = x[:, 0:w]
    k = x[:, w:2 * w]
    v = x[:, 2 * w:3 * w]
    wl = x[:, 3 * w:3 * w + LANES]
    al = x[:, 3 * w + LANES:3 * w + 2 * LANES]
    gl = x[:, 3 * w + 2 * LANES:3 * w + 3 * LANES]

    z = w0_ref[...] + _bdot(jnp.tanh(wl), wup_ref[...])
    softplus_neg = jnp.maximum(-z, 0.0) + jnp.log(1.0 + jnp.exp(-jnp.abs(z)))
    logw = -jnp.exp(-softplus_neg - 0.5)
    a = _sigmoid(a0_ref[...] + _bdot(al, aup_ref[...]))
    g = _bdot(_sigmoid(gl), gup_ref[...])
    if has_vres:
        vl = x[:, 3 * w + 3 * LANES:3 * w + 4 * LANES]
        v = v + (vf_ref[...] - v) * _sigmoid(v0_ref[...] + _bdot(vl, vup_ref[...]))
    else:
        vf_out_ref[...] = v

    li = lax.broadcasted_iota(jnp.int32, (LANES, LANES), 0)
    lj = lax.broadcasted_iota(jnp.int32, (LANES, LANES), 1)
    pair_bd = ((li < hd) == (lj < hd)).astype(F32)
    lane = lax.broadcasted_iota(jnp.int32, (1, LANES), 1)
    m0 = (lane < hd).astype(F32)
    m1 = (lane >= hd).astype(F32)

    def head_sum(t):
        return jnp.concatenate([_dot_sel(t[:, p * LANES:(p + 1) * LANES], pair_bd, 2) for p in range(w // LANES)],
                               axis=1)

    def by_head(t):
        return jnp.concatenate([t * m0, t * m1], axis=0)

    kkr = k * kk_ref[...]
    kk = kkr / jnp.maximum(jnp.sqrt(head_sum(kkr * kkr)), 1e-12)
    k = k * (1.0 + (a - 1.0) * ka_ref[...])

    ti = lax.broadcasted_iota(jnp.int32, (2 * c, 2 * c), 0)
    tj = lax.broadcasted_iota(jnp.int32, (2 * c, 2 * c), 1)
    same_head = (ti // c) == (tj // c)
    incl = (same_head & (tj <= ti)).astype(F32)
    strict = (same_head & (tj < ti)).astype(F32)
    eye = (tj == ti).astype(F32)
    blk16 = ((ti // 16) == (tj // 16)).astype(F32)
    lvl1 = (((ti // 32) == (tj // 32)) & ((ti // 16) != (tj // 16))).astype(F32)
    lvl2 = (same_head & ((ti // 32) != (tj // 32))).astype(F32)

    ri = lax.broadcasted_iota(jnp.int32, (rows, rows), 0)
    rj = lax.broadcasted_iota(jnp.int32, (rows, rows), 1)
    chunk_incl = (((ri // c) == (rj // c)) & (rj <= ri)).astype(F32)
    lb = _sel_dot(chunk_incl, logw, 3)
    e_neg = jnp.exp(-lb)
    alpha_t = -kk * jnp.exp(lb - logw)
    beta = kk * a
    beta_h = beta * e_neg
    k_h = k * e_neg
    r_t = r * jnp.exp(lb)

    n_ch = rows // c
    n_pair = w // LANES

    def slabs(t):
        return [t[ci * c:(ci + 1) * c, p * LANES:(p + 1) * LANES] for ci in range(n_ch) for p in range(n_pair)]

    def stack_by_head(t):
        return jnp.stack([by_head(s) for s in slabs(t)])

    xa2, v2 = stack_by_head(alpha_t), stack_by_head(v)
    gram = _bmm_nt(jnp.concatenate([xa2, stack_by_head(r_t)], axis=1),
                   jnp.concatenate([stack_by_head(beta_h), stack_by_head(k_h)], axis=1))
    a_ab = gram[:, 0:2 * c, 0:2 * c] * strict
    a_ak = gram[:, 0:2 * c, 2 * c:4 * c] * strict
    a_r = jnp.concatenate([gram[:, 2 * c:4 * c, 2 * c:4 * c] * incl, gram[:, 2 * c:4 * c, 0:2 * c] * incl], axis=2)
    t = _tri_inverse(a_ab, eye, blk16, lvl1, lvl2)
    wu = _bmm(t, jnp.concatenate([xa2, _bmm(a_ak, v2)], axis=2))
    wt = wu[:, 0:c, 0:LANES] + wu[:, c:2 * c, 0:LANES]
    ut = wu[:, 0:c, LANES:2 * LANES] + wu[:, c:2 * c, LANES:2 * LANES]
    xr = jnp.stack(slabs(r_t))
    vp = jnp.stack(slabs(v))

    st = state_ref[...]
    y_chunks = []
    for ci in range(n_ch):
        bs = slice(ci * n_pair, (ci + 1) * n_pair)
        lb_last = lb[(ci + 1) * c - 1:(ci + 1) * c, :]
        e_end = jnp.exp(lb_last - lb[ci * c:(ci + 1) * c])
        k_e = k[ci * c:(ci + 1) * c] * e_end
        beta_e = beta[ci * c:(ci + 1) * c] * e_end
        gamma_c = jnp.exp(lb_last)
        u = _bmm_nt(wt[bs], st) + ut[bs]
        u2 = jnp.concatenate([u * m0, u * m1], axis=1)
        y2 = _bmm(a_r[bs], jnp.concatenate([v2[bs], u2], axis=1))
        y = _bmm_nt(xr[bs], st) + y2[:, 0:c] + y2[:, c:2 * c]
        y_chunks.append(jnp.concatenate([y[p] for p in range(n_pair)], axis=1))
        vu_t = jnp.stack([jnp.concatenate([vp[ci * n_pair + p], u[p]], axis=0).T for p in range(n_pair)])
        ke_be = jnp.stack([jnp.concatenate([k_e[:, p * LANES:(p + 1) * LANES], beta_e[:, p * LANES:(p + 1) * LANES]],
                                           axis=0) for p in range(n_pair)])
        decay = jnp.stack([gamma_c[:, p * LANES:(p + 1) * LANES] for p in range(n_pair)])
        st = st * decay + pair_bd * _bmm(vu_t, ke_be)
    state_ref[...] = st
    y = jnp.concatenate(y_chunks, axis=0)

    inv_n = 1.0 / hd
    mean = head_sum(y) * inv_n
    yc = y - mean
    var = head_sum(yc * yc) * inv_n
    yn = yc * lax.rsqrt(var + A_GN_EPS) * gng_ref[...] + gnb_ref[...]
    bonus = head_sum(r * k * rk_ref[...]) * v
    y_ref[...] = (yn + bonus) * g


def _rwkv(x, p, v_first):
    s = x.shape[0]
    rows = A_CHUNK * A_CHUNKS_PER_STEP
    has_vres = v_first is not None
    row = lambda n: pl.BlockSpec((rows, n), lambda i: (i, 0))
    const = lambda a: pl.BlockSpec(a.shape, lambda i: (0,) * a.ndim)
    names = ["w0", "w_up", "a0", "a_up", "g_up", "k_k", "k_a", "r_k", "gn_g", "gn_b"]
    if has_vres:
        names += ["v0", "vres_up"]
    consts = [p[n] for n in names]
    ins = [x] + ([v_first] if has_vres else []) + consts
    in_specs = [row(A_IN)] + ([row(A_WIDTH)] if has_vres else []) + [const(a) for a in consts]
    y_shape = jax.ShapeDtypeStruct((s, A_WIDTH), F32)
    out = pl.pallas_call(
        functools.partial(_rwkv_body, has_vres),
        grid=(s // rows,), in_specs=in_specs,
        out_specs=row(A_WIDTH) if has_vres else [row(A_WIDTH), row(A_WIDTH)],
        out_shape=y_shape if has_vres else [y_shape, y_shape],
        scratch_shapes=[pltpu.VMEM((A_WIDTH // LANES, LANES, LANES), F32)],
        compiler_params=_params("arbitrary"), name="rwkv7",
    )(*ins)
    return (out, v_first) if has_vres else (out[0], out[1])


def _mla_prep_body(x_ref, cq_ref, sq_ref, ck_ref, sk_ref, qg_ref, wq_ref, wqs_ref, kg_ref, wk_ref, wv_ref,
                   q_ref, k_ref, v_ref):
    x = x_ref[...]
    cq = x[:, 0:B_Q_RANK]
    ckv = x[:, B_Q_RANK:B_Q_RANK + B_KV_RANK]
    kr = x[:, B_Q_RANK + B_KV_RANK:B_Q_RANK + B_KV_RANK + LANES]
    kr_sw = x[:, B_Q_RANK + B_KV_RANK + LANES:B_Q_RANK + B_KV_RANK + 2 * LANES]
    cqn = _rms(cq, qg_ref[...]).astype(BF16)
    ckn = _rms(ckv, kg_ref[...]).astype(BF16)
    q = jnp.dot(cqn, wq_ref[...], preferred_element_type=F32)
    q_sw = jnp.dot(cqn, wqs_ref[...], preferred_element_type=F32)
    k_nope = jnp.dot(ckn, wk_ref[...], preferred_element_type=F32)
    val = jnp.dot(ckn, wv_ref[...], preferred_element_type=F32)
    k_rope = kr * ck_ref[...] + kr_sw * sk_ref[...]
    cq_t, sq_t = cq_ref[...], sq_ref[...]
    ones_col = (lax.broadcasted_iota(jnp.int32, (1, LANES), 1) == B_V).astype(F32)
    for hd in range(B_HEADS):
        sl = slice(hd * LANES, (hd + 1) * LANES)
        q_ref[hd] = (q[:, sl] * cq_t + q_sw[:, sl] * sq_t).astype(BF16)
        k_ref[hd] = (k_nope[:, sl] + k_rope).astype(BF16)
        v_ref[hd] = (val[:, sl] + ones_col).astype(BF16)


def _mla_prep(x, tabs, p):
    s = x.shape[0]
    tm = ROW_TILE
    row = lambda n: pl.BlockSpec((tm, n), lambda i: (i, 0))
    const = lambda a: pl.BlockSpec(a.shape, lambda i: (0,) * a.ndim)
    consts = [p["q_norm_g"], p["w_q"], p["w_q_swap"], p["kv_norm_g"], p["w_k"], p["w_v"]]
    head_spec = pl.BlockSpec((B_HEADS, tm, LANES), lambda i: (0, i, 0))
    head_shape = jax.ShapeDtypeStruct((B_HEADS, s, LANES), BF16)
    return pl.pallas_call(
        _mla_prep_body, grid=(s // tm,),
        in_specs=[row(B_IN)] + [row(LANES)] * 4 + [const(a) for a in consts],
        out_specs=[head_spec] * 3, out_shape=[head_shape] * 3,
        compiler_params=_params("arbitrary"), name="mla_prep",
    )(x, *tabs, *consts)


def _flash_body(q_ref, k_ref, v_ref, o_ref, m_ref, acc_ref):
    i = pl.program_id(1)
    tile = q_ref.shape[1]
    sub = tile // B_SUBTILES
    m_ref[...] = jnp.full_like(m_ref, -jnp.inf)
    acc_ref[...] = jnp.zeros_like(acc_ref)

    def step(j, on_diagonal):
        start = pl.multiple_of(j * tile, tile)
        scores = []
        for a in range(B_SUBTILES):
            n_keys = (a + 1) * sub if on_diagonal else tile
            s = lax.dot_general(q_ref[0, a * sub:(a + 1) * sub, :], k_ref[0, pl.ds(start, n_keys), :],
                                (((1,), (1,)), ((), ())), preferred_element_type=F32)
            if on_diagonal:
                rows = lax.broadcasted_iota(jnp.int32, s.shape, 0) + a * sub
                cols = lax.broadcasted_iota(jnp.int32, s.shape, 1)
                s = jnp.where(cols <= rows, s, -jnp.inf)
            scores.append(s)
        for a, s in enumerate(scores):
            rs = slice(a * sub, (a + 1) * sub)
            m_old = m_ref[rs, :]
            m_new = jnp.maximum(m_old, jnp.max(s, axis=-1, keepdims=True))
            p = jnp.exp2(s - m_new).astype(BF16)
            pv = jnp.dot(p, v_ref[0, pl.ds(start, s.shape[1]), :], preferred_element_type=F32)
            acc_ref[rs, :] = jnp.exp2(m_old - m_new) * acc_ref[rs, :] + pv
            m_ref[rs, :] = m_new

    def off_diagonal(j, carry):
        step(j, False)
        return carry

    lax.fori_loop(0, i, off_diagonal, 0)
    step(i, True)
    acc = acc_ref[...]
    o_ref[...] = acc / acc[:, B_V:B_V + 1]


def _flash(q, k, v, tile):
    n_h, s, _ = q.shape
    head = pl.BlockSpec((1, s, LANES), lambda h, i: (h, 0, 0))
    return pl.pallas_call(
        _flash_body, grid=(n_h, s // tile),
        in_specs=[pl.BlockSpec((1, tile, LANES), lambda h, i: (h, i, 0)), head, head],
        out_specs=pl.BlockSpec((tile, LANES), lambda h, i: (i, h)),
        out_shape=jax.ShapeDtypeStruct((s, n_h * LANES), F32),
        scratch_shapes=[pltpu.VMEM((tile, 1), F32), pltpu.VMEM((tile, LANES), F32)],
        compiler_params=_params("arbitrary", "arbitrary"), name="mla_flash",
    )(q, k, v)


def _hgrn_body(x_ref, lb_ref, ng_ref, o_ref, state_ref):
    @pl.when(pl.program_id(0) == 0)
    def _():
        state_ref[...] = jnp.zeros_like(state_ref)

    c = C_CHUNK
    w = C_WIDTH
    x = x_ref[...]
    lower = lb_ref[...]
    fz = x[:, w:2 * w]
    f = lower + (1.0 - lower) * _sigmoid(fz)
    logf = jnp.log(jnp.maximum(f, C_MIN_FORGET))
    kf = (1.0 - lower) * _sigmoid(-fz)
    cq = x[:, 0:w]
    q = cq * _sigmoid(cq)
    val = x[:, 2 * w:3 * w]
    cg = x[:, 3 * w:4 * w]

    ti = lax.broadcasted_iota(jnp.int32, (c, c), 0)
    tj = lax.broadcasted_iota(jnp.int32, (c, c), 1)
    row = lax.broadcasted_iota(jnp.int32, (c, 1), 0)
    b = _sel_dot((tj <= ti).astype(F32), logf, 3)
    b_last = b[c - 1:c, :]

    n0 = C_SUB
    ref = _sel_dot((tj == (ti // n0) * n0 + n0 // 2 - 1).astype(F32), b, 3)
    q_s = q * jnp.exp(b - ref)
    k_s = kf * jnp.exp(ref - b)
    mask = ((ti // n0) == (tj // n0)) & (tj <= ti)
    attn = [jnp.where(mask, _bdot_nt(q_s[:, hd * C_DIM:(hd + 1) * C_DIM], k_s[:, hd * C_DIM:(hd + 1) * C_DIM]), 0.0)
            for hd in range(C_HEADS)]
    n = n0
    while n < c:
        ref = _sel_dot((tj == (ti // (2 * n)) * (2 * n) + n - 1).astype(F32), b, 3)
        right = ((row // n) % 2) == 1
        q_s = q * jnp.exp(jnp.where(right, b - ref, 0.0))
        k_s = kf * jnp.exp(jnp.where(right, 0.0, ref - b))
        mask = ((ti // (2 * n)) == (tj // (2 * n))) & (((ti // n) % 2) == 1) & (((tj // n) % 2) == 0)
        attn = [jnp.where(mask, _bdot_nt(q_s[:, hd * C_DIM:(hd + 1) * C_DIM], k_s[:, hd * C_DIM:(hd + 1) * C_DIM]),
                          attn[hd]) for hd in range(C_HEADS)]
        n *= 2

    q_in = q * jnp.exp(b)
    k_end = kf * jnp.exp(b_last - b)
    g_end = jnp.exp(b_last)
    outs = []
    for hd in range(C_HEADS):
        sl = slice(hd * C_DIM, (hd + 1) * C_DIM)
        st = state_ref[hd]
        vh = val[:, sl]
        o = _bdot(attn[hd], vh) + _bdot_nt(q_in[:, sl], st)
        state_ref[hd] = st * g_end[:, sl] + _bdot(vh.T, k_end[:, sl])
        g = cg[:, sl]
        outs.append(_rms(o, ng_ref[...]) * (g * _sigmoid(g)))
    o_ref[...] = jnp.concatenate(outs, axis=1)


def _hgrn(x, lower, norm_g):
    s = x.shape[0]
    c = C_CHUNK
    return pl.pallas_call(
        _hgrn_body, grid=(s // c,),
        in_specs=[pl.BlockSpec((c, C_IN), lambda i: (i, 0)), pl.BlockSpec((1, C_WIDTH), lambda i: (0, 0)),
                  pl.BlockSpec((1, C_DIM), lambda i: (0, 0))],
        out_specs=pl.BlockSpec((c, C_WIDTH), lambda i: (i, 0)),
        out_shape=jax.ShapeDtypeStruct((s, C_WIDTH), F32),
        scratch_shapes=[pltpu.VMEM((C_HEADS, C_DIM, C_DIM), F32)],
        compiler_params=_params("arbitrary"), name="hgrn2",
    )(x, lower, norm_g)


def _merge_body(h_ref, ya_ref, yb_ref, yc_ref, gl_ref, wa_ref, wb_ref, wc_ref, wo_ref, pg_ref, o_ref):
    d = D_MODEL
    gl = gl_ref[...]
    merged = (_sigmoid(gl[:, 0:d]) * _bdot(ya_ref[...], wa_ref[...])
              + _sigmoid(gl[:, d:2 * d]) * _bdot(yb_ref[...], wb_ref[...])
              + _sigmoid(gl[:, 2 * d:3 * d]) * _bdot(yc_ref[...], wc_ref[...]))
    o_ref[...] = h_ref[...] + _rms(_bdot(merged, wo_ref[...]), pg_ref[...])


def _merge(h, ya, yb, yc, gl, wa, wb, wc, wo, post_g):
    s = h.shape[0]
    tm = ROW_TILE
    row = lambda n: pl.BlockSpec((tm, n), lambda i: (i, 0))
    const = lambda a: pl.BlockSpec(a.shape, lambda i: (0,) * a.ndim)
    consts = [wa, wb, wc, wo, post_g]
    return pl.pallas_call(
        _merge_body, grid=(s // tm,),
        in_specs=[row(D_MODEL), row(A_WIDTH), row(B_HEADS * LANES), row(C_WIDTH), row(N_BRANCH * D_MODEL)]
        + [const(a) for a in consts],
        out_specs=row(D_MODEL), out_shape=jax.ShapeDtypeStruct((s, D_MODEL), F32),
        compiler_params=_params("arbitrary"), name="merge",
    )(h, ya, yb, yc, gl, *consts)


def _pad_cols(a, n):
    return jnp.pad(a, ((0, 0), (0, n - a.shape[1])))


def _pad_rows(a, n):
    return jnp.pad(a, ((0, n - a.shape[0]), (0, 0)))


def _rope_half_swap(a):
    half = B_ROPE // 2
    return jnp.concatenate([a[:, half:], a[:, :half]], axis=1)


def _layer_weights(l, w_in, rwkv_mu, vres_down, vres_mu, w_uq, w_ukv, mla_out):
    wi = w_in[l]
    d = D_MODEL
    o = 3 * A_WIDTH
    lora = [wi[:, o:o + 64], wi[:, o + 64:o + 128], wi[:, o + 128:o + 256]]
    mus = [rwkv_mu[l][o:o + 64], rwkv_mu[l][o + 64:o + 128], rwkv_mu[l][o + 128:o + 256]]
    if l > 0:
        lora.append(vres_down[l - 1])
        mus.append(vres_mu[l - 1])
    else:
        lora.append(jnp.zeros((d, 0), F32))
        mus.append(jnp.zeros((0,), F32))
    w_a = jnp.concatenate([wi[:, :o]] + [_pad_cols(t, LANES) for t in lora], axis=1)
    mu_a = jnp.concatenate([rwkv_mu[l][:o]] + [jnp.pad(t, (0, LANES - t.shape[0])) for t in mus])[None, :]

    o = A_COLS
    w_kr = wi[:, o + B_Q_RANK + B_KV_RANK:o + B_Q_RANK + B_KV_RANK + B_ROPE]
    place = lambda t: jnp.pad(t, ((0, 0), (B_NOPE, LANES - B_NOPE - B_ROPE)))
    w_b = jnp.concatenate([wi[:, o:o + B_Q_RANK + B_KV_RANK], place(w_kr), place(_rope_half_swap(w_kr))], axis=1)
    o += B_Q_RANK + B_KV_RANK + B_ROPE
    w_c = wi[:, o:o + C_IN]
    w_g = wi[:, o + C_IN:o + C_IN + N_BRANCH * d]

    scale = (B_NOPE + B_ROPE) ** -0.5 * np.log2(np.e)
    uq = (w_uq[l] * scale).reshape(B_Q_RANK, B_HEADS, B_NOPE + B_ROPE)
    uq_sw = jnp.concatenate([jnp.zeros_like(uq[..., :B_NOPE]), uq[..., B_NOPE + B_ROPE // 2:],
                             uq[..., B_NOPE:B_NOPE + B_ROPE // 2]], axis=-1)
    pad_head = lambda t: jnp.pad(t, ((0, 0), (0, 0), (0, LANES - t.shape[-1]))).reshape(t.shape[0], B_HEADS * LANES)
    ukv = w_ukv[l].reshape(B_KV_RANK, B_HEADS, B_NOPE + B_V)
    mo = jnp.pad(mla_out[l].reshape(B_HEADS, B_V, d), ((0, 0), (0, LANES - B_V), (0, 0))).reshape(B_HEADS * LANES, d)
    bf = lambda t: t.astype(BF16)
    return dict(w_a=bf(w_a), mu_a=mu_a, w_b=bf(w_b), w_c=bf(w_c), w_g=bf(w_g),
                w_q=bf(pad_head(uq)), w_q_swap=bf(pad_head(uq_sw)),
                w_k=bf(pad_head(ukv[..., :B_NOPE])), w_v=bf(pad_head(ukv[..., B_NOPE:])), mla_out=bf(mo))


def _rope_tables(positions):
    inv_freq = ROPE_THETA ** (-jnp.arange(0, B_ROPE, 2, dtype=F32) / B_ROPE)
    ang = positions.astype(F32)[:, None] * inv_freq
    cos, sin = jnp.cos(ang), jnp.sin(ang)
    s = positions.shape[0]
    pad = jnp.zeros((s, LANES - B_NOPE - B_ROPE), F32)
    cos_q = jnp.concatenate([jnp.ones((s, B_NOPE), F32), cos, cos, pad], axis=1)
    sin_t = jnp.concatenate([jnp.zeros((s, B_NOPE), F32), -sin, sin, pad], axis=1)
    cos_k = jnp.concatenate([jnp.zeros((s, B_NOPE), F32), cos, cos, pad], axis=1)
    return cos_q, sin_t, cos_k, sin_t


def kernel(x, positions, ffn1_pre_g, ffn1_post_g, ffn1_w_gate, ffn1_w_up, ffn1_w_down, mix_pre_g, mix_post_g, w_in, rwkv_mu, rwkv_w0, rwkv_w_up, rwkv_a0, rwkv_a_up, rwkv_g_up, rwkv_k_k, rwkv_k_a, rwkv_r_k, rwkv_gn_g, rwkv_gn_b, rwkv_vres_down, rwkv_vres_mu, rwkv_vres_up, rwkv_v0, rwkv_out, mla_q_norm_g, mla_w_uq, mla_kv_norm_g, mla_w_ukv, mla_out, hgrn_lower_bounds, hgrn_norm_g, hgrn_out, w_o, ffn2_pre_g, ffn2_post_g, ffn2_w_gate, ffn2_w_up, ffn2_w_down):
    bsz, seq, d = x.shape
    assert bsz == 1 and d == D_MODEL and seq % max(ROW_TILE, C_CHUNK, A_CHUNK) == 0
    depth = w_in.shape[0]
    flash_tile = min(1024, seq)
    tabs = _rope_tables(positions[0])
    lb_p = jax.nn.softmax(hgrn_lower_bounds.astype(F32), axis=0)
    lower_bounds = jnp.cumsum(lb_p, axis=0) - lb_p[0]
    row = lambda t: t[None, :]
    bf = lambda t: t.astype(BF16)

    h = x[0]
    v_first = None
    for l in range(depth):
        h = _ffn(h, row(ffn1_pre_g[l]), bf(ffn1_w_gate[l]), bf(ffn1_w_up[l]), bf(ffn1_w_down[l]),
                 row(ffn1_post_g[l]))

        lw = _layer_weights(l, w_in, rwkv_mu, rwkv_vres_down, rwkv_vres_mu, mla_w_uq, mla_w_ukv, mla_out)
        pre_g = row(mix_pre_g[l])
        x_a = _proj(h, pre_g, lw["w_a"], lw["mu_a"])
        x_b = _proj(h, pre_g, lw["w_b"])
        x_c = _proj(h, pre_g, lw["w_c"])
        gate_logits = _proj(h, pre_g, lw["w_g"])

        pa = dict(w0=row(rwkv_w0[l]), w_up=bf(_pad_rows(rwkv_w_up[l], LANES)), a0=row(rwkv_a0[l]),
                  a_up=bf(_pad_rows(rwkv_a_up[l], LANES)), g_up=bf(rwkv_g_up[l]), k_k=row(rwkv_k_k[l]),
                  k_a=row(rwkv_k_a[l]), r_k=row(rwkv_r_k[l].reshape(-1)), gn_g=row(rwkv_gn_g[l]),
                  gn_b=row(rwkv_gn_b[l]))
        if l > 0:
            pa.update(v0=row(rwkv_v0[l - 1]), vres_up=bf(_pad_rows(rwkv_vres_up[l - 1], LANES)))
        y_a, v_first = _rwkv(x_a, pa, v_first)

        pb = dict(q_norm_g=row(mla_q_norm_g[l]), kv_norm_g=row(mla_kv_norm_g[l]), w_q=lw["w_q"],
                  w_q_swap=lw["w_q_swap"], w_k=lw["w_k"], w_v=lw["w_v"])
        q_h, k_h, v_h = _mla_prep(x_b, tabs, pb)
        y_b = _flash(q_h, k_h, v_h, flash_tile)

        y_c = _hgrn(x_c, row(lower_bounds[l]), row(hgrn_norm_g[l]))

        h = _merge(h, y_a, y_b, y_c, gate_logits, bf(rwkv_out[l]), lw["mla_out"], bf(hgrn_out[l]), bf(w_o[l]),
                   row(mix_post_g[l]))

        h = _ffn(h, row(ffn2_pre_g[l]), bf(ffn2_w_gate[l]), bf(ffn2_w_up[l]), bf(ffn2_w_down[l]),
                 row(ffn2_post_g[l]))
    return h[None]
```

```python
import functools

import jax
import jax.numpy as jnp
import numpy as np
from jax import lax
from jax.experimental import pallas as pl
from jax.experimental.pallas import tpu as pltpu

F32 = jnp.float32
BF16 = jnp.bfloat16

D_MODEL = 1024
D_FF = 2816
NORM_EPS = 1e-6
MACARON_WEIGHT = 0.5

A_HEADS = 8
A_HEAD_DIM = 64
A_WIDTH = 512
A_GN_EPS = 64e-5
A_CHUNK = 64
A_CHUNKS_PER_STEP = 4
A_COLS = 1792
A_IN = 2048

B_HEADS = 8
B_NOPE = 64
B_ROPE = 32
B_V = 64
B_Q_RANK = 384
B_KV_RANK = 256
B_IN = 896
ROPE_THETA = 10000.0
B_SUBTILES = 4
LANES = 128

C_HEADS = 4
C_DIM = 128
C_WIDTH = 512
C_IN = 2048
C_CHUNK = 128
C_SUB = 8
C_MIN_FORGET = 1e-6

N_BRANCH = 3
ROW_TILE = 512
VMEM_LIMIT = 56 * 1024 * 1024

def _bdot(a, b):
    return jnp.dot(a.astype(BF16), b.astype(BF16), preferred_element_type=F32)


def _bdot_nt(a, b):
    return lax.dot_general(a.astype(BF16), b.astype(BF16), (((1,), (1,)), ((), ())),
                           preferred_element_type=F32)


def _bmm(a, b):
    return lax.dot_general(a.astype(BF16), b.astype(BF16), (((2,), (1,)), ((0,), (0,))),
                           preferred_element_type=F32)


def _bmm_nt(a, b):
    return lax.dot_general(a.astype(BF16), b.astype(BF16), (((2,), (2,)), ((0,), (0,))),
                           preferred_element_type=F32)


def _split_terms(data, parts):
    terms, rem = [], data
    for _ in range(parts):
        piece = rem.astype(BF16)
        terms.append(piece)
        rem = rem - piece.astype(F32)
    return terms


def _sel_dot(sel, data, parts):
    sel = sel.astype(BF16)
    return sum(jnp.dot(sel, t, preferred_element_type=F32) for t in _split_terms(data, parts))


def _dot_sel(data, sel, parts):
    sel = sel.astype(BF16)
    return sum(jnp.dot(t, sel, preferred_element_type=F32) for t in _split_terms(data, parts))


def _rms(x, g):
    return x * lax.rsqrt(jnp.mean(x * x, axis=-1, keepdims=True) + NORM_EPS) * g


def _sigmoid(x):
    return 1.0 / (1.0 + jnp.exp(-x))


def _params(*sem):
    return pltpu.CompilerParams(dimension_semantics=sem, vmem_limit_bytes=VMEM_LIMIT)


def _ffn_body(x_ref, pre_g_ref, wg_ref, wu_ref, wd_ref, post_g_ref, o_ref, xn_ref, acc_ref):
    j = pl.program_id(1)

    @pl.when(j == 0)
    def _():
        xn_ref[...] = _rms(x_ref[...], pre_g_ref[...]).astype(BF16)
        acc_ref[...] = jnp.zeros_like(acc_ref)

    xn = xn_ref[...]
    gate = jnp.dot(xn, wg_ref[...], preferred_element_type=F32)
    up = jnp.dot(xn, wu_ref[...], preferred_element_type=F32)
    mid = (gate * _sigmoid(gate) * up).astype(BF16)
    acc_ref[...] += jnp.dot(mid, wd_ref[...], preferred_element_type=F32)

    @pl.when(j == pl.num_programs(1) - 1)
    def _():
        o_ref[...] = x_ref[...] + MACARON_WEIGHT * _rms(acc_ref[...], post_g_ref[...])


def _ffn(h, pre_g, wg, wu, wd, post_g):
    s = h.shape[0]
    tf = D_FF // 2
    return pl.pallas_call(
        _ffn_body,
        grid=(s // ROW_TILE, D_FF // tf),
        in_specs=[
            pl.BlockSpec((ROW_TILE, D_MODEL), lambda i, j: (i, 0)),
            pl.BlockSpec((1, D_MODEL), lambda i, j: (0, 0)),
            pl.BlockSpec((D_MODEL, tf), lambda i, j: (0, j)),
            pl.BlockSpec((D_MODEL, tf), lambda i, j: (0, j)),
            pl.BlockSpec((tf, D_MODEL), lambda i, j: (j, 0)),
            pl.BlockSpec((1, D_MODEL), lambda i, j: (0, 0)),
        ],
        out_specs=pl.BlockSpec((ROW_TILE, D_MODEL), lambda i, j: (i, 0)),
        out_shape=jax.ShapeDtypeStruct((s, D_MODEL), F32),
        scratch_shapes=[pltpu.VMEM((ROW_TILE, D_MODEL), BF16), pltpu.VMEM((ROW_TILE, D_MODEL), F32)],
        compiler_params=_params("arbitrary", "arbitrary"),
        name="ffn",
    )(h, pre_g, wg, wu, wd, post_g)


def _proj_body(x_ref, g_ref, w_ref, o_ref):
    xn = _rms(x_ref[...], g_ref[...]).astype(BF16)
    o_ref[...] = jnp.dot(xn, w_ref[...], preferred_element_type=F32)


def _proj_shift_body(x_ref, g_ref, w_ref, mu_ref, o_ref, carry_ref):
    @pl.when(pl.program_id(0) == 0)
    def _():
        carry_ref[...] = jnp.zeros_like(carry_ref)

    xn = _rms(x_ref[...], g_ref[...]).astype(BF16)
    p = jnp.dot(xn, w_ref[...], preferred_element_type=F32)
    rows = p.shape[0]
    prev = pltpu.roll(p, 1, axis=0)
    first = lax.broadcasted_iota(jnp.int32, p.shape, 0) == 0
    prev = jnp.where(first, carry_ref[0:1, :], prev)
    carry_ref[0:1, :] = p[rows - 1:rows, :]
    o_ref[...] = p + (prev - p) * mu_ref[...]


def _proj(h, pre_g, w, mu=None):
    s = h.shape[0]
    n = w.shape[1]
    x_spec = pl.BlockSpec((ROW_TILE, D_MODEL), lambda i: (i, 0))
    g_spec = pl.BlockSpec((1, D_MODEL), lambda i: (0, 0))
    w_spec = pl.BlockSpec((D_MODEL, n), lambda i: (0, 0))
    o_spec = pl.BlockSpec((ROW_TILE, n), lambda i: (i, 0))
    out_shape = jax.ShapeDtypeStruct((s, n), F32)
    if mu is None:
        return pl.pallas_call(
            _proj_body, grid=(s // ROW_TILE,), in_specs=[x_spec, g_spec, w_spec], out_specs=o_spec,
            out_shape=out_shape, compiler_params=_params("arbitrary"), name="proj",
        )(h, pre_g, w)
    return pl.pallas_call(
        _proj_shift_body, grid=(s // ROW_TILE,),
        in_specs=[x_spec, g_spec, w_spec, pl.BlockSpec((1, n), lambda i: (0, 0))], out_specs=o_spec,
        out_shape=out_shape, scratch_shapes=[pltpu.VMEM((8, n), F32)],
        compiler_params=_params("arbitrary"), name="proj_shift",
    )(h, pre_g, w, mu)


def _tri_inverse(a_strict, eye, blk16, lvl1, lvl2):
    d = a_strict * blk16
    d2 = _bmm(d, d)
    d4 = _bmm(d2, d2)
    d8 = _bmm(d4, d4)
    t = eye + d
    t = t + _bmm(t, d2)
    t = t + _bmm(t, d4)
    t = t + _bmm(t, d8)
    t = t + _bmm(_bmm(t, a_strict * lvl1), t)
    t = t + _bmm(_bmm(t, a_strict * lvl2), t)
    return t


def _rwkv_body(has_vres, *refs):
    if has_vres:
        (x_ref, vf_ref, w0_ref, wup_ref, a0_ref, aup_ref, gup_ref, kk_ref, ka_ref, rk_ref, gng_ref, gnb_ref,
         v0_ref, vup_ref, y_ref, state_ref) = refs
    else:
        (x_ref, w0_ref, wup_ref, a0_ref, aup_ref, gup_ref, kk_ref, ka_ref, rk_ref, gng_ref, gnb_ref,
         y_ref, vf_out_ref, state_ref) = refs

    @pl.when(pl.program_id(0) == 0)
    def _():
        state_ref[...] = jnp.zeros_like(state_ref)

    c = A_CHUNK
    w = A_WIDTH
    hd = A_HEAD_DIM
    x = x_ref[...]
    rows = x.shape[0]
    r = x[:, 0:w]
    k = x[:, w:2 * w]
    v = x[:, 2 * w:3 * w]
    wl = x[:, 3 * w:3 * w + LANES]
    al = x[:, 3 * w + LANES:3 * w + 2 * LANES]
    gl = x[:, 3 * w + 2 * LANES:3 * w + 3 * LANES]

    z = w0_ref[...] + _bdot(jnp.tanh(wl), wup_ref[...])
    softplus_neg = jnp.maximum(-z, 0.0) + jnp.log(1.0 + jnp.exp(-jnp.abs(z)))
    logw = -jnp.exp(-softplus_neg - 0.5)
    a = _sigmoid(a0_ref[...] + _bdot(al, aup_ref[...]))
    g = _bdot(_sigmoid(gl), gup_ref[...])
    if has_vres:
        vl = x[:, 3 * w + 3 * LANES:3 * w + 4 * LANES]
        v = v + (vf_ref[...] - v) * _sigmoid(v0_ref[...] + _bdot(vl, vup_ref[...]))
    else:
        vf_out_ref[...] = v

    li = lax.broadcasted_iota(jnp.int32, (LANES, LANES), 0)
    lj = lax.broadcasted_iota(jnp.int32, (LANES, LANES), 1)
    pair_bd = ((li < hd) == (lj < hd)).astype(F32)
    lane = lax.broadcasted_iota(jnp.int32, (1, LANES), 1)
    m0 = (lane < hd).astype(F32)
    m1 = (lane >= hd).astype(F32)

    def head_sum(t):
        return jnp.concatenate([_dot_sel(t[:, p * LANES:(p + 1) * LANES], pair_bd, 2) for p in range(w // LANES)],
                               axis=1)

    def by_head(t):
        return jnp.concatenate([t * m0, t * m1], axis=0)

    kkr = k * kk_ref[...]
    kk = kkr / jnp.maximum(jnp.sqrt(head_sum(kkr * kkr)), 1e-12)
    k = k * (1.0 + (a - 1.0) * ka_ref[...])

    ti = lax.broadcasted_iota(jnp.int32, (2 * c, 2 * c), 0)
    tj = lax.broadcasted_iota(jnp.int32, (2 * c, 2 * c), 1)
    same_head = (ti // c) == (tj // c)
    incl = (same_head & (tj <= ti)).astype(F32)
    strict = (same_head & (tj < ti)).astype(F32)
    eye = (tj == ti).astype(F32)
    blk16 = ((ti // 16) == (tj // 16)).astype(F32)
    lvl1 = (((ti // 32) == (tj // 32)) & ((ti // 16) != (tj // 16))).astype(F32)
    lvl2 = (same_head & ((ti // 32) != (tj // 32))).astype(F32)

    ri = lax.broadcasted_iota(jnp.int32, (rows, rows), 0)
    rj = lax.broadcasted_iota(jnp.int32, (rows, rows), 1)
    chunk_incl = (((ri // c) == (rj // c)) & (rj <= ri)).astype(F32)
    lb = _sel_dot(chunk_incl, logw, 3)
    e_neg = jnp.exp(-lb)
    alpha_t = -kk * jnp.exp(lb - logw)
    beta = kk * a
    beta_h = beta * e_neg
    k_h = k * e_neg
    r_t = r * jnp.exp(lb)

    n_ch = rows // c
    n_pair = w // LANES

    def slabs(t):
        return [t[ci * c:(ci + 1) * c, p * LANES:(p + 1) * LANES] for ci in range(n_ch) for p in range(n_pair)]

    def stack_by_head(t):
        return jnp.stack([by_head(s) for s in slabs(t)])

    xa2, v2 = stack_by_head(alpha_t), stack_by_head(v)
    gram = _bmm_nt(jnp.concatenate([xa2, stack_by_head(r_t)], axis=1),
                   jnp.concatenate([stack_by_head(beta_h), stack_by_head(k_h)], axis=1))
    a_ab = gram[:, 0:2 * c, 0:2 * c] * strict
    a_ak = gram[:, 0:2 * c, 2 * c:4 * c] * strict
    a_r = jnp.concatenate([gram[:, 2 * c:4 * c, 2 * c:4 * c] * incl, gram[:, 2 * c:4 * c, 0:2 * c] * incl], axis=2)
    t = _tri_inverse(a_ab, eye, blk16, lvl1, lvl2)
    wu = _bmm(t, jnp.concatenate([xa2, _bmm(a_ak, v2)], axis=2))
    wt = wu[:, 0:c, 0:LANES] + wu[:, c:2 * c, 0:LANES]
    ut = wu[:, 0:c, LANES:2 * LANES] + wu[:, c:2 * c, LANES:2 * LANES]
    xr = jnp.stack(slabs(r_t))
    vp = jnp.stack(slabs(v))

    st = state_ref[...]
    y_chunks = []
    for ci in range(n_ch):
        bs = slice(ci * n_pair, (ci + 1) * n_pair)
        lb_last = lb[(ci + 1) * c - 1:(ci + 1) * c, :]
        e_end = jnp.exp(lb_last - lb[ci * c:(ci + 1) * c])
        k_e = k[ci * c:(ci + 1) * c] * e_end
        beta_e = beta[ci * c:(ci + 1) * c] * e_end
        gamma_c = jnp.exp(lb_last)
        u = _bmm_nt(wt[bs], st) + ut[bs]
        u2 = jnp.concatenate([u * m0, u * m1], axis=1)
        y2 = _bmm(a_r[bs], jnp.concatenate([v2[bs], u2], axis=1))
        y = _bmm_nt(xr[bs], st) + y2[:, 0:c] + y2[:, c:2 * c]
        y_chunks.append(jnp.concatenate([y[p] for p in range(n_pair)], axis=1))
        vu_t = jnp.stack([jnp.concatenate([vp[ci * n_pair + p], u[p]], axis=0).T for p in range(n_pair)])
        ke_be = jnp.stack([jnp.concatenate([k_e[:, p * LANES:(p + 1) * LANES], beta_e[:, p * LANES:(p + 1) * LANES]],
                                           axis=0) for p in range(n_pair)])
        decay = jnp.stack([gamma_c[:, p * LANES:(p + 1) * LANES] for p in range(n_pair)])
        st = st * decay + pair_bd * _bmm(vu_t, ke_be)
    state_ref[...] = st
    y = jnp.concatenate(y_chunks, axis=0)

    inv_n = 1.0 / hd
    mean = head_sum(y) * inv_n
    yc = y - mean
    var = head_sum(yc * yc) * inv_n
    yn = yc * lax.rsqrt(var + A_GN_EPS) * gng_ref[...] + gnb_ref[...]
    bonus = head_sum(r * k * rk_ref[...]) * v
    y_ref[...] = (yn + bonus) * g


def _rwkv(x, p, v_first):
    s = x.shape[0]
    rows = A_CHUNK * A_CHUNKS_PER_STEP
    has_vres = v_first is not None
    row = lambda n: pl.BlockSpec((rows, n), lambda i: (i, 0))
    const = lambda a: pl.BlockSpec(a.shape, lambda i: (0,) * a.ndim)
    names = ["w0", "w_up", "a0", "a_up", "g_up", "k_k", "k_a", "r_k", "gn_g", "gn_b"]
    if has_vres:
        names += ["v0", "vres_up"]
    consts = [p[n] for n in names]
    ins = [x] + ([v_first] if has_vres else []) + consts
    in_specs = [row(A_IN)] + ([row(A_WIDTH)] if has_vres else []) + [const(a) for a in consts]
    y_shape = jax.ShapeDtypeStruct((s, A_WIDTH), F32)
    out = pl.pallas_call(
        functools.partial(_rwkv_body, has_vres),
        grid=(s // rows,), in_specs=in_specs,
        out_specs=row(A_WIDTH) if has_vres else [row(A_WIDTH), row(A_WIDTH)],
        out_shape=y_shape if has_vres else [y_shape, y_shape],
        scratch_shapes=[pltpu.VMEM((A_WIDTH // LANES, LANES, LANES), F32)],
        compiler_params=_params("arbitrary"), name="rwkv7",
    )(*ins)
    return (out, v_first) if has_vres else (out[0], out[1])


def _mla_prep_body(x_ref, cq_ref, sq_ref, ck_ref, sk_ref, qg_ref, wq_ref, wqs_ref, kg_ref, wk_ref, wv_ref,
                   q_ref, k_ref, vt_ref):
    x = x_ref[...]
    cq = x[:, 0:B_Q_RANK]
    ckv = x[:, B_Q_RANK:B_Q_RANK + B_KV_RANK]
    kr = x[:, B_Q_RANK + B_KV_RANK:B_Q_RANK + B_KV_RANK + LANES]
    kr_sw = x[:, B_Q_RANK + B_KV_RANK + LANES:B_Q_RANK + B_KV_RANK + 2 * LANES]
    cqn = _rms(cq, qg_ref[...]).astype(BF16)
    ckn = _rms(ckv, kg_ref[...]).astype(BF16)
    q = jnp.dot(cqn, wq_ref[...], preferred_element_type=F32)
    q_sw = jnp.dot(cqn, wqs_ref[...], preferred_element_type=F32)
    k_nope = jnp.dot(ckn, wk_ref[...], preferred_element_type=F32)
    val = jnp.dot(ckn, wv_ref[...], preferred_element_type=F32)
    k_rope = kr * ck_ref[...] + kr_sw * sk_ref[...]
    cq_t, sq_t = cq_ref[...], sq_ref[...]
    ones_col = (lax.broadcasted_iota(jnp.int32, (1, LANES), 1) == B_V).astype(F32)
    for hd in range(B_HEADS):
        sl = slice(hd * LANES, (hd + 1) * LANES)
        q_ref[hd] = (q[:, sl] * cq_t + q_sw[:, sl] * sq_t).astype(BF16)
        k_ref[hd] = (k_nope[:, sl] + k_rope).astype(BF16)
        vt_ref[hd] = (val[:, sl] + ones_col).T.astype(BF16)


def _mla_prep(x, tabs, p):
    s = x.shape[0]
    tm = ROW_TILE
    row = lambda n: pl.BlockSpec((tm, n), lambda i: (i, 0))
    const = lambda a: pl.BlockSpec(a.shape, lambda i: (0,) * a.ndim)
    consts = [p["q_norm_g"], p["w_q"], p["w_q_swap"], p["kv_norm_g"], p["w_k"], p["w_v"]]
    head_spec = pl.BlockSpec((B_HEADS, tm, LANES), lambda i: (0, i, 0))
    head_shape = jax.ShapeDtypeStruct((B_HEADS, s, LANES), BF16)
    head_t_spec = pl.BlockSpec((B_HEADS, LANES, tm), lambda i: (0, 0, i))
    head_t_shape = jax.ShapeDtypeStruct((B_HEADS, LANES, s), BF16)
    return pl.pallas_call(
        _mla_prep_body, grid=(s // tm,),
        in_specs=[row(B_IN)] + [row(LANES)] * 4 + [const(a) for a in consts],
        out_specs=[head_spec, head_spec, head_t_spec], out_shape=[head_shape, head_shape, head_t_shape],
        compiler_params=_params("arbitrary"), name="mla_prep",
    )(x, *tabs, *consts)


def _flash_body(q_ref, k_ref, vt_ref, o_ref, m_ref, acc_ref):
    i = pl.program_id(1)
    tile = q_ref.shape[1]
    sub = tile // B_SUBTILES
    m_ref[...] = jnp.full_like(m_ref, -jnp.inf)
    acc_ref[...] = jnp.zeros_like(acc_ref)

    def step(j, on_diagonal):
        start = pl.multiple_of(j * tile, tile)
        scores = []
        for a in range(B_SUBTILES):
            n_keys = (a + 1) * sub if on_diagonal else tile
            s = lax.dot_general(k_ref[0, pl.ds(start, n_keys), :], q_ref[0, a * sub:(a + 1) * sub, :],
                                (((1,), (1,)), ((), ())), preferred_element_type=F32)
            if on_diagonal:
                keys = lax.broadcasted_iota(jnp.int32, s.shape, 0)
                queries = lax.broadcasted_iota(jnp.int32, s.shape, 1) + a * sub
                s = jnp.where(keys <= queries, s, -jnp.inf)
            scores.append(s)
        for a, s in enumerate(scores):
            cs = slice(a * sub, (a + 1) * sub)
            m_old = m_ref[:, cs]
            m_new = jnp.maximum(m_old, jnp.max(s, axis=0, keepdims=True))
            p = jnp.exp2(s - m_new).astype(BF16)
            pv = jnp.dot(vt_ref[0, :, pl.ds(start, s.shape[0])], p, preferred_element_type=F32)
            acc_ref[:, cs] = jnp.exp2(m_old - m_new) * acc_ref[:, cs] + pv
            m_ref[:, cs] = m_new

    def off_diagonal(j, carry):
        step(j, False)
        return carry

    lax.fori_loop(0, i, off_diagonal, 0)
    step(i, True)
    acc = acc_ref[...]
    o_ref[...] = (acc / acc[B_V:B_V + 1, :]).T


def _flash(q, k, vt, tile):
    n_h, s, _ = q.shape
    return pl.pallas_call(
        _flash_body, grid=(n_h, s // tile),
        in_specs=[pl.BlockSpec((1, tile, LANES), lambda h, i: (h, i, 0)),
                  pl.BlockSpec((1, s, LANES), lambda h, i: (h, 0, 0)),
                  pl.BlockSpec((1, LANES, s), lambda h, i: (h, 0, 0))],
        out_specs=pl.BlockSpec((tile, LANES), lambda h, i: (i, h)),
        out_shape=jax.ShapeDtypeStruct((s, n_h * LANES), F32),
        scratch_shapes=[pltpu.VMEM((1, tile), F32), pltpu.VMEM((LANES, tile), F32)],
        compiler_params=_params("arbitrary", "arbitrary"), name="mla_flash",
    )(q, k, vt)


def _hgrn_body(x_ref, lb_ref, ng_ref, o_ref, state_ref):
    @pl.when(pl.program_id(0) == 0)
    def _():
        state_ref[...] = jnp.zeros_like(state_ref)

    c = C_CHUNK
    w = C_WIDTH
    x = x_ref[...]
    lower = lb_ref[...]
    fz = x[:, w:2 * w]
    f = lower + (1.0 - lower) * _sigmoid(fz)
    logf = jnp.log(jnp.maximum(f, C_MIN_FORGET))
    kf = (1.0 - lower) * _sigmoid(-fz)
    cq = x[:, 0:w]
    q = cq * _sigmoid(cq)
    val = x[:, 2 * w:3 * w]
    cg = x[:, 3 * w:4 * w]

    ti = lax.broadcasted_iota(jnp.int32, (c, c), 0)
    tj = lax.broadcasted_iota(jnp.int32, (c, c), 1)
    row = lax.broadcasted_iota(jnp.int32, (c, 1), 0)
    b = _sel_dot((tj <= ti).astype(F32), logf, 3)
    b_last = b[c - 1:c, :]

    n0 = C_SUB
    ref = _sel_dot((tj == (ti // n0) * n0 + n0 // 2 - 1).astype(F32), b, 3)
    q_s = q * jnp.exp(b - ref)
    k_s = kf * jnp.exp(ref - b)
    mask = ((ti // n0) == (tj // n0)) & (tj <= ti)
    attn = [jnp.where(mask, _bdot_nt(q_s[:, hd * C_DIM:(hd + 1) * C_DIM], k_s[:, hd * C_DIM:(hd + 1) * C_DIM]), 0.0)
            for hd in range(C_HEADS)]
    n = n0
    while n < c:
        ref = _sel_dot((tj == (ti // (2 * n)) * (2 * n) + n - 1).astype(F32), b, 3)
        right = ((row // n) % 2) == 1
        q_s = q * jnp.exp(jnp.where(right, b - ref, 0.0))
        k_s = kf * jnp.exp(jnp.where(right, 0.0, ref - b))
        mask = ((ti // (2 * n)) == (tj // (2 * n))) & (((ti // n) % 2) == 1) & (((tj // n) % 2) == 0)
        attn = [jnp.where(mask, _bdot_nt(q_s[:, hd * C_DIM:(hd + 1) * C_DIM], k_s[:, hd * C_DIM:(hd + 1) * C_DIM]),
                          attn[hd]) for hd in range(C_HEADS)]
        n *= 2

    q_in = q * jnp.exp(b)
    k_end = kf * jnp.exp(b_last - b)
    g_end = jnp.exp(b_last)
    outs = []
    for hd in range(C_HEADS):
        sl = slice(hd * C_DIM, (hd + 1) * C_DIM)
        st = state_ref[hd]
        vh = val[:, sl]
        o = _bdot(attn[hd], vh) + _bdot_nt(q_in[:, sl], st)
        state_ref[hd] = st * g_end[:, sl] + _bdot(vh.T, k_end[:, sl])
        g = cg[:, sl]
        outs.append(_rms(o, ng_ref[...]) * (g * _sigmoid(g)))
    o_ref[...] = jnp.concatenate(outs, axis=1)


def _hgrn(x, lower, norm_g):
    s = x.shape[0]
    c = C_CHUNK
    return pl.pallas_call(
        _hgrn_body, grid=(s // c,),
        in_specs=[pl.BlockSpec((c, C_IN), lambda i: (i, 0)), pl.BlockSpec((1, C_WIDTH), lambda i: (0, 0)),
                  pl.BlockSpec((1, C_DIM), lambda i: (0, 0))],
        out_specs=pl.BlockSpec((c, C_WIDTH), lambda i: (i, 0)),
        out_shape=jax.ShapeDtypeStruct((s, C_WIDTH), F32),
        scratch_shapes=[pltpu.VMEM((C_HEADS, C_DIM, C_DIM), F32)],
        compiler_params=_params("arbitrary"), name="hgrn2",
    )(x, lower, norm_g)


def _merge_body(h_ref, ya_ref, yb_ref, yc_ref, g_ref, wg_ref, wa_ref, wb_ref, wc_ref, wo_ref, pg_ref, o_ref):
    d = D_MODEL
    h = h_ref[...]
    u = _rms(h, g_ref[...]).astype(BF16)
    merged = None
    for n, (y_ref, w_ref) in enumerate(((ya_ref, wa_ref), (yb_ref, wb_ref), (yc_ref, wc_ref))):
        gate = _sigmoid(jnp.dot(u, wg_ref[:, n * d:(n + 1) * d], preferred_element_type=F32))
        term = gate * _bdot(y_ref[...], w_ref[...])
        merged = term if merged is None else merged + term
    o_ref[...] = h + _rms(_bdot(merged, wo_ref[...]), pg_ref[...])


def _merge(h, ya, yb, yc, pre_g, wg, wa, wb, wc, wo, post_g):
    s = h.shape[0]
    tm = ROW_TILE
    row = lambda n: pl.BlockSpec((tm, n), lambda i: (i, 0))
    const = lambda a: pl.BlockSpec(a.shape, lambda i: (0,) * a.ndim)
    consts = [pre_g, wg, wa, wb, wc, wo, post_g]
    return pl.pallas_call(
        _merge_body, grid=(s // tm,),
        in_specs=[row(D_MODEL), row(A_WIDTH), row(B_HEADS * LANES), row(C_WIDTH)] + [const(a) for a in consts],
        out_specs=row(D_MODEL), out_shape=jax.ShapeDtypeStruct((s, D_MODEL), F32),
        compiler_params=_params("arbitrary"), name="merge",
    )(h, ya, yb, yc, *consts)


def _pad_cols(a, n):
    return jnp.pad(a, ((0, 0), (0, n - a.shape[1])))


def _pad_rows(a, n):
    return jnp.pad(a, ((0, n - a.shape[0]), (0, 0)))


def _rope_half_swap(a):
    half = B_ROPE // 2
    return jnp.concatenate([a[:, half:], a[:, :half]], axis=1)


def _layer_weights(l, w_in, rwkv_mu, vres_down, vres_mu, w_uq, w_ukv, mla_out):
    wi = w_in[l]
    d = D_MODEL
    o = 3 * A_WIDTH
    lora = [wi[:, o:o + 64], wi[:, o + 64:o + 128], wi[:, o + 128:o + 256]]
    mus = [rwkv_mu[l][o:o + 64], rwkv_mu[l][o + 64:o + 128], rwkv_mu[l][o + 128:o + 256]]
    if l > 0:
        lora.append(vres_down[l - 1])
        mus.append(vres_mu[l - 1])
    else:
        lora.append(jnp.zeros((d, 0), F32))
        mus.append(jnp.zeros((0,), F32))
    w_a = jnp.concatenate([wi[:, :o]] + [_pad_cols(t, LANES) for t in lora], axis=1)
    mu_a = jnp.concatenate([rwkv_mu[l][:o]] + [jnp.pad(t, (0, LANES - t.shape[0])) for t in mus])[None, :]

    o = A_COLS
    w_kr = wi[:, o + B_Q_RANK + B_KV_RANK:o + B_Q_RANK + B_KV_RANK + B_ROPE]
    place = lambda t: jnp.pad(t, ((0, 0), (B_NOPE, LANES - B_NOPE - B_ROPE)))
    w_b = jnp.concatenate([wi[:, o:o + B_Q_RANK + B_KV_RANK], place(w_kr), place(_rope_half_swap(w_kr))], axis=1)
    o += B_Q_RANK + B_KV_RANK + B_ROPE
    w_c = wi[:, o:o + C_IN]
    w_g = wi[:, o + C_IN:o + C_IN + N_BRANCH * d]

    scale = (B_NOPE + B_ROPE) ** -0.5 * np.log2(np.e)
    uq = (w_uq[l] * scale).reshape(B_Q_RANK, B_HEADS, B_NOPE + B_ROPE)
    uq_sw = jnp.concatenate([jnp.zeros_like(uq[..., :B_NOPE]), uq[..., B_NOPE + B_ROPE // 2:],
                             uq[..., B_NOPE:B_NOPE + B_ROPE // 2]], axis=-1)
    pad_head = lambda t: jnp.pad(t, ((0, 0), (0, 0), (0, LANES - t.shape[-1]))).reshape(t.shape[0], B_HEADS * LANES)
    ukv = w_ukv[l].reshape(B_KV_RANK, B_HEADS, B_NOPE + B_V)
    mo = jnp.pad(mla_out[l].reshape(B_HEADS, B_V, d), ((0, 0), (0, LANES - B_V), (0, 0))).reshape(B_HEADS * LANES, d)
    bf = lambda t: t.astype(BF16)
    return dict(w_a=bf(w_a), mu_a=mu_a, w_b=bf(w_b), w_c=bf(w_c), w_g=bf(w_g),
                w_q=bf(pad_head(uq)), w_q_swap=bf(pad_head(uq_sw)),
                w_k=bf(pad_head(ukv[..., :B_NOPE])), w_v=bf(pad_head(ukv[..., B_NOPE:])), mla_out=bf(mo))


def _rope_tables(positions):
    inv_freq = ROPE_THETA ** (-jnp.arange(0, B_ROPE, 2, dtype=F32) / B_ROPE)
    ang = positions.astype(F32)[:, None] * inv_freq
    cos, sin = jnp.cos(ang), jnp.sin(ang)
    s = positions.shape[0]
    pad = jnp.zeros((s, LANES - B_NOPE - B_ROPE), F32)
    cos_q = jnp.concatenate([jnp.ones((s, B_NOPE), F32), cos, cos, pad], axis=1)
    sin_t = jnp.concatenate([jnp.zeros((s, B_NOPE), F32), -sin, sin, pad], axis=1)
    cos_k = jnp.concatenate([jnp.zeros((s, B_NOPE), F32), cos, cos, pad], axis=1)
    return cos_q, sin_t, cos_k, sin_t


def kernel(x, positions, ffn1_pre_g, ffn1_post_g, ffn1_w_gate, ffn1_w_up, ffn1_w_down, mix_pre_g, mix_post_g, w_in, rwkv_mu, rwkv_w0, rwkv_w_up, rwkv_a0, rwkv_a_up, rwkv_g_up, rwkv_k_k, rwkv_k_a, rwkv_r_k, rwkv_gn_g, rwkv_gn_b, rwkv_vres_down, rwkv_vres_mu, rwkv_vres_up, rwkv_v0, rwkv_out, mla_q_norm_g, mla_w_uq, mla_kv_norm_g, mla_w_ukv, mla_out, hgrn_lower_bounds, hgrn_norm_g, hgrn_out, w_o, ffn2_pre_g, ffn2_post_g, ffn2_w_gate, ffn2_w_up, ffn2_w_down):
    bsz, seq, d = x.shape
    assert bsz == 1 and d == D_MODEL and seq % max(ROW_TILE, C_CHUNK, A_CHUNK) == 0
    depth = w_in.shape[0]
    flash_tile = min(1024, seq)
    tabs = _rope_tables(positions[0])
    lb_p = jax.nn.softmax(hgrn_lower_bounds.astype(F32), axis=0)
    lower_bounds = jnp.cumsum(lb_p, axis=0) - lb_p[0]
    row = lambda t: t[None, :]
    bf = lambda t: t.astype(BF16)

    h = x[0]
    v_first = None
    for l in range(depth):
        h = _ffn(h, row(ffn1_pre_g[l]), bf(ffn1_w_gate[l]), bf(ffn1_w_up[l]), bf(ffn1_w_down[l]),
                 row(ffn1_post_g[l]))

        lw = _layer_weights(l, w_in, rwkv_mu, rwkv_vres_down, rwkv_vres_mu, mla_w_uq, mla_w_ukv, mla_out)
        pre_g = row(mix_pre_g[l])
        x_a = _proj(h, pre_g, lw["w_a"], lw["mu_a"])
        x_b = _proj(h, pre_g, lw["w_b"])
        x_c = _proj(h, pre_g, lw["w_c"])

        pa = dict(w0=row(rwkv_w0[l]), w_up=bf(_pad_rows(rwkv_w_up[l], LANES)), a0=row(rwkv_a0[l]),
                  a_up=bf(_pad_rows(rwkv_a_up[l], LANES)), g_up=bf(rwkv_g_up[l]), k_k=row(rwkv_k_k[l]),
                  k_a=row(rwkv_k_a[l]), r_k=row(rwkv_r_k[l].reshape(-1)), gn_g=row(rwkv_gn_g[l]),
                  gn_b=row(rwkv_gn_b[l]))
        if l > 0:
            pa.update(v0=row(rwkv_v0[l - 1]), vres_up=bf(_pad_rows(rwkv_vres_up[l - 1], LANES)))
        y_a, v_first = _rwkv(x_a, pa, v_first)

        pb = dict(q_norm_g=row(mla_q_norm_g[l]), kv_norm_g=row(mla_kv_norm_g[l]), w_q=lw["w_q"],
                  w_q_swap=lw["w_q_swap"], w_k=lw["w_k"], w_v=lw["w_v"])
        q_h, k_h, vt_h = _mla_prep(x_b, tabs, pb)
        y_b = _flash(q_h, k_h, vt_h, flash_tile)

        y_c = _hgrn(x_c, row(lower_bounds[l]), row(hgrn_norm_g[l]))

        h = _merge(h, y_a, y_b, y_c, pre_g, lw["w_g"], bf(rwkv_out[l]), lw["mla_out"], bf(hgrn_out[l]),
                   bf(w_o[l]), row(mix_post_g[l]))

        h = _ffn(h, row(ffn2_pre_g[l]), bf(ffn2_w_gate[l]), bf(ffn2_w_up[l]), bf(ffn2_w_down[l]),
                 row(ffn2_post_g[l]))
    return h[None]
```

```python
import functools

import jax
import jax.numpy as jnp
import numpy as np
from jax import lax
from jax.experimental import pallas as pl
from jax.experimental.pallas import tpu as pltpu

F32 = jnp.float32
BF16 = jnp.bfloat16

D_MODEL = 1024
D_FF = 2816
NORM_EPS = 1e-6
MACARON_WEIGHT = 0.5

A_HEADS = 8
A_HEAD_DIM = 64
A_WIDTH = 512
A_GN_EPS = 64e-5
A_CHUNK = 64
A_CHUNKS_PER_STEP = 4
A_COLS = 1792
A_IN = 2048

B_HEADS = 8
B_NOPE = 64
B_ROPE = 32
B_V = 64
B_Q_RANK = 384
B_KV_RANK = 256
B_IN = 896
ROPE_THETA = 10000.0
B_SUBTILES = 4
LANES = 128

C_HEADS = 4
C_DIM = 128
C_WIDTH = 512
C_IN = 2048
C_CHUNK = 128
C_SUB = 8
C_MIN_FORGET = 1e-6

N_BRANCH = 3
ROW_TILE = 512
VMEM_LIMIT = 56 * 1024 * 1024

def _bdot(a, b):
    return jnp.dot(a.astype(BF16), b.astype(BF16), preferred_element_type=F32)


def _bdot_nt(a, b):
    return lax.dot_general(a.astype(BF16), b.astype(BF16), (((1,), (1,)), ((), ())),
                           preferred_element_type=F32)


def _bmm(a, b):
    return lax.dot_general(a.astype(BF16), b.astype(BF16), (((2,), (1,)), ((0,), (0,))),
                           preferred_element_type=F32)


def _bmm_nt(a, b):
    return lax.dot_general(a.astype(BF16), b.astype(BF16), (((2,), (2,)), ((0,), (0,))),
                           preferred_element_type=F32)


def _split_terms(data, parts):
    terms, rem = [], data
    for _ in range(parts):
        piece = rem.astype(BF16)
        terms.append(piece)
        rem = rem - piece.astype(F32)
    return terms


def _sel_dot(sel, data, parts):
    sel = sel.astype(BF16)
    return sum(jnp.dot(sel, t, preferred_element_type=F32) for t in _split_terms(data, parts))


def _dot_sel(data, sel, parts):
    sel = sel.astype(BF16)
    return sum(jnp.dot(t, sel, preferred_element_type=F32) for t in _split_terms(data, parts))


def _rms(x, g):
    return x * lax.rsqrt(jnp.mean(x * x, axis=-1, keepdims=True) + NORM_EPS) * g


def _sigmoid(x):
    return 1.0 / (1.0 + jnp.exp(-x))


def _params(*sem):
    return pltpu.CompilerParams(dimension_semantics=sem, vmem_limit_bytes=VMEM_LIMIT)


def _ffn_body(x_ref, pre_g_ref, wg_ref, wu_ref, wd_ref, post_g_ref, o_ref, xn_ref, acc_ref):
    j = pl.program_id(1)

    @pl.when(j == 0)
    def _():
        xn_ref[...] = _rms(x_ref[...], pre_g_ref[...]).astype(BF16)
        acc_ref[...] = jnp.zeros_like(acc_ref)

    xn = xn_ref[...]
    gate = jnp.dot(xn, wg_ref[...], preferred_element_type=F32)
    up = jnp.dot(xn, wu_ref[...], preferred_element_type=F32)
    mid = (gate * _sigmoid(gate) * up).astype(BF16)
    acc_ref[...] += jnp.dot(mid, wd_ref[...], preferred_element_type=F32)

    @pl.when(j == pl.num_programs(1) - 1)
    def _():
        o_ref[...] = x_ref[...] + MACARON_WEIGHT * _rms(acc_ref[...], post_g_ref[...])


def _ffn(h, pre_g, wg, wu, wd, post_g):
    s = h.shape[0]
    tf = D_FF // 2
    return pl.pallas_call(
        _ffn_body,
        grid=(s // ROW_TILE, D_FF // tf),
        in_specs=[
            pl.BlockSpec((ROW_TILE, D_MODEL), lambda i, j: (i, 0)),
            pl.BlockSpec((1, D_MODEL), lambda i, j: (0, 0)),
            pl.BlockSpec((D_MODEL, tf), lambda i, j: (0, j)),
            pl.BlockSpec((D_MODEL, tf), lambda i, j: (0, j)),
            pl.BlockSpec((tf, D_MODEL), lambda i, j: (j, 0)),
            pl.BlockSpec((1, D_MODEL), lambda i, j: (0, 0)),
        ],
        out_specs=pl.BlockSpec((ROW_TILE, D_MODEL), lambda i, j: (i, 0)),
        out_shape=jax.ShapeDtypeStruct((s, D_MODEL), F32),
        scratch_shapes=[pltpu.VMEM((ROW_TILE, D_MODEL), BF16), pltpu.VMEM((ROW_TILE, D_MODEL), F32)],
        compiler_params=_params("arbitrary", "arbitrary"),
        name="ffn",
    )(h, pre_g, wg, wu, wd, post_g)


def _proj_body(x_ref, g_ref, w_ref, o_ref):
    xn = _rms(x_ref[...], g_ref[...]).astype(BF16)
    o_ref[...] = jnp.dot(xn, w_ref[...], preferred_element_type=F32)


def _proj_shift_body(x_ref, g_ref, w_ref, mu_ref, o_ref, carry_ref):
    @pl.when(pl.program_id(0) == 0)
    def _():
        carry_ref[...] = jnp.zeros_like(carry_ref)

    xn = _rms(x_ref[...], g_ref[...]).astype(BF16)
    p = jnp.dot(xn, w_ref[...], preferred_element_type=F32)
    rows = p.shape[0]
    prev = pltpu.roll(p, 1, axis=0)
    first = lax.broadcasted_iota(jnp.int32, p.shape, 0) == 0
    prev = jnp.where(first, carry_ref[0:1, :], prev)
    carry_ref[0:1, :] = p[rows - 1:rows, :]
    o_ref[...] = p + (prev - p) * mu_ref[...]


def _proj(h, pre_g, w, mu=None):
    s = h.shape[0]
    n = w.shape[1]
    x_spec = pl.BlockSpec((ROW_TILE, D_MODEL), lambda i: (i, 0))
    g_spec = pl.BlockSpec((1, D_MODEL), lambda i: (0, 0))
    w_spec = pl.BlockSpec((D_MODEL, n), lambda i: (0, 0))
    o_spec = pl.BlockSpec((ROW_TILE, n), lambda i: (i, 0))
    out_shape = jax.ShapeDtypeStruct((s, n), F32)
    if mu is None:
        return pl.pallas_call(
            _proj_body, grid=(s // ROW_TILE,), in_specs=[x_spec, g_spec, w_spec], out_specs=o_spec,
            out_shape=out_shape, compiler_params=_params("arbitrary"), name="proj",
        )(h, pre_g, w)
    return pl.pallas_call(
        _proj_shift_body, grid=(s // ROW_TILE,),
        in_specs=[x_spec, g_spec, w_spec, pl.BlockSpec((1, n), lambda i: (0, 0))], out_specs=o_spec,
        out_shape=out_shape, scratch_shapes=[pltpu.VMEM((8, n), F32)],
        compiler_params=_params("arbitrary"), name="proj_shift",
    )(h, pre_g, w, mu)


def _tri_inverse(a_strict, eye, blk16, lvl1, lvl2):
    d = a_strict * blk16
    d2 = _bmm(d, d)
    d4 = _bmm(d2, d2)
    d8 = _bmm(d4, d4)
    t = eye + d
    t = t + _bmm(t, d2)
    t = t + _bmm(t, d4)
    t = t + _bmm(t, d8)
    t = t + _bmm(_bmm(t, a_strict * lvl1), t)
    t = t + _bmm(_bmm(t, a_strict * lvl2), t)
    return t


def _rwkv_body(has_vres, *refs):
    if has_vres:
        (x_ref, vf_ref, w0_ref, wup_ref, a0_ref, aup_ref, gup_ref, kk_ref, ka_ref, rk_ref, gng_ref, gnb_ref,
         v0_ref, vup_ref, y_ref, state_ref) = refs
    else:
        (x_ref, w0_ref, wup_ref, a0_ref, aup_ref, gup_ref, kk_ref, ka_ref, rk_ref, gng_ref, gnb_ref,
         y_ref, vf_out_ref, state_ref) = refs

    @pl.when(pl.program_id(0) == 0)
    def _():
        state_ref[...] = jnp.zeros_like(state_ref)

    c = A_CHUNK
    w = A_WIDTH
    hd = A_HEAD_DIM
    x = x_ref[...]
    rows = x.shape[0]
    r = x[:, 0:w]
    k = x[:, w:2 * w]
    v = x[:, 2 * w:3 * w]
    wl = x[:, 3 * w:3 * w + LANES]
    al = x[:, 3 * w + LANES:3 * w + 2 * LANES]
    gl = x[:, 3 * w + 2 * LANES:3 * w + 3 * LANES]

    z = w0_ref[...] + _bdot(jnp.tanh(wl), wup_ref[...])
    softplus_neg = jnp.maximum(-z, 0.0) + jnp.log(1.0 + jnp.exp(-jnp.abs(z)))
    logw = -jnp.exp(-softplus_neg - 0.5)
    a = _sigmoid(a0_ref[...] + _bdot(al, aup_ref[...]))
    g = _bdot(_sigmoid(gl), gup_ref[...])
    if has_vres:
        vl = x[:, 3 * w + 3 * LANES:3 * w + 4 * LANES]
        v = v + (vf_ref[...] - v) * _sigmoid(v0_ref[...] + _bdot(vl, vup_ref[...]))
    else:
        vf_out_ref[...] = v

    li = lax.broadcasted_iota(jnp.int32, (LANES, LANES), 0)
    lj = lax.broadcasted_iota(jnp.int32, (LANES, LANES), 1)
    pair_bd = ((li < hd) == (lj < hd)).astype(F32)
    lane = lax.broadcasted_iota(jnp.int32, (1, LANES), 1)
    m0 = (lane < hd).astype(F32)
    m1 = (lane >= hd).astype(F32)

    def head_sum(t):
        return jnp.concatenate([_dot_sel(t[:, p * LANES:(p + 1) * LANES], pair_bd, 2) for p in range(w // LANES)],
                               axis=1)

    def by_head(t):
        return jnp.concatenate([t * m0, t * m1], axis=0)

    kkr = k * kk_ref[...]
    kk = kkr / jnp.maximum(jnp.sqrt(head_sum(kkr * kkr)), 1e-12)
    k = k * (1.0 + (a - 1.0) * ka_ref[...])

    ti = lax.broadcasted_iota(jnp.int32, (2 * c, 2 * c), 0)
    tj = lax.broadcasted_iota(jnp.int32, (2 * c, 2 * c), 1)
    same_head = (ti // c) == (tj // c)
    incl = (same_head & (tj <= ti)).astype(F32)
    strict = (same_head & (tj < ti)).astype(F32)
    eye = (tj == ti).astype(F32)
    blk16 = ((ti // 16) == (tj // 16)).astype(F32)
    lvl1 = (((ti // 32) == (tj // 32)) & ((ti // 16) != (tj // 16))).astype(F32)
    lvl2 = (same_head & ((ti // 32) != (tj // 32))).astype(F32)

    ri = lax.broadcasted_iota(jnp.int32, (rows, rows), 0)
    rj = lax.broadcasted_iota(jnp.int32, (rows, rows), 1)
    chunk_incl = (((ri // c) == (rj // c)) & (rj <= ri)).astype(F32)
    lb = _sel_dot(chunk_incl, logw, 3)
    e_neg = jnp.exp(-lb)
    alpha_t = -kk * jnp.exp(lb - logw)
    beta = kk * a
    beta_h = beta * e_neg
    k_h = k * e_neg
    r_t = r * jnp.exp(lb)

    n_ch = rows // c
    n_pair = w // LANES

    def slabs(t):
        return [t[ci * c:(ci + 1) * c, p * LANES:(p + 1) * LANES] for ci in range(n_ch) for p in range(n_pair)]

    def stack_by_head(t):
        return jnp.stack([by_head(s) for s in slabs(t)])

    xa2, v2 = stack_by_head(alpha_t), stack_by_head(v)
    gram = _bmm_nt(jnp.concatenate([xa2, stack_by_head(r_t)], axis=1),
                   jnp.concatenate([stack_by_head(beta_h), stack_by_head(k_h)], axis=1))
    a_ab = gram[:, 0:2 * c, 0:2 * c] * strict
    a_ak = gram[:, 0:2 * c, 2 * c:4 * c] * strict
    a_r = jnp.concatenate([gram[:, 2 * c:4 * c, 2 * c:4 * c] * incl, gram[:, 2 * c:4 * c, 0:2 * c] * incl], axis=2)
    t = _tri_inverse(a_ab, eye, blk16, lvl1, lvl2)
    wu = _bmm(t, jnp.concatenate([xa2, _bmm(a_ak, v2)], axis=2))
    wt = wu[:, 0:c, 0:LANES] + wu[:, c:2 * c, 0:LANES]
    ut = wu[:, 0:c, LANES:2 * LANES] + wu[:, c:2 * c, LANES:2 * LANES]
    xr = jnp.stack(slabs(r_t))
    vp = jnp.stack(slabs(v))

    st = state_ref[...]
    y_chunks = []
    for ci in range(n_ch):
        bs = slice(ci * n_pair, (ci + 1) * n_pair)
        lb_last = lb[(ci + 1) * c - 1:(ci + 1) * c, :]
        e_end = jnp.exp(lb_last - lb[ci * c:(ci + 1) * c])
        k_e = k[ci * c:(ci + 1) * c] * e_end
        beta_e = beta[ci * c:(ci + 1) * c] * e_end
        gamma_c = jnp.exp(lb_last)
        u = _bmm_nt(wt[bs], st) + ut[bs]
        u2 = jnp.concatenate([u * m0, u * m1], axis=1)
        y2 = _bmm(a_r[bs], jnp.concatenate([v2[bs], u2], axis=1))
        y = _bmm_nt(xr[bs], st) + y2[:, 0:c] + y2[:, c:2 * c]
        y_chunks.append(jnp.concatenate([y[p] for p in range(n_pair)], axis=1))
        vu_t = jnp.stack([jnp.concatenate([vp[ci * n_pair + p], u[p]], axis=0).T for p in range(n_pair)])
        ke_be = jnp.stack([jnp.concatenate([k_e[:, p * LANES:(p + 1) * LANES], beta_e[:, p * LANES:(p + 1) * LANES]],
                                           axis=0) for p in range(n_pair)])
        decay = jnp.stack([gamma_c[:, p * LANES:(p + 1) * LANES] for p in range(n_pair)])
        st = st * decay + pair_bd * _bmm(vu_t, ke_be)
    state_ref[...] = st
    y = jnp.concatenate(y_chunks, axis=0)

    inv_n = 1.0 / hd
    mean = head_sum(y) * inv_n
    yc = y - mean
    var = head_sum(yc * yc) * inv_n
    yn = yc * lax.rsqrt(var + A_GN_EPS) * gng_ref[...] + gnb_ref[...]
    bonus = head_sum(r * k * rk_ref[...]) * v
    y_ref[...] = (yn + bonus) * g


def _rwkv(x, p, v_first):
    s = x.shape[0]
    rows = A_CHUNK * A_CHUNKS_PER_STEP
    has_vres = v_first is not None
    row = lambda n: pl.BlockSpec((rows, n), lambda i: (i, 0))
    const = lambda a: pl.BlockSpec(a.shape, lambda i: (0,) * a.ndim)
    names = ["w0", "w_up", "a0", "a_up", "g_up", "k_k", "k_a", "r_k", "gn_g", "gn_b"]
    if has_vres:
        names += ["v0", "vres_up"]
    consts = [p[n] for n in names]
    ins = [x] + ([v_first] if has_vres else []) + consts
    in_specs = [row(A_IN)] + ([row(A_WIDTH)] if has_vres else []) + [const(a) for a in consts]
    y_shape = jax.ShapeDtypeStruct((s, A_WIDTH), F32)
    out = pl.pallas_call(
        functools.partial(_rwkv_body, has_vres),
        grid=(s // rows,), in_specs=in_specs,
        out_specs=row(A_WIDTH) if has_vres else [row(A_WIDTH), row(A_WIDTH)],
        out_shape=y_shape if has_vres else [y_shape, y_shape],
        scratch_shapes=[pltpu.VMEM((A_WIDTH // LANES, LANES, LANES), F32)],
        compiler_params=_params("arbitrary"), name="rwkv7",
    )(*ins)
    return (out, v_first) if has_vres else (out[0], out[1])


def _mla_prep_body(x_ref, cq_ref, sq_ref, ck_ref, sk_ref, qg_ref, wq_ref, wqs_ref, kg_ref, wk_ref, wv_ref,
                   q_ref, k_ref, vt_ref):
    x = x_ref[...]
    cq = x[:, 0:B_Q_RANK]
    ckv = x[:, B_Q_RANK:B_Q_RANK + B_KV_RANK]
    kr = x[:, B_Q_RANK + B_KV_RANK:B_Q_RANK + B_KV_RANK + LANES]
    kr_sw = x[:, B_Q_RANK + B_KV_RANK + LANES:B_Q_RANK + B_KV_RANK + 2 * LANES]
    cqn = _rms(cq, qg_ref[...]).astype(BF16)
    ckn = _rms(ckv, kg_ref[...]).astype(BF16)
    q = jnp.dot(cqn, wq_ref[...], preferred_element_type=F32)
    q_sw = jnp.dot(cqn, wqs_ref[...], preferred_element_type=F32)
    k_nope = jnp.dot(ckn, wk_ref[...], preferred_element_type=F32)
    val = jnp.dot(ckn, wv_ref[...], preferred_element_type=F32)
    k_rope = kr * ck_ref[...] + kr_sw * sk_ref[...]
    cq_t, sq_t = cq_ref[...], sq_ref[...]
    ones_col = (lax.broadcasted_iota(jnp.int32, (1, LANES), 1) == B_V).astype(F32)
    for hd in range(B_HEADS):
        sl = slice(hd * LANES, (hd + 1) * LANES)
        q_ref[hd] = (q[:, sl] * cq_t + q_sw[:, sl] * sq_t).astype(BF16)
        k_ref[hd] = (k_nope[:, sl] + k_rope).astype(BF16)
        vt_ref[hd] = (val[:, sl] + ones_col).T.astype(BF16)


def _mla_prep(x, tabs, p):
    s = x.shape[0]
    tm = ROW_TILE
    row = lambda n: pl.BlockSpec((tm, n), lambda i: (i, 0))
    const = lambda a: pl.BlockSpec(a.shape, lambda i: (0,) * a.ndim)
    consts = [p["q_norm_g"], p["w_q"], p["w_q_swap"], p["kv_norm_g"], p["w_k"], p["w_v"]]
    head_spec = pl.BlockSpec((B_HEADS, tm, LANES), lambda i: (0, i, 0))
    head_shape = jax.ShapeDtypeStruct((B_HEADS, s, LANES), BF16)
    head_t_spec = pl.BlockSpec((B_HEADS, LANES, tm), lambda i: (0, 0, i))
    head_t_shape = jax.ShapeDtypeStruct((B_HEADS, LANES, s), BF16)
    return pl.pallas_call(
        _mla_prep_body, grid=(s // tm,),
        in_specs=[row(B_IN)] + [row(LANES)] * 4 + [const(a) for a in consts],
        out_specs=[head_spec, head_spec, head_t_spec], out_shape=[head_shape, head_shape, head_t_shape],
        compiler_params=_params("arbitrary"), name="mla_prep",
    )(x, *tabs, *consts)


def _flash_body(q_ref, k_ref, vt_ref, o_ref, m_ref, acc_ref, s0_ref, s1_ref):
    i = pl.program_id(1)
    tile = q_ref.shape[1]
    sub = tile // B_SUBTILES
    m_ref[...] = jnp.full_like(m_ref, -jnp.inf)
    acc_ref[...] = jnp.zeros_like(acc_ref)

    def scores(j, s_ref, a):
        start = pl.multiple_of(j * tile, tile)
        s_ref[:, a * sub:(a + 1) * sub] = lax.dot_general(
            k_ref[0, pl.ds(start, tile), :], q_ref[0, a * sub:(a + 1) * sub, :], (((1,), (1,)), ((), ())),
            preferred_element_type=F32)

    def softmax_pv(j, s_ref, a, on_diagonal):
        start = pl.multiple_of(j * tile, tile)
        cs = slice(a * sub, (a + 1) * sub)
        n_keys = (a + 1) * sub if on_diagonal else tile
        s = s_ref[0:n_keys, cs]
        if on_diagonal:
            keys = lax.broadcasted_iota(jnp.int32, s.shape, 0)
            queries = lax.broadcasted_iota(jnp.int32, s.shape, 1) + a * sub
            s = jnp.where(keys <= queries, s, -jnp.inf)
        m_old = m_ref[:, cs]
        m_new = jnp.maximum(m_old, jnp.max(s, axis=0, keepdims=True))
        p = jnp.exp2(s - m_new).astype(BF16)
        pv = jnp.dot(vt_ref[0, :, pl.ds(start, n_keys)], p, preferred_element_type=F32)
        acc_ref[:, cs] = jnp.exp2(m_old - m_new) * acc_ref[:, cs] + pv
        m_ref[:, cs] = m_new

    def phase(j, cur_ref, nxt_ref):
        for a in range(B_SUBTILES):
            scores(j + 1, nxt_ref, a)
            softmax_pv(j, cur_ref, a, False)

    def diagonal(cur_ref):
        for a in range(B_SUBTILES):
            softmax_pv(i, cur_ref, a, True)
        acc = acc_ref[...]
        o_ref[...] = (acc / acc[B_V:B_V + 1, :]).T

    for a in range(B_SUBTILES):
        scores(0, s0_ref, a)

    def two_tiles(jj, carry):
        phase(2 * jj, s0_ref, s1_ref)
        phase(2 * jj + 1, s1_ref, s0_ref)
        return carry

    lax.fori_loop(0, i // 2, two_tiles, 0)

    @pl.when(i % 2 == 1)
    def _():
        phase(i - 1, s0_ref, s1_ref)
        diagonal(s1_ref)

    @pl.when(i % 2 == 0)
    def _():
        diagonal(s0_ref)


def _flash(q, k, vt, tile):
    n_h, s, _ = q.shape
    return pl.pallas_call(
        _flash_body, grid=(n_h, s // tile),
        in_specs=[pl.BlockSpec((1, tile, LANES), lambda h, i: (h, i, 0)),
                  pl.BlockSpec((1, s, LANES), lambda h, i: (h, 0, 0)),
                  pl.BlockSpec((1, LANES, s), lambda h, i: (h, 0, 0))],
        out_specs=pl.BlockSpec((tile, LANES), lambda h, i: (i, h)),
        out_shape=jax.ShapeDtypeStruct((s, n_h * LANES), F32),
        scratch_shapes=[pltpu.VMEM((1, tile), F32), pltpu.VMEM((LANES, tile), F32),
                        pltpu.VMEM((tile, tile), F32), pltpu.VMEM((tile, tile), F32)],
        compiler_params=_params("arbitrary", "arbitrary"), name="mla_flash",
    )(q, k, vt)


def _hgrn_body(x_ref, lb_ref, ng_ref, o_ref, state_ref):
    @pl.when(pl.program_id(0) == 0)
    def _():
        state_ref[...] = jnp.zeros_like(state_ref)

    c = C_CHUNK
    w = C_WIDTH
    x = x_ref[...]
    lower = lb_ref[...]
    fz = x[:, w:2 * w]
    f = lower + (1.0 - lower) * _sigmoid(fz)
    logf = jnp.log(jnp.maximum(f, C_MIN_FORGET))
    kf = (1.0 - lower) * _sigmoid(-fz)
    cq = x[:, 0:w]
    q = cq * _sigmoid(cq)
    val = x[:, 2 * w:3 * w]
    cg = x[:, 3 * w:4 * w]

    ti = lax.broadcasted_iota(jnp.int32, (c, c), 0)
    tj = lax.broadcasted_iota(jnp.int32, (c, c), 1)
    row = lax.broadcasted_iota(jnp.int32, (c, 1), 0)
    b = _sel_dot((tj <= ti).astype(F32), logf, 3)
    b_last = b[c - 1:c, :]

    n0 = C_SUB
    ref = _sel_dot((tj == (ti // n0) * n0 + n0 // 2 - 1).astype(F32), b, 3)
    q_s = q * jnp.exp(b - ref)
    k_s = kf * jnp.exp(ref - b)
    mask = ((ti // n0) == (tj // n0)) & (tj <= ti)
    attn = [jnp.where(mask, _bdot_nt(q_s[:, hd * C_DIM:(hd + 1) * C_DIM], k_s[:, hd * C_DIM:(hd + 1) * C_DIM]), 0.0)
            for hd in range(C_HEADS)]
    n = n0
    while n < c:
        ref = _sel_dot((tj == (ti // (2 * n)) * (2 * n) + n - 1).astype(F32), b, 3)
        right = ((row // n) % 2) == 1
        q_s = q * jnp.exp(jnp.where(right, b - ref, 0.0))
        k_s = kf * jnp.exp(jnp.where(right, 0.0, ref - b))
        mask = ((ti // (2 * n)) == (tj // (2 * n))) & (((ti // n) % 2) == 1) & (((tj // n) % 2) == 0)
        attn = [jnp.where(mask, _bdot_nt(q_s[:, hd * C_DIM:(hd + 1) * C_DIM], k_s[:, hd * C_DIM:(hd + 1) * C_DIM]),
                          attn[hd]) for hd in range(C_HEADS)]
        n *= 2

    q_in = q * jnp.exp(b)
    k_end = kf * jnp.exp(b_last - b)
    g_end = jnp.exp(b_last)
    outs = []
    for hd in range(C_HEADS):
        sl = slice(hd * C_DIM, (hd + 1) * C_DIM)
        st = state_ref[hd]
        vh = val[:, sl]
        o = _bdot(attn[hd], vh) + _bdot_nt(q_in[:, sl], st)
        state_ref[hd] = st * g_end[:, sl] + _bdot(vh.T, k_end[:, sl])
        g = cg[:, sl]
        outs.append(_rms(o, ng_ref[...]) * (g * _sigmoid(g)))
    o_ref[...] = jnp.concatenate(outs, axis=1)


def _hgrn(x, lower, norm_g):
    s = x.shape[0]
    c = C_CHUNK
    return pl.pallas_call(
        _hgrn_body, grid=(s // c,),
        in_specs=[pl.BlockSpec((c, C_IN), lambda i: (i, 0)), pl.BlockSpec((1, C_WIDTH), lambda i: (0, 0)),
                  pl.BlockSpec((1, C_DIM), lambda i: (0, 0))],
        out_specs=pl.BlockSpec((c, C_WIDTH), lambda i: (i, 0)),
        out_shape=jax.ShapeDtypeStruct((s, C_WIDTH), F32),
        scratch_shapes=[pltpu.VMEM((C_HEADS, C_DIM, C_DIM), F32)],
        compiler_params=_params("arbitrary"), name="hgrn2",
    )(x, lower, norm_g)


def _merge_body(h_ref, ya_ref, yb_ref, yc_ref, g_ref, wg_ref, wa_ref, wb_ref, wc_ref, wo_ref, pg_ref, o_ref):
    d = D_MODEL
    h = h_ref[...]
    u = _rms(h, g_ref[...]).astype(BF16)
    merged = None
    for n, (y_ref, w_ref) in enumerate(((ya_ref, wa_ref), (yb_ref, wb_ref), (yc_ref, wc_ref))):
        gate = _sigmoid(jnp.dot(u, wg_ref[:, n * d:(n + 1) * d], preferred_element_type=F32))
        term = gate * _bdot(y_ref[...], w_ref[...])
        merged = term if merged is None else merged + term
    o_ref[...] = h + _rms(_bdot(merged, wo_ref[...]), pg_ref[...])


def _merge(h, ya, yb, yc, pre_g, wg, wa, wb, wc, wo, post_g):
    s = h.shape[0]
    tm = ROW_TILE
    row = lambda n: pl.BlockSpec((tm, n), lambda i: (i, 0))
    const = lambda a: pl.BlockSpec(a.shape, lambda i: (0,) * a.ndim)
    consts = [pre_g, wg, wa, wb, wc, wo, post_g]
    return pl.pallas_call(
        _merge_body, grid=(s // tm,),
        in_specs=[row(D_MODEL), row(A_WIDTH), row(B_HEADS * LANES), row(C_WIDTH)] + [const(a) for a in consts],
        out_specs=row(D_MODEL), out_shape=jax.ShapeDtypeStruct((s, D_MODEL), F32),
        compiler_params=_params("arbitrary"), name="merge",
    )(h, ya, yb, yc, *consts)


def _pad_cols(a, n):
    return jnp.pad(a, ((0, 0), (0, n - a.shape[1])))


def _pad_rows(a, n):
    return jnp.pad(a, ((0, n - a.shape[0]), (0, 0)))


def _rope_half_swap(a):
    half = B_ROPE // 2
    return jnp.concatenate([a[:, half:], a[:, :half]], axis=1)


def _layer_weights(l, w_in, rwkv_mu, vres_down, vres_mu, w_uq, w_ukv, mla_out):
    wi = w_in[l]
    d = D_MODEL
    o = 3 * A_WIDTH
    lora = [wi[:, o:o + 64], wi[:, o + 64:o + 128], wi[:, o + 128:o + 256]]
    mus = [rwkv_mu[l][o:o + 64], rwkv_mu[l][o + 64:o + 128], rwkv_mu[l][o + 128:o + 256]]
    if l > 0:
        lora.append(vres_down[l - 1])
        mus.append(vres_mu[l - 1])
    else:
        lora.append(jnp.zeros((d, 0), F32))
        mus.append(jnp.zeros((0,), F32))
    w_a = jnp.concatenate([wi[:, :o]] + [_pad_cols(t, LANES) for t in lora], axis=1)
    mu_a = jnp.concatenate([rwkv_mu[l][:o]] + [jnp.pad(t, (0, LANES - t.shape[0])) for t in mus])[None, :]

    o = A_COLS
    w_kr = wi[:, o + B_Q_RANK + B_KV_RANK:o + B_Q_RANK + B_KV_RANK + B_ROPE]
    place = lambda t: jnp.pad(t, ((0, 0), (B_NOPE, LANES - B_NOPE - B_ROPE)))
    w_b = jnp.concatenate([wi[:, o:o + B_Q_RANK + B_KV_RANK], place(w_kr), place(_rope_half_swap(w_kr))], axis=1)
    o += B_Q_RANK + B_KV_RANK + B_ROPE
    w_c = wi[:, o:o + C_IN]
    w_g = wi[:, o + C_IN:o + C_IN + N_BRANCH * d]

    scale = (B_NOPE + B_ROPE) ** -0.5 * np.log2(np.e)
    uq = (w_uq[l] * scale).reshape(B_Q_RANK, B_HEADS, B_NOPE + B_ROPE)
    uq_sw = jnp.concatenate([jnp.zeros_like(uq[..., :B_NOPE]), uq[..., B_NOPE + B_ROPE // 2:],
                             uq[..., B_NOPE:B_NOPE + B_ROPE // 2]], axis=-1)
    pad_head = lambda t: jnp.pad(t, ((0, 0), (0, 0), (0, LANES - t.shape[-1]))).reshape(t.shape[0], B_HEADS * LANES)
    ukv = w_ukv[l].reshape(B_KV_RANK, B_HEADS, B_NOPE + B_V)
    mo = jnp.pad(mla_out[l].reshape(B_HEADS, B_V, d), ((0, 0), (0, LANES - B_V), (0, 0))).reshape(B_HEADS * LANES, d)
    bf = lambda t: t.astype(BF16)
    return dict(w_a=bf(w_a), mu_a=mu_a, w_b=bf(w_b), w_c=bf(w_c), w_g=bf(w_g),
                w_q=bf(pad_head(uq)), w_q_swap=bf(pad_head(uq_sw)),
                w_k=bf(pad_head(ukv[..., :B_NOPE])), w_v=bf(pad_head(ukv[..., B_NOPE:])), mla_out=bf(mo))


def _rope_tables(positions):
    inv_freq = ROPE_THETA ** (-jnp.arange(0, B_ROPE, 2, dtype=F32) / B_ROPE)
    ang = positions.astype(F32)[:, None] * inv_freq
    cos, sin = jnp.cos(ang), jnp.sin(ang)
    s = positions.shape[0]
    pad = jnp.zeros((s, LANES - B_NOPE - B_ROPE), F32)
    cos_q = jnp.concatenate([jnp.ones((s, B_NOPE), F32), cos, cos, pad], axis=1)
    sin_t = jnp.concatenate([jnp.zeros((s, B_NOPE), F32), -sin, sin, pad], axis=1)
    cos_k = jnp.concatenate([jnp.zeros((s, B_NOPE), F32), cos, cos, pad], axis=1)
    return cos_q, sin_t, cos_k, sin_t


def kernel(x, positions, ffn1_pre_g, ffn1_post_g, ffn1_w_gate, ffn1_w_up, ffn1_w_down, mix_pre_g, mix_post_g, w_in, rwkv_mu, rwkv_w0, rwkv_w_up, rwkv_a0, rwkv_a_up, rwkv_g_up, rwkv_k_k, rwkv_k_a, rwkv_r_k, rwkv_gn_g, rwkv_gn_b, rwkv_vres_down, rwkv_vres_mu, rwkv_vres_up, rwkv_v0, rwkv_out, mla_q_norm_g, mla_w_uq, mla_kv_norm_g, mla_w_ukv, mla_out, hgrn_lower_bounds, hgrn_norm_g, hgrn_out, w_o, ffn2_pre_g, ffn2_post_g, ffn2_w_gate, ffn2_w_up, ffn2_w_down):
    bsz, seq, d = x.shape
    assert bsz == 1 and d == D_MODEL and seq % max(ROW_TILE, C_CHUNK, A_CHUNK) == 0
    depth = w_in.shape[0]
    flash_tile = min(1024, seq)
    tabs = _rope_tables(positions[0])
    lb_p = jax.nn.softmax(hgrn_lower_bounds.astype(F32), axis=0)
    lower_bounds = jnp.cumsum(lb_p, axis=0) - lb_p[0]
    row = lambda t: t[None, :]
    bf = lambda t: t.astype(BF16)

    h = x[0]
    v_first = None
    for l in range(depth):
        h = _ffn(h, row(ffn1_pre_g[l]), bf(ffn1_w_gate[l]), bf(ffn1_w_up[l]), bf(ffn1_w_down[l]),
                 row(ffn1_post_g[l]))

        lw = _layer_weights(l, w_in, rwkv_mu, rwkv_vres_down, rwkv_vres_mu, mla_w_uq, mla_w_ukv, mla_out)
        pre_g = row(mix_pre_g[l])
        x_a = _proj(h, pre_g, lw["w_a"], lw["mu_a"])
        x_b = _proj(h, pre_g, lw["w_b"])
        x_c = _proj(h, pre_g, lw["w_c"])

        pa = dict(w0=row(rwkv_w0[l]), w_up=bf(_pad_rows(rwkv_w_up[l], LANES)), a0=row(rwkv_a0[l]),
                  a_up=bf(_pad_rows(rwkv_a_up[l], LANES)), g_up=bf(rwkv_g_up[l]), k_k=row(rwkv_k_k[l]),
                  k_a=row(rwkv_k_a[l]), r_k=row(rwkv_r_k[l].reshape(-1)), gn_g=row(rwkv_gn_g[l]),
                  gn_b=row(rwkv_gn_b[l]))
        if l > 0:
            pa.update(v0=row(rwkv_v0[l - 1]), vres_up=bf(_pad_rows(rwkv_vres_up[l - 1], LANES)))
        y_a, v_first = _rwkv(x_a, pa, v_first)

        pb = dict(q_norm_g=row(mla_q_norm_g[l]), kv_norm_g=row(mla_kv_norm_g[l]), w_q=lw["w_q"],
                  w_q_swap=lw["w_q_swap"], w_k=lw["w_k"], w_v=lw["w_v"])
        q_h, k_h, vt_h = _mla_prep(x_b, tabs, pb)
        y_b = _flash(q_h, k_h, vt_h, flash_tile)

        y_c = _hgrn(x_c, row(lower_bounds[l]), row(hgrn_norm_g[l]))

        h = _merge(h, y_a, y_b, y_c, pre_g, lw["w_g"], bf(rwkv_out[l]), lw["mla_out"], bf(hgrn_out[l]),
                   bf(w_o[l]), row(mix_post_g[l]))

        h = _ffn(h, row(ffn2_pre_g[l]), bf(ffn2_w_gate[l]), bf(ffn2_w_up[l]), bf(ffn2_w_down[l]),
                 row(ffn2_post_g[l]))
    return h[None]
```

```python
import functools

import jax
import jax.numpy as jnp
import numpy as np
from jax import lax
from jax.experimental import pallas as pl
from jax.experimental.pallas import tpu as pltpu

F32 = jnp.float32
BF16 = jnp.bfloat16

D_MODEL = 1024
D_FF = 2816
NORM_EPS = 1e-6
MACARON_WEIGHT = 0.5

A_HEADS = 8
A_HEAD_DIM = 64
A_WIDTH = 512
A_GN_EPS = 64e-5
A_CHUNK = 64
A_CHUNKS_PER_STEP = 4
A_COLS = 1792
A_IN = 2048

B_HEADS = 8
B_NOPE = 64
B_ROPE = 32
B_V = 64
B_Q_RANK = 384
B_KV_RANK = 256
B_IN = 896
ROPE_THETA = 10000.0
B_SUBTILES = 4
LANES = 128

C_HEADS = 4
C_DIM = 128
C_WIDTH = 512
C_IN = 2048
C_CHUNK = 128
C_CHUNKS_PER_STEP = 2
C_SUB = 8
C_MIN_FORGET = 1e-6

N_BRANCH = 3
ROW_TILE = 512
FFN_ROW_TILE = 512
VMEM_LIMIT = 56 * 1024 * 1024

def _bdot(a, b):
    return jnp.dot(a.astype(BF16), b.astype(BF16), preferred_element_type=F32)


def _bdot_nt(a, b):
    return lax.dot_general(a.astype(BF16), b.astype(BF16), (((1,), (1,)), ((), ())),
                           preferred_element_type=F32)


def _bmm(a, b):
    return lax.dot_general(a.astype(BF16), b.astype(BF16), (((2,), (1,)), ((0,), (0,))),
                           preferred_element_type=F32)


def _bmm_nt(a, b):
    return lax.dot_general(a.astype(BF16), b.astype(BF16), (((2,), (2,)), ((0,), (0,))),
                           preferred_element_type=F32)


def _split_terms(data, parts):
    terms, rem = [], data
    for _ in range(parts):
        piece = rem.astype(BF16)
        terms.append(piece)
        rem = rem - piece.astype(F32)
    return terms


def _sel_dot(sel, data, parts):
    sel = sel.astype(BF16)
    return sum(jnp.dot(sel, t, preferred_element_type=F32) for t in _split_terms(data, parts))


def _dot_sel(data, sel, parts):
    sel = sel.astype(BF16)
    return sum(jnp.dot(t, sel, preferred_element_type=F32) for t in _split_terms(data, parts))


def _rms(x, g):
    return x * lax.rsqrt(jnp.mean(x * x, axis=-1, keepdims=True) + NORM_EPS) * g


def _sigmoid(x):
    return jax.nn.sigmoid(x)


def _params(*sem):
    return pltpu.CompilerParams(dimension_semantics=sem, vmem_limit_bytes=VMEM_LIMIT)


def _ffn_body(x_ref, pre_g_ref, wg_ref, wu_ref, wd_ref, post_g_ref, o_ref):
    x = x_ref[...]
    xn = _rms(x, pre_g_ref[...]).astype(BF16)
    gate = jnp.dot(xn, wg_ref[...], preferred_element_type=F32)
    up = jnp.dot(xn, wu_ref[...], preferred_element_type=F32)
    mid = (gate * _sigmoid(gate) * up).astype(BF16)
    y = jnp.dot(mid, wd_ref[...], preferred_element_type=F32)
    o_ref[...] = x + MACARON_WEIGHT * _rms(y, post_g_ref[...])


def _ffn(h, pre_g, wg, wu, wd, post_g):
    s = h.shape[0]
    tm = FFN_ROW_TILE
    const = lambda a: pl.BlockSpec(a.shape, lambda i: (0,) * a.ndim)
    return pl.pallas_call(
        _ffn_body,
        grid=(s // tm,),
        in_specs=[pl.BlockSpec((tm, D_MODEL), lambda i: (i, 0)), const(pre_g), const(wg), const(wu), const(wd),
                  const(post_g)],
        out_specs=pl.BlockSpec((tm, D_MODEL), lambda i: (i, 0)),
        out_shape=jax.ShapeDtypeStruct((s, D_MODEL), F32),
        compiler_params=_params("arbitrary"),
        name="ffn",
    )(h, pre_g, wg, wu, wd, post_g)


def _proj_body(x_ref, g_ref, w_ref, o_ref):
    xn = _rms(x_ref[...], g_ref[...]).astype(BF16)
    o_ref[...] = jnp.dot(xn, w_ref[...], preferred_element_type=F32)


def _proj_shift_body(x_ref, g_ref, w_ref, mu_ref, o_ref, carry_ref):
    @pl.when(pl.program_id(0) == 0)
    def _():
        carry_ref[...] = jnp.zeros_like(carry_ref)

    xn = _rms(x_ref[...], g_ref[...]).astype(BF16)
    p = jnp.dot(xn, w_ref[...], preferred_element_type=F32)
    rows = p.shape[0]
    prev = pltpu.roll(p, 1, axis=0)
    first = lax.broadcasted_iota(jnp.int32, p.shape, 0) == 0
    prev = jnp.where(first, carry_ref[0:1, :], prev)
    carry_ref[0:1, :] = p[rows - 1:rows, :]
    o_ref[...] = p + (prev - p) * mu_ref[...]


def _proj(h, pre_g, w, mu=None):
    s = h.shape[0]
    n = w.shape[1]
    x_spec = pl.BlockSpec((ROW_TILE, D_MODEL), lambda i: (i, 0))
    g_spec = pl.BlockSpec((1, D_MODEL), lambda i: (0, 0))
    w_spec = pl.BlockSpec((D_MODEL, n), lambda i: (0, 0))
    o_spec = pl.BlockSpec((ROW_TILE, n), lambda i: (i, 0))
    out_shape = jax.ShapeDtypeStruct((s, n), F32)
    if mu is None:
        return pl.pallas_call(
            _proj_body, grid=(s // ROW_TILE,), in_specs=[x_spec, g_spec, w_spec], out_specs=o_spec,
            out_shape=out_shape, compiler_params=_params("arbitrary"), name="proj",
        )(h, pre_g, w)
    return pl.pallas_call(
        _proj_shift_body, grid=(s // ROW_TILE,),
        in_specs=[x_spec, g_spec, w_spec, pl.BlockSpec((1, n), lambda i: (0, 0))], out_specs=o_spec,
        out_shape=out_shape, scratch_shapes=[pltpu.VMEM((8, n), F32)],
        compiler_params=_params("arbitrary"), name="proj_shift",
    )(h, pre_g, w, mu)


def _tri_inverse(a_strict, eye, blk16, lvl1, lvl2):
    d = a_strict * blk16
    d2 = _bmm(d, d)
    d4 = _bmm(d2, d2)
    d8 = _bmm(d4, d4)
    t = eye + d
    t = t + _bmm(t, d2)
    t = t + _bmm(t, d4)
    t = t + _bmm(t, d8)
    t = t + _bmm(_bmm(t, a_strict * lvl1), t)
    t = t + _bmm(_bmm(t, a_strict * lvl2), t)
    return t


def _rwkv_body(has_vres, *refs):
    if has_vres:
        (x_ref, vf_ref, w0_ref, wup_ref, a0_ref, aup_ref, gup_ref, kk_ref, ka_ref, rk_ref, gng_ref, gnb_ref,
         v0_ref, vup_ref, y_ref, state_ref) = refs
    else:
        (x_ref, w0_ref, wup_ref, a0_ref, aup_ref, gup_ref, kk_ref, ka_ref, rk_ref, gng_ref, gnb_ref,
         y_ref, vf_out_ref, state_ref) = refs

    @pl.when(pl.program_id(0) == 0)
    def _():
        state_ref[...] = jnp.zeros_like(state_ref)

    c = A_CHUNK
    w = A_WIDTH
    hd = A_HEAD_DIM
    x = x_ref[...]
    rows = x.shape[0]
    r = x[:, 0:w]
    k = x[:, w:2 * w]
    v = x[:, 2 * w:3 * w]
    wl = x[:, 3 * w:3 * w + LANES]
    al = x[:, 3 * w + LANES:3 * w + 2 * LANES]
    gl = x[:, 3 * w + 2 * LANES:3 * w + 3 * LANES]

    z = w0_ref[...] + _bdot(jnp.tanh(wl), wup_ref[...])
    softplus_neg = jnp.maximum(-z, 0.0) + jnp.log(1.0 + jnp.exp(-jnp.abs(z)))
    logw = -jnp.exp(-softplus_neg - 0.5)
    a = _sigmoid(a0_ref[...] + _bdot(al, aup_ref[...]))
    g = _bdot(_sigmoid(gl), gup_ref[...])
    if has_vres:
        vl = x[:, 3 * w + 3 * LANES:3 * w + 4 * LANES]
        v = v + (vf_ref[...] - v) * _sigmoid(v0_ref[...] + _bdot(vl, vup_ref[...]))
    else:
        vf_out_ref[...] = v

    li = lax.broadcasted_iota(jnp.int32, (LANES, LANES), 0)
    lj = lax.broadcasted_iota(jnp.int32, (LANES, LANES), 1)
    pair_bd = ((li < hd) == (lj < hd)).astype(F32)
    lane = lax.broadcasted_iota(jnp.int32, (1, LANES), 1)
    m0 = (lane < hd).astype(F32)
    m1 = (lane >= hd).astype(F32)

    def head_sum(t):
        return jnp.concatenate([_dot_sel(t[:, p * LANES:(p + 1) * LANES], pair_bd, 2) for p in range(w // LANES)],
                               axis=1)

    def by_head(t):
        return jnp.concatenate([t * m0, t * m1], axis=0)

    kkr = k * kk_ref[...]
    kk = kkr / jnp.maximum(jnp.sqrt(head_sum(kkr * kkr)), 1e-12)
    k = k * (1.0 + (a - 1.0) * ka_ref[...])

    ti = lax.broadcasted_iota(jnp.int32, (2 * c, 2 * c), 0)
    tj = lax.broadcasted_iota(jnp.int32, (2 * c, 2 * c), 1)
    same_head = (ti // c) == (tj // c)
    incl = (same_head & (tj <= ti)).astype(F32)
    strict = (same_head & (tj < ti)).astype(F32)
    eye = (tj == ti).astype(F32)
    blk16 = ((ti // 16) == (tj // 16)).astype(F32)
    lvl1 = (((ti // 32) == (tj // 32)) & ((ti // 16) != (tj // 16))).astype(F32)
    lvl2 = (same_head & ((ti // 32) != (tj // 32))).astype(F32)

    ri = lax.broadcasted_iota(jnp.int32, (rows, rows), 0)
    rj = lax.broadcasted_iota(jnp.int32, (rows, rows), 1)
    chunk_incl = (((ri // c) == (rj // c)) & (rj <= ri)).astype(F32)
    lb = _sel_dot(chunk_incl, logw, 3)
    e_neg = jnp.exp(-lb)
    alpha_t = -kk * jnp.exp(lb - logw)
    beta = kk * a
    beta_h = beta * e_neg
    k_h = k * e_neg
    r_t = r * jnp.exp(lb)

    n_ch = rows // c
    n_pair = w // LANES

    def slabs(t):
        return [t[ci * c:(ci + 1) * c, p * LANES:(p + 1) * LANES] for ci in range(n_ch) for p in range(n_pair)]

    def stack_by_head(t):
        return jnp.stack([by_head(s) for s in slabs(t)])

    xa2, v2 = stack_by_head(alpha_t), stack_by_head(v)
    gram = _bmm_nt(jnp.concatenate([xa2, stack_by_head(r_t)], axis=1),
                   jnp.concatenate([stack_by_head(beta_h), stack_by_head(k_h)], axis=1))
    a_ab = gram[:, 0:2 * c, 0:2 * c] * strict
    a_ak = gram[:, 0:2 * c, 2 * c:4 * c] * strict
    a_r = jnp.concatenate([gram[:, 2 * c:4 * c, 2 * c:4 * c] * incl, gram[:, 2 * c:4 * c, 0:2 * c] * incl], axis=2)
    t = _tri_inverse(a_ab, eye, blk16, lvl1, lvl2)
    wu = _bmm(t, jnp.concatenate([xa2, _bmm(a_ak, v2)], axis=2))
    wt = wu[:, 0:c, 0:LANES] + wu[:, c:2 * c, 0:LANES]
    ut = wu[:, 0:c, LANES:2 * LANES] + wu[:, c:2 * c, LANES:2 * LANES]
    xr = jnp.stack(slabs(r_t))
    vp = jnp.stack(slabs(v))

    st = state_ref[...]
    y_chunks = []
    for ci in range(n_ch):
        bs = slice(ci * n_pair, (ci + 1) * n_pair)
        lb_last = lb[(ci + 1) * c - 1:(ci + 1) * c, :]
        e_end = jnp.exp(lb_last - lb[ci * c:(ci + 1) * c])
        k_e = k[ci * c:(ci + 1) * c] * e_end
        beta_e = beta[ci * c:(ci + 1) * c] * e_end
        gamma_c = jnp.exp(lb_last)
        u = _bmm_nt(wt[bs], st) + ut[bs]
        u2 = jnp.concatenate([u * m0, u * m1], axis=1)
        y2 = _bmm(a_r[bs], jnp.concatenate([v2[bs], u2], axis=1))
        y = _bmm_nt(xr[bs], st) + y2[:, 0:c] + y2[:, c:2 * c]
        y_chunks.append(jnp.concatenate([y[p] for p in range(n_pair)], axis=1))
        vu_t = jnp.stack([jnp.concatenate([vp[ci * n_pair + p], u[p]], axis=0).T for p in range(n_pair)])
        ke_be = jnp.stack([jnp.concatenate([k_e[:, p * LANES:(p + 1) * LANES], beta_e[:, p * LANES:(p + 1) * LANES]],
                                           axis=0) for p in range(n_pair)])
        decay = jnp.stack([gamma_c[:, p * LANES:(p + 1) * LANES] for p in range(n_pair)])
        st = st * decay + pair_bd * _bmm(vu_t, ke_be)
    state_ref[...] = st
    y = jnp.concatenate(y_chunks, axis=0)

    inv_n = 1.0 / hd
    mean = head_sum(y) * inv_n
    yc = y - mean
    var = head_sum(yc * yc) * inv_n
    yn = yc * lax.rsqrt(var + A_GN_EPS) * gng_ref[...] + gnb_ref[...]
    bonus = head_sum(r * k * rk_ref[...]) * v
    y_ref[...] = (yn + bonus) * g


def _rwkv(x, p, v_first):
    s = x.shape[0]
    rows = A_CHUNK * A_CHUNKS_PER_STEP
    has_vres = v_first is not None
    row = lambda n: pl.BlockSpec((rows, n), lambda i: (i, 0))
    const = lambda a: pl.BlockSpec(a.shape, lambda i: (0,) * a.ndim)
    names = ["w0", "w_up", "a0", "a_up", "g_up", "k_k", "k_a", "r_k", "gn_g", "gn_b"]
    if has_vres:
        names += ["v0", "vres_up"]
    consts = [p[n] for n in names]
    ins = [x] + ([v_first] if has_vres else []) + consts
    in_specs = [row(A_IN)] + ([row(A_WIDTH)] if has_vres else []) + [const(a) for a in consts]
    y_shape = jax.ShapeDtypeStruct((s, A_WIDTH), F32)
    out = pl.pallas_call(
        functools.partial(_rwkv_body, has_vres),
        grid=(s // rows,), in_specs=in_specs,
        out_specs=row(A_WIDTH) if has_vres else [row(A_WIDTH), row(A_WIDTH)],
        out_shape=y_shape if has_vres else [y_shape, y_shape],
        scratch_shapes=[pltpu.VMEM((A_WIDTH // LANES, LANES, LANES), F32)],
        compiler_params=_params("arbitrary"), name="rwkv7",
    )(*ins)
    return (out, v_first) if has_vres else (out[0], out[1])


def _mla_prep_body(x_ref, cq_ref, sq_ref, ck_ref, sk_ref, qg_ref, wq_ref, wqs_ref, kg_ref, wk_ref, wv_ref,
                   q_ref, k_ref, vt_ref):
    x = x_ref[...]
    cq = x[:, 0:B_Q_RANK]
    ckv = x[:, B_Q_RANK:B_Q_RANK + B_KV_RANK]
    kr = x[:, B_Q_RANK + B_KV_RANK:B_Q_RANK + B_KV_RANK + LANES]
    kr_sw = x[:, B_Q_RANK + B_KV_RANK + LANES:B_Q_RANK + B_KV_RANK + 2 * LANES]
    cqn = _rms(cq, qg_ref[...]).astype(BF16)
    ckn = _rms(ckv, kg_ref[...]).astype(BF16)
    q = jnp.dot(cqn, wq_ref[...], preferred_element_type=F32)
    q_sw = jnp.dot(cqn, wqs_ref[...], preferred_element_type=F32)
    k_nope = jnp.dot(ckn, wk_ref[...], preferred_element_type=F32)
    val = jnp.dot(ckn, wv_ref[...], preferred_element_type=F32)
    k_rope = kr * ck_ref[...] + kr_sw * sk_ref[...]
    cq_t, sq_t = cq_ref[...], sq_ref[...]
    ones_col = (lax.broadcasted_iota(jnp.int32, (1, LANES), 1) == B_V).astype(F32)
    for hd in range(B_HEADS):
        sl = slice(hd * LANES, (hd + 1) * LANES)
        q_ref[hd] = (q[:, sl] * cq_t + q_sw[:, sl] * sq_t).astype(BF16)
        k_ref[hd] = (k_nope[:, sl] + k_rope).astype(BF16)
        vt_ref[hd] = (val[:, sl] + ones_col).T.astype(BF16)


def _mla_prep(x, tabs, p):
    s = x.shape[0]
    tm = ROW_TILE
    row = lambda n: pl.BlockSpec((tm, n), lambda i: (i, 0))
    const = lambda a: pl.BlockSpec(a.shape, lambda i: (0,) * a.ndim)
    consts = [p["q_norm_g"], p["w_q"], p["w_q_swap"], p["kv_norm_g"], p["w_k"], p["w_v"]]
    head_spec = pl.BlockSpec((B_HEADS, tm, LANES), lambda i: (0, i, 0))
    head_shape = jax.ShapeDtypeStruct((B_HEADS, s, LANES), BF16)
    head_t_spec = pl.BlockSpec((B_HEADS, LANES, tm), lambda i: (0, 0, i))
    head_t_shape = jax.ShapeDtypeStruct((B_HEADS, LANES, s), BF16)
    return pl.pallas_call(
        _mla_prep_body, grid=(s // tm,),
        in_specs=[row(B_IN)] + [row(LANES)] * 4 + [const(a) for a in consts],
        out_specs=[head_spec, head_spec, head_t_spec], out_shape=[head_shape, head_shape, head_t_shape],
        compiler_params=_params("arbitrary"), name="mla_prep",
    )(x, *tabs, *consts)


def _flash_body(q_ref, qn_ref, k_ref, vt_ref, o_ref, m_ref, acc_ref, s0_ref, s1_ref, s2_ref):
    i = pl.program_id(1)
    tile = q_ref.shape[1]
    sub = tile // B_SUBTILES
    m_ref[...] = jnp.full_like(m_ref, -jnp.inf)
    acc_ref[...] = jnp.zeros_like(acc_ref)

    def scores(queries_ref, j, s_ref, a):
        start = pl.multiple_of(j * tile, tile)
        s_ref[:, a * sub:(a + 1) * sub] = lax.dot_general(
            k_ref[0, pl.ds(start, tile), :], queries_ref[0, a * sub:(a + 1) * sub, :], (((1,), (1,)), ((), ())),
            preferred_element_type=F32)

    def softmax_pv(j, s_ref, a, on_diagonal):
        start = pl.multiple_of(j * tile, tile)
        cs = slice(a * sub, (a + 1) * sub)
        n_keys = (a + 1) * sub if on_diagonal else tile
        s = s_ref[0:n_keys, cs]
        if on_diagonal:
            keys = lax.broadcasted_iota(jnp.int32, s.shape, 0)
            queries = lax.broadcasted_iota(jnp.int32, s.shape, 1) + a * sub
            s = jnp.where(keys <= queries, s, -jnp.inf)
        m_old = m_ref[:, cs]
        m_new = jnp.maximum(m_old, jnp.max(s, axis=0, keepdims=True))
        p = jnp.exp2(s - m_new).astype(BF16)
        pv = jnp.dot(vt_ref[0, :, pl.ds(start, n_keys)], p, preferred_element_type=F32)
        acc_ref[:, cs] = jnp.exp2(m_old - m_new) * acc_ref[:, cs] + pv
        m_ref[:, cs] = m_new

    def phase(j, cur_ref, nxt_ref):
        for a in range(B_SUBTILES):
            scores(q_ref, j + 1, nxt_ref, a)
            softmax_pv(j, cur_ref, a, False)

    def diagonal(cur_ref, overlap_next):
        for a in range(B_SUBTILES):
            if overlap_next:
                scores(qn_ref, 0, s2_ref, a)
            softmax_pv(i, cur_ref, a, True)
        acc = acc_ref[...]
        o_ref[...] = (acc / acc[B_V:B_V + 1, :]).T

    @pl.when(i == 0)
    def _():
        for a in range(B_SUBTILES):
            scores(q_ref, 0, s2_ref, a)
        diagonal(s2_ref, False)
        for a in range(B_SUBTILES):
            scores(qn_ref, 0, s2_ref, a)

    @pl.when(i > 0)
    def _():
        phase(0, s2_ref, s0_ref)

        def two_tiles(jj, carry):
            phase(2 * jj + 1, s0_ref, s1_ref)
            phase(2 * jj + 2, s1_ref, s0_ref)
            return carry

        lax.fori_loop(0, (i - 1) // 2, two_tiles, 0)

        @pl.when(i % 2 == 0)
        def _():
            phase(i - 1, s0_ref, s1_ref)
            diagonal(s1_ref, True)

        @pl.when(i % 2 == 1)
        def _():
            diagonal(s0_ref, True)


def _flash(q, k, vt, tile):
    n_h, s, _ = q.shape
    n_q = s // tile
    scores = pltpu.VMEM((tile, tile), F32)
    return pl.pallas_call(
        _flash_body, grid=(n_h, n_q),
        in_specs=[pl.BlockSpec((1, tile, LANES), lambda h, i: (h, i, 0)),
                  pl.BlockSpec((1, tile, LANES), lambda h, i: (h, jnp.minimum(i + 1, n_q - 1), 0)),
                  pl.BlockSpec((1, s, LANES), lambda h, i: (h, 0, 0)),
                  pl.BlockSpec((1, LANES, s), lambda h, i: (h, 0, 0))],
        out_specs=pl.BlockSpec((tile, LANES), lambda h, i: (i, h)),
        out_shape=jax.ShapeDtypeStruct((s, n_h * LANES), F32),
        scratch_shapes=[pltpu.VMEM((1, tile), F32), pltpu.VMEM((LANES, tile), F32), scores, scores, scores],
        compiler_params=_params("arbitrary", "arbitrary"), name="mla_flash",
    )(q, q, k, vt)


def _hgrn_body(x_ref, lb_ref, ng_ref, o_ref, state_ref):
    @pl.when(pl.program_id(0) == 0)
    def _():
        state_ref[...] = jnp.zeros_like(state_ref)

    c = C_CHUNK
    w = C_WIDTH
    x = x_ref[...]
    rows = x.shape[0]
    n_ch = rows // c
    lower = lb_ref[...]
    fz = x[:, w:2 * w]
    f = lower + (1.0 - lower) * _sigmoid(fz)
    logf = jnp.log(jnp.maximum(f, C_MIN_FORGET))
    kf = (1.0 - lower) * _sigmoid(-fz)
    cq = x[:, 0:w]
    q = cq * _sigmoid(cq)
    val = x[:, 2 * w:3 * w]
    cg = x[:, 3 * w:4 * w]

    def heads(t):
        return jnp.stack([t[ci * c:(ci + 1) * c, hd * C_DIM:(hd + 1) * C_DIM]
                          for ci in range(n_ch) for hd in range(C_HEADS)])

    ri = lax.broadcasted_iota(jnp.int32, (rows, rows), 0)
    rj = lax.broadcasted_iota(jnp.int32, (rows, rows), 1)
    row = lax.broadcasted_iota(jnp.int32, (rows, 1), 0)
    ti = lax.broadcasted_iota(jnp.int32, (c, c), 0)
    tj = lax.broadcasted_iota(jnp.int32, (c, c), 1)
    b = _sel_dot((((ri // c) == (rj // c)) & (rj <= ri)).astype(F32), logf, 3)

    def block_row(size, offset):
        return jnp.concatenate([jnp.broadcast_to(b[r + offset:r + offset + 1, :], (size, w))
                                for r in range(0, rows, size)], axis=0)

    b_last = block_row(c, c - 1)

    n0 = C_SUB
    ref = block_row(n0, n0 // 2 - 1)
    mask = ((ti // n0) == (tj // n0)) & (tj <= ti)
    attn = jnp.where(mask, _bmm_nt(heads(q * jnp.exp(b - ref)), heads(kf * jnp.exp(ref - b))), 0.0)
    n = n0
    while n < c:
        ref = block_row(2 * n, n - 1)
        right = ((row // n) % 2) == 1
        q_s = q * jnp.exp(jnp.where(right, b - ref, 0.0))
        k_s = kf * jnp.exp(jnp.where(right, 0.0, ref - b))
        mask = ((ti // (2 * n)) == (tj // (2 * n))) & (((ti // n) % 2) == 1) & (((tj // n) % 2) == 0)
        attn = jnp.where(mask, _bmm_nt(heads(q_s), heads(k_s)), attn)
        n *= 2

    vals = heads(val)
    intra = _bmm(attn, vals)
    q_in = heads(q * jnp.exp(b))
    k_end = heads(kf * jnp.exp(b_last - b))
    g_end = jnp.exp(b_last)
    gates = cg * _sigmoid(cg)
    st = state_ref[...]
    out_chunks = []
    for ci in range(n_ch):
        bs = slice(ci * C_HEADS, (ci + 1) * C_HEADS)
        o = intra[bs] + _bmm_nt(q_in[bs], st)
        vt = jnp.stack([vals[ci * C_HEADS + hd].T for hd in range(C_HEADS)])
        decay = jnp.stack([g_end[(ci + 1) * c - 1:(ci + 1) * c, hd * C_DIM:(hd + 1) * C_DIM] for hd in range(C_HEADS)])
        st = st * decay + _bmm(vt, k_end[bs])
        out_chunks.append(jnp.concatenate([_rms(o[hd], ng_ref[...]) for hd in range(C_HEADS)], axis=1))
    state_ref[...] = st
    o_ref[...] = jnp.concatenate(out_chunks, axis=0) * gates


def _hgrn(x, lower, norm_g):
    s = x.shape[0]
    rows = C_CHUNK * C_CHUNKS_PER_STEP
    return pl.pallas_call(
        _hgrn_body, grid=(s // rows,),
        in_specs=[pl.BlockSpec((rows, C_IN), lambda i: (i, 0)), pl.BlockSpec((1, C_WIDTH), lambda i: (0, 0)),
                  pl.BlockSpec((1, C_DIM), lambda i: (0, 0))],
        out_specs=pl.BlockSpec((rows, C_WIDTH), lambda i: (i, 0)),
        out_shape=jax.ShapeDtypeStruct((s, C_WIDTH), F32),
        scratch_shapes=[pltpu.VMEM((C_HEADS, C_DIM, C_DIM), F32)],
        compiler_params=_params("arbitrary"), name="hgrn2",
    )(x, lower, norm_g)


def _merge_body(h_ref, ya_ref, yb_ref, yc_ref, g_ref, wg_ref, wa_ref, wb_ref, wc_ref, wo_ref, pg_ref, o_ref):
    d = D_MODEL
    h = h_ref[...]
    u = _rms(h, g_ref[...]).astype(BF16)
    merged = None
    for n, (y_ref, w_ref) in enumerate(((ya_ref, wa_ref), (yb_ref, wb_ref), (yc_ref, wc_ref))):
        gate = _sigmoid(jnp.dot(u, wg_ref[:, n * d:(n + 1) * d], preferred_element_type=F32))
        term = gate * _bdot(y_ref[...], w_ref[...])
        merged = term if merged is None else merged + term
    o_ref[...] = h + _rms(_bdot(merged, wo_ref[...]), pg_ref[...])


def _merge(h, ya, yb, yc, pre_g, wg, wa, wb, wc, wo, post_g):
    s = h.shape[0]
    tm = ROW_TILE
    row = lambda n: pl.BlockSpec((tm, n), lambda i: (i, 0))
    const = lambda a: pl.BlockSpec(a.shape, lambda i: (0,) * a.ndim)
    consts = [pre_g, wg, wa, wb, wc, wo, post_g]
    return pl.pallas_call(
        _merge_body, grid=(s // tm,),
        in_specs=[row(D_MODEL), row(A_WIDTH), row(B_HEADS * LANES), row(C_WIDTH)] + [const(a) for a in consts],
        out_specs=row(D_MODEL), out_shape=jax.ShapeDtypeStruct((s, D_MODEL), F32),
        compiler_params=_params("arbitrary"), name="merge",
    )(h, ya, yb, yc, *consts)


def _pad_cols(a, n):
    return jnp.pad(a, ((0, 0), (0, n - a.shape[1])))


def _pad_rows(a, n):
    return jnp.pad(a, ((0, n - a.shape[0]), (0, 0)))


def _rope_half_swap(a):
    half = B_ROPE // 2
    return jnp.concatenate([a[:, half:], a[:, :half]], axis=1)


def _layer_weights(l, w_in, rwkv_mu, vres_down, vres_mu, w_uq, w_ukv, mla_out):
    wi = w_in[l]
    d = D_MODEL
    o = 3 * A_WIDTH
    lora = [wi[:, o:o + 64], wi[:, o + 64:o + 128], wi[:, o + 128:o + 256]]
    mus = [rwkv_mu[l][o:o + 64], rwkv_mu[l][o + 64:o + 128], rwkv_mu[l][o + 128:o + 256]]
    if l > 0:
        lora.append(vres_down[l - 1])
        mus.append(vres_mu[l - 1])
    else:
        lora.append(jnp.zeros((d, 0), F32))
        mus.append(jnp.zeros((0,), F32))
    w_a = jnp.concatenate([wi[:, :o]] + [_pad_cols(t, LANES) for t in lora], axis=1)
    mu_a = jnp.concatenate([rwkv_mu[l][:o]] + [jnp.pad(t, (0, LANES - t.shape[0])) for t in mus])[None, :]

    o = A_COLS
    w_kr = wi[:, o + B_Q_RANK + B_KV_RANK:o + B_Q_RANK + B_KV_RANK + B_ROPE]
    place = lambda t: jnp.pad(t, ((0, 0), (B_NOPE, LANES - B_NOPE - B_ROPE)))
    w_b = jnp.concatenate([wi[:, o:o + B_Q_RANK + B_KV_RANK], place(w_kr), place(_rope_half_swap(w_kr))], axis=1)
    o += B_Q_RANK + B_KV_RANK + B_ROPE
    w_c = wi[:, o:o + C_IN]
    w_g = wi[:, o + C_IN:o + C_IN + N_BRANCH * d]

    scale = (B_NOPE + B_ROPE) ** -0.5 * np.log2(np.e)
    uq = (w_uq[l] * scale).reshape(B_Q_RANK, B_HEADS, B_NOPE + B_ROPE)
    uq_sw = jnp.concatenate([jnp.zeros_like(uq[..., :B_NOPE]), uq[..., B_NOPE + B_ROPE // 2:],
                             uq[..., B_NOPE:B_NOPE + B_ROPE // 2]], axis=-1)
    pad_head = lambda t: jnp.pad(t, ((0, 0), (0, 0), (0, LANES - t.shape[-1]))).reshape(t.shape[0], B_HEADS * LANES)
    ukv = w_ukv[l].reshape(B_KV_RANK, B_HEADS, B_NOPE + B_V)
    mo = jnp.pad(mla_out[l].reshape(B_HEADS, B_V, d), ((0, 0), (0, LANES - B_V), (0, 0))).reshape(B_HEADS * LANES, d)
    bf = lambda t: t.astype(BF16)
    return dict(w_a=bf(w_a), mu_a=mu_a, w_b=bf(w_b), w_c=bf(w_c), w_g=bf(w_g),
                w_q=bf(pad_head(uq)), w_q_swap=bf(pad_head(uq_sw)),
                w_k=bf(pad_head(ukv[..., :B_NOPE])), w_v=bf(pad_head(ukv[..., B_NOPE:])), mla_out=bf(mo))


def _rope_tables(positions):
    inv_freq = ROPE_THETA ** (-jnp.arange(0, B_ROPE, 2, dtype=F32) / B_ROPE)
    ang = positions.astype(F32)[:, None] * inv_freq
    cos, sin = jnp.cos(ang), jnp.sin(ang)
    s = positions.shape[0]
    pad = jnp.zeros((s, LANES - B_NOPE - B_ROPE), F32)
    cos_q = jnp.concatenate([jnp.ones((s, B_NOPE), F32), cos, cos, pad], axis=1)
    sin_t = jnp.concatenate([jnp.zeros((s, B_NOPE), F32), -sin, sin, pad], axis=1)
    cos_k = jnp.concatenate([jnp.zeros((s, B_NOPE), F32), cos, cos, pad], axis=1)
    return cos_q, sin_t, cos_k, sin_t


def kernel(x, positions, ffn1_pre_g, ffn1_post_g, ffn1_w_gate, ffn1_w_up, ffn1_w_down, mix_pre_g, mix_post_g, w_in, rwkv_mu, rwkv_w0, rwkv_w_up, rwkv_a0, rwkv_a_up, rwkv_g_up, rwkv_k_k, rwkv_k_a, rwkv_r_k, rwkv_gn_g, rwkv_gn_b, rwkv_vres_down, rwkv_vres_mu, rwkv_vres_up, rwkv_v0, rwkv_out, mla_q_norm_g, mla_w_uq, mla_kv_norm_g, mla_w_ukv, mla_out, hgrn_lower_bounds, hgrn_norm_g, hgrn_out, w_o, ffn2_pre_g, ffn2_post_g, ffn2_w_gate, ffn2_w_up, ffn2_w_down):
    bsz, seq, d = x.shape
    assert bsz == 1 and d == D_MODEL and seq % max(ROW_TILE, C_CHUNK * C_CHUNKS_PER_STEP, A_CHUNK * A_CHUNKS_PER_STEP) == 0
    depth = w_in.shape[0]
    flash_tile = min(1024, seq)
    tabs = _rope_tables(positions[0])
    lb_p = jax.nn.softmax(hgrn_lower_bounds.astype(F32), axis=0)
    lower_bounds = jnp.cumsum(lb_p, axis=0) - lb_p[0]
    row = lambda t: t[None, :]
    bf = lambda t: t.astype(BF16)

    h = x[0]
    v_first = None
    for l in range(depth):
        h = _ffn(h, row(ffn1_pre_g[l]), bf(ffn1_w_gate[l]), bf(ffn1_w_up[l]), bf(ffn1_w_down[l]),
                 row(ffn1_post_g[l]))

        lw = _layer_weights(l, w_in, rwkv_mu, rwkv_vres_down, rwkv_vres_mu, mla_w_uq, mla_w_ukv, mla_out)
        pre_g = row(mix_pre_g[l])
        x_a = _proj(h, pre_g, lw["w_a"], lw["mu_a"])
        x_b = _proj(h, pre_g, lw["w_b"])
        x_c = _proj(h, pre_g, lw["w_c"])

        pa = dict(w0=row(rwkv_w0[l]), w_up=bf(_pad_rows(rwkv_w_up[l], LANES)), a0=row(rwkv_a0[l]),
                  a_up=bf(_pad_rows(rwkv_a_up[l], LANES)), g_up=bf(rwkv_g_up[l]), k_k=row(rwkv_k_k[l]),
                  k_a=row(rwkv_k_a[l]), r_k=row(rwkv_r_k[l].reshape(-1)), gn_g=row(rwkv_gn_g[l]),
                  gn_b=row(rwkv_gn_b[l]))
        if l > 0:
            pa.update(v0=row(rwkv_v0[l - 1]), vres_up=bf(_pad_rows(rwkv_vres_up[l - 1], LANES)))
        y_a, v_first = _rwkv(x_a, pa, v_first)

        pb = dict(q_norm_g=row(mla_q_norm_g[l]), kv_norm_g=row(mla_kv_norm_g[l]), w_q=lw["w_q"],
                  w_q_swap=lw["w_q_swap"], w_k=lw["w_k"], w_v=lw["w_v"])
        q_h, k_h, vt_h = _mla_prep(x_b, tabs, pb)
        y_b = _flash(q_h, k_h, vt_h, flash_tile)

        y_c = _hgrn(x_c, row(lower_bounds[l]), row(hgrn_norm_g[l]))

        h = _merge(h, y_a, y_b, y_c, pre_g, lw["w_g"], bf(rwkv_out[l]), lw["mla_out"], bf(hgrn_out[l]),
                   bf(w_o[l]), row(mix_post_g[l]))

        h = _ffn(h, row(ffn2_pre_g[l]), bf(ffn2_w_gate[l]), bf(ffn2_w_up[l]), bf(ffn2_w_down[l]),
                 row(ffn2_post_g[l]))
    return h[None]
```

```python
import functools

import jax
import jax.numpy as jnp
import numpy as np
from jax import lax
from jax.experimental import pallas as pl
from jax.experimental.pallas import tpu as pltpu

F32 = jnp.float32
BF16 = jnp.bfloat16

D_MODEL = 1024
D_FF = 2816
NORM_EPS = 1e-6
MACARON_WEIGHT = 0.5

A_HEADS = 8
A_HEAD_DIM = 64
A_WIDTH = 512
A_GN_EPS = 64e-5
A_CHUNK = 64
A_CHUNKS_PER_STEP = 4
A_COLS = 1792
A_IN = 2048

B_HEADS = 8
B_NOPE = 64
B_ROPE = 32
B_V = 64
B_Q_RANK = 384
B_KV_RANK = 256
B_IN = 896
ROPE_THETA = 10000.0
B_SUBTILES = 2
LANES = 128

C_HEADS = 4
C_DIM = 128
C_WIDTH = 512
C_IN = 2048
C_CHUNK = 128
C_CHUNKS_PER_STEP = 2
C_SUB = 8
C_MIN_FORGET = 1e-6

N_BRANCH = 3
ROW_TILE = 512
FFN_ROW_TILE = 1024
VMEM_LIMIT = 56 * 1024 * 1024

def _bdot(a, b):
    return jnp.dot(a.astype(BF16), b.astype(BF16), preferred_element_type=F32)


def _bdot_nt(a, b):
    return lax.dot_general(a.astype(BF16), b.astype(BF16), (((1,), (1,)), ((), ())),
                           preferred_element_type=F32)


def _bmm(a, b):
    return lax.dot_general(a.astype(BF16), b.astype(BF16), (((2,), (1,)), ((0,), (0,))),
                           preferred_element_type=F32)


def _bmm_nt(a, b):
    return lax.dot_general(a.astype(BF16), b.astype(BF16), (((2,), (2,)), ((0,), (0,))),
                           preferred_element_type=F32)


def _split_terms(data, parts):
    terms, rem = [], data
    for _ in range(parts):
        piece = rem.astype(BF16)
        terms.append(piece)
        rem = rem - piece.astype(F32)
    return terms


def _sel_dot(sel, data, parts):
    sel = sel.astype(BF16)
    return sum(jnp.dot(sel, t, preferred_element_type=F32) for t in _split_terms(data, parts))


def _dot_sel(data, sel, parts):
    sel = sel.astype(BF16)
    return sum(jnp.dot(t, sel, preferred_element_type=F32) for t in _split_terms(data, parts))


def _rms(x, g):
    return x * lax.rsqrt(jnp.mean(x * x, axis=-1, keepdims=True) + NORM_EPS) * g


def _sigmoid(x):
    return jax.nn.sigmoid(x)


def _params(*sem):
    return pltpu.CompilerParams(dimension_semantics=sem, vmem_limit_bytes=VMEM_LIMIT)


def _ffn_body(x_ref, pre_g_ref, wg_ref, wu_ref, wd_ref, post_g_ref, o_ref):
    x = x_ref[...]
    xn = _rms(x, pre_g_ref[...]).astype(BF16)
    gate = jnp.dot(xn, wg_ref[...], preferred_element_type=F32)
    up = jnp.dot(xn, wu_ref[...], preferred_element_type=F32)
    mid = (gate * _sigmoid(gate) * up).astype(BF16)
    y = jnp.dot(mid, wd_ref[...], preferred_element_type=F32)
    o_ref[...] = x + MACARON_WEIGHT * _rms(y, post_g_ref[...])


def _ffn(h, pre_g, wg, wu, wd, post_g):
    s = h.shape[0]
    tm = FFN_ROW_TILE
    const = lambda a: pl.BlockSpec(a.shape, lambda i: (0,) * a.ndim)
    return pl.pallas_call(
        _ffn_body,
        grid=(s // tm,),
        in_specs=[pl.BlockSpec((tm, D_MODEL), lambda i: (i, 0)), const(pre_g), const(wg), const(wu), const(wd),
                  const(post_g)],
        out_specs=pl.BlockSpec((tm, D_MODEL), lambda i: (i, 0)),
        out_shape=jax.ShapeDtypeStruct((s, D_MODEL), F32),
        compiler_params=_params("arbitrary"),
        name="ffn",
    )(h, pre_g, wg, wu, wd, post_g)


def _proj_body(x_ref, g_ref, w_ref, o_ref):
    xn = _rms(x_ref[...], g_ref[...]).astype(BF16)
    o_ref[...] = jnp.dot(xn, w_ref[...], preferred_element_type=F32)


def _proj_shift_body(x_ref, g_ref, w_ref, mu_ref, o_ref, carry_ref):
    @pl.when(pl.program_id(0) == 0)
    def _():
        carry_ref[...] = jnp.zeros_like(carry_ref)

    xn = _rms(x_ref[...], g_ref[...]).astype(BF16)
    p = jnp.dot(xn, w_ref[...], preferred_element_type=F32)
    rows = p.shape[0]
    prev = pltpu.roll(p, 1, axis=0)
    first = lax.broadcasted_iota(jnp.int32, p.shape, 0) == 0
    prev = jnp.where(first, carry_ref[0:1, :], prev)
    carry_ref[0:1, :] = p[rows - 1:rows, :]
    o_ref[...] = p + (prev - p) * mu_ref[...]


def _proj(h, pre_g, w, mu=None):
    s = h.shape[0]
    n = w.shape[1]
    x_spec = pl.BlockSpec((ROW_TILE, D_MODEL), lambda i: (i, 0))
    g_spec = pl.BlockSpec((1, D_MODEL), lambda i: (0, 0))
    w_spec = pl.BlockSpec((D_MODEL, n), lambda i: (0, 0))
    o_spec = pl.BlockSpec((ROW_TILE, n), lambda i: (i, 0))
    out_shape = jax.ShapeDtypeStruct((s, n), F32)
    if mu is None:
        return pl.pallas_call(
            _proj_body, grid=(s // ROW_TILE,), in_specs=[x_spec, g_spec, w_spec], out_specs=o_spec,
            out_shape=out_shape, compiler_params=_params("arbitrary"), name="proj",
        )(h, pre_g, w)
    return pl.pallas_call(
        _proj_shift_body, grid=(s // ROW_TILE,),
        in_specs=[x_spec, g_spec, w_spec, pl.BlockSpec((1, n), lambda i: (0, 0))], out_specs=o_spec,
        out_shape=out_shape, scratch_shapes=[pltpu.VMEM((8, n), F32)],
        compiler_params=_params("arbitrary"), name="proj_shift",
    )(h, pre_g, w, mu)


def _tri_inverse(a_strict, eye, blk16, lvl1, lvl2):
    d = a_strict * blk16
    d2 = _bmm(d, d)
    d4 = _bmm(d2, d2)
    d8 = _bmm(d4, d4)
    t = eye + d
    t = t + _bmm(t, d2)
    t = t + _bmm(t, d4)
    t = t + _bmm(t, d8)
    t = t + _bmm(_bmm(t, a_strict * lvl1), t)
    t = t + _bmm(_bmm(t, a_strict * lvl2), t)
    return t


def _rwkv_body(has_vres, *refs):
    if has_vres:
        (x_ref, vf_ref, w0_ref, wup_ref, a0_ref, aup_ref, gup_ref, kk_ref, ka_ref, rk_ref, gng_ref, gnb_ref,
         v0_ref, vup_ref, y_ref, state_ref) = refs
    else:
        (x_ref, w0_ref, wup_ref, a0_ref, aup_ref, gup_ref, kk_ref, ka_ref, rk_ref, gng_ref, gnb_ref,
         y_ref, vf_out_ref, state_ref) = refs

    @pl.when(pl.program_id(0) == 0)
    def _():
        state_ref[...] = jnp.zeros_like(state_ref)

    c = A_CHUNK
    w = A_WIDTH
    hd = A_HEAD_DIM
    x = x_ref[...]
    rows = x.shape[0]
    r = x[:, 0:w]
    k = x[:, w:2 * w]
    v = x[:, 2 * w:3 * w]
    wl = x[:, 3 * w:3 * w + LANES]
    al = x[:, 3 * w + LANES:3 * w + 2 * LANES]
    gl = x[:, 3 * w + 2 * LANES:3 * w + 3 * LANES]

    z = w0_ref[...] + _bdot(jnp.tanh(wl), wup_ref[...])
    softplus_neg = jnp.maximum(-z, 0.0) + jnp.log(1.0 + jnp.exp(-jnp.abs(z)))
    logw = -jnp.exp(-softplus_neg - 0.5)
    a = _sigmoid(a0_ref[...] + _bdot(al, aup_ref[...]))
    g = _bdot(_sigmoid(gl), gup_ref[...])
    if has_vres:
        vl = x[:, 3 * w + 3 * LANES:3 * w + 4 * LANES]
        v = v + (vf_ref[...] - v) * _sigmoid(v0_ref[...] + _bdot(vl, vup_ref[...]))
    else:
        vf_out_ref[...] = v

    li = lax.broadcasted_iota(jnp.int32, (LANES, LANES), 0)
    lj = lax.broadcasted_iota(jnp.int32, (LANES, LANES), 1)
    pair_bd = ((li < hd) == (lj < hd)).astype(F32)
    lane = lax.broadcasted_iota(jnp.int32, (1, LANES), 1)
    m0 = (lane < hd).astype(F32)
    m1 = (lane >= hd).astype(F32)

    def head_sum(t):
        return jnp.concatenate([_dot_sel(t[:, p * LANES:(p + 1) * LANES], pair_bd, 2) for p in range(w // LANES)],
                               axis=1)

    def by_head(t):
        return jnp.concatenate([t * m0, t * m1], axis=0)

    kkr = k * kk_ref[...]
    kk = kkr / jnp.maximum(jnp.sqrt(head_sum(kkr * kkr)), 1e-12)
    k = k * (1.0 + (a - 1.0) * ka_ref[...])

    ti = lax.broadcasted_iota(jnp.int32, (2 * c, 2 * c), 0)
    tj = lax.broadcasted_iota(jnp.int32, (2 * c, 2 * c), 1)
    same_head = (ti // c) == (tj // c)
    incl = (same_head & (tj <= ti)).astype(F32)
    strict = (same_head & (tj < ti)).astype(F32)
    eye = (tj == ti).astype(F32)
    blk16 = ((ti // 16) == (tj // 16)).astype(F32)
    lvl1 = (((ti // 32) == (tj // 32)) & ((ti // 16) != (tj // 16))).astype(F32)
    lvl2 = (same_head & ((ti // 32) != (tj // 32))).astype(F32)

    ri = lax.broadcasted_iota(jnp.int32, (rows, rows), 0)
    rj = lax.broadcasted_iota(jnp.int32, (rows, rows), 1)
    chunk_incl = (((ri // c) == (rj // c)) & (rj <= ri)).astype(F32)
    lb = _sel_dot(chunk_incl, logw, 3)
    e_neg = jnp.exp(-lb)
    alpha_t = -kk * jnp.exp(lb - logw)
    beta = kk * a
    beta_h = beta * e_neg
    k_h = k * e_neg
    r_t = r * jnp.exp(lb)

    n_ch = rows // c
    n_pair = w // LANES

    def slabs(t):
        return [t[ci * c:(ci + 1) * c, p * LANES:(p + 1) * LANES] for ci in range(n_ch) for p in range(n_pair)]

    def stack_by_head(t):
        return jnp.stack([by_head(s) for s in slabs(t)])

    xa2, v2 = stack_by_head(alpha_t), stack_by_head(v)
    gram = _bmm_nt(jnp.concatenate([xa2, stack_by_head(r_t)], axis=1),
                   jnp.concatenate([stack_by_head(beta_h), stack_by_head(k_h)], axis=1))
    a_ab = gram[:, 0:2 * c, 0:2 * c] * strict
    a_ak = gram[:, 0:2 * c, 2 * c:4 * c] * strict
    a_r = jnp.concatenate([gram[:, 2 * c:4 * c, 2 * c:4 * c] * incl, gram[:, 2 * c:4 * c, 0:2 * c] * incl], axis=2)
    t = _tri_inverse(a_ab, eye, blk16, lvl1, lvl2)
    wu = _bmm(t, jnp.concatenate([xa2, _bmm(a_ak, v2)], axis=2))
    wt = wu[:, 0:c, 0:LANES] + wu[:, c:2 * c, 0:LANES]
    ut = wu[:, 0:c, LANES:2 * LANES] + wu[:, c:2 * c, LANES:2 * LANES]
    xr = jnp.stack(slabs(r_t))
    vp = jnp.stack(slabs(v))

    st = state_ref[...]
    y_chunks = []
    for ci in range(n_ch):
        bs = slice(ci * n_pair, (ci + 1) * n_pair)
        lb_last = lb[(ci + 1) * c - 1:(ci + 1) * c, :]
        e_end = jnp.exp(lb_last - lb[ci * c:(ci + 1) * c])
        k_e = k[ci * c:(ci + 1) * c] * e_end
        beta_e = beta[ci * c:(ci + 1) * c] * e_end
        gamma_c = jnp.exp(lb_last)
        u = _bmm_nt(wt[bs], st) + ut[bs]
        u2 = jnp.concatenate([u * m0, u * m1], axis=1)
        y2 = _bmm(a_r[bs], jnp.concatenate([v2[bs], u2], axis=1))
        y = _bmm_nt(xr[bs], st) + y2[:, 0:c] + y2[:, c:2 * c]
        y_chunks.append(jnp.concatenate([y[p] for p in range(n_pair)], axis=1))
        vu_t = jnp.stack([jnp.concatenate([vp[ci * n_pair + p], u[p]], axis=0).T for p in range(n_pair)])
        ke_be = jnp.stack([jnp.concatenate([k_e[:, p * LANES:(p + 1) * LANES], beta_e[:, p * LANES:(p + 1) * LANES]],
                                           axis=0) for p in range(n_pair)])
        decay = jnp.stack([gamma_c[:, p * LANES:(p + 1) * LANES] for p in range(n_pair)])
        st = st * decay + pair_bd * _bmm(vu_t, ke_be)
    state_ref[...] = st
    y = jnp.concatenate(y_chunks, axis=0)

    inv_n = 1.0 / hd
    mean = head_sum(y) * inv_n
    yc = y - mean
    var = head_sum(yc * yc) * inv_n
    yn = yc * lax.rsqrt(var + A_GN_EPS) * gng_ref[...] + gnb_ref[...]
    bonus = head_sum(r * k * rk_ref[...]) * v
    y_ref[...] = (yn + bonus) * g


def _rwkv(x, p, v_first):
    s = x.shape[0]
    rows = A_CHUNK * A_CHUNKS_PER_STEP
    has_vres = v_first is not None
    row = lambda n: pl.BlockSpec((rows, n), lambda i: (i, 0))
    const = lambda a: pl.BlockSpec(a.shape, lambda i: (0,) * a.ndim)
    names = ["w0", "w_up", "a0", "a_up", "g_up", "k_k", "k_a", "r_k", "gn_g", "gn_b"]
    if has_vres:
        names += ["v0", "vres_up"]
    consts = [p[n] for n in names]
    ins = [x] + ([v_first] if has_vres else []) + consts
    in_specs = [row(A_IN)] + ([row(A_WIDTH)] if has_vres else []) + [const(a) for a in consts]
    y_shape = jax.ShapeDtypeStruct((s, A_WIDTH), F32)
    out = pl.pallas_call(
        functools.partial(_rwkv_body, has_vres),
        grid=(s // rows,), in_specs=in_specs,
        out_specs=row(A_WIDTH) if has_vres else [row(A_WIDTH), row(A_WIDTH)],
        out_shape=y_shape if has_vres else [y_shape, y_shape],
        scratch_shapes=[pltpu.VMEM((A_WIDTH // LANES, LANES, LANES), F32)],
        compiler_params=_params("arbitrary"), name="rwkv7",
    )(*ins)
    return (out, v_first) if has_vres else (out[0], out[1])


def _mla_prep_body(x_ref, cq_ref, sq_ref, ck_ref, sk_ref, qg_ref, wq_ref, wqs_ref, kg_ref, wk_ref, wv_ref,
                   q_ref, k_ref, vt_ref):
    x = x_ref[...]
    cq = x[:, 0:B_Q_RANK]
    ckv = x[:, B_Q_RANK:B_Q_RANK + B_KV_RANK]
    kr = x[:, B_Q_RANK + B_KV_RANK:B_Q_RANK + B_KV_RANK + LANES]
    kr_sw = x[:, B_Q_RANK + B_KV_RANK + LANES:B_Q_RANK + B_KV_RANK + 2 * LANES]
    cqn = _rms(cq, qg_ref[...]).astype(BF16)
    ckn = _rms(ckv, kg_ref[...]).astype(BF16)
    q = jnp.dot(cqn, wq_ref[...], preferred_element_type=F32)
    q_sw = jnp.dot(cqn, wqs_ref[...], preferred_element_type=F32)
    k_nope = jnp.dot(ckn, wk_ref[...], preferred_element_type=F32)
    val = jnp.dot(ckn, wv_ref[...], preferred_element_type=F32)
    k_rope = kr * ck_ref[...] + kr_sw * sk_ref[...]
    cq_t, sq_t = cq_ref[...], sq_ref[...]
    ones_col = (lax.broadcasted_iota(jnp.int32, (1, LANES), 1) == B_V).astype(F32)
    for hd in range(B_HEADS):
        sl = slice(hd * LANES, (hd + 1) * LANES)
        q_ref[hd] = (q[:, sl] * cq_t + q_sw[:, sl] * sq_t).astype(BF16)
        k_ref[hd] = (k_nope[:, sl] + k_rope).astype(BF16)
        vt_ref[hd] = (val[:, sl] + ones_col).T.astype(BF16)


def _mla_prep(x, tabs, p):
    s = x.shape[0]
    tm = ROW_TILE
    row = lambda n: pl.BlockSpec((tm, n), lambda i: (i, 0))
    const = lambda a: pl.BlockSpec(a.shape, lambda i: (0,) * a.ndim)
    consts = [p["q_norm_g"], p["w_q"], p["w_q_swap"], p["kv_norm_g"], p["w_k"], p["w_v"]]
    head_spec = pl.BlockSpec((B_HEADS, tm, LANES), lambda i: (0, i, 0))
    head_shape = jax.ShapeDtypeStruct((B_HEADS, s, LANES), BF16)
    head_t_spec = pl.BlockSpec((B_HEADS, LANES, tm), lambda i: (0, 0, i))
    head_t_shape = jax.ShapeDtypeStruct((B_HEADS, LANES, s), BF16)
    return pl.pallas_call(
        _mla_prep_body, grid=(s // tm,),
        in_specs=[row(B_IN)] + [row(LANES)] * 4 + [const(a) for a in consts],
        out_specs=[head_spec, head_spec, head_t_spec], out_shape=[head_shape, head_shape, head_t_shape],
        compiler_params=_params("arbitrary"), name="mla_prep",
    )(x, *tabs, *consts)


def _flash_body(q_ref, qn_ref, k_ref, vt_ref, o_ref, m_ref, acc_ref, s0_ref, s1_ref, s2_ref):
    i = pl.program_id(1)
    tile = q_ref.shape[1]
    sub = tile // B_SUBTILES
    m_ref[...] = jnp.full_like(m_ref, -jnp.inf)
    acc_ref[...] = jnp.zeros_like(acc_ref)

    def scores(queries_ref, j, s_ref, a):
        start = pl.multiple_of(j * tile, tile)
        s_ref[:, a * sub:(a + 1) * sub] = lax.dot_general(
            k_ref[0, pl.ds(start, tile), :], queries_ref[0, a * sub:(a + 1) * sub, :], (((1,), (1,)), ((), ())),
            preferred_element_type=F32)

    def softmax_pv(j, s_ref, a, on_diagonal):
        start = pl.multiple_of(j * tile, tile)
        cs = slice(a * sub, (a + 1) * sub)
        n_keys = (a + 1) * sub if on_diagonal else tile
        s = s_ref[0:n_keys, cs]
        if on_diagonal:
            keys = lax.broadcasted_iota(jnp.int32, s.shape, 0)
            queries = lax.broadcasted_iota(jnp.int32, s.shape, 1) + a * sub
            s = jnp.where(keys <= queries, s, -jnp.inf)
        m_old = m_ref[:, cs]
        m_new = jnp.maximum(m_old, jnp.max(s, axis=0, keepdims=True))
        p = jnp.exp2(s - m_new).astype(BF16)
        pv = jnp.dot(vt_ref[0, :, pl.ds(start, n_keys)], p, preferred_element_type=F32)
        acc_ref[:, cs] = jnp.exp2(m_old - m_new) * acc_ref[:, cs] + pv
        m_ref[:, cs] = m_new

    def phase(j, cur_ref, nxt_ref):
        for a in range(B_SUBTILES):
            scores(q_ref, j + 1, nxt_ref, a)
            softmax_pv(j, cur_ref, a, False)

    def diagonal(cur_ref, overlap_next):
        for a in range(B_SUBTILES):
            if overlap_next:
                scores(qn_ref, 0, s2_ref, a)
            softmax_pv(i, cur_ref, a, True)
        acc = acc_ref[...]
        o_ref[...] = (acc / acc[B_V:B_V + 1, :]).T

    @pl.when(i == 0)
    def _():
        for a in range(B_SUBTILES):
            scores(q_ref, 0, s2_ref, a)
        diagonal(s2_ref, False)
        for a in range(B_SUBTILES):
            scores(qn_ref, 0, s2_ref, a)

    @pl.when(i > 0)
    def _():
        phase(0, s2_ref, s0_ref)

        def two_tiles(jj, carry):
            phase(2 * jj + 1, s0_ref, s1_ref)
            phase(2 * jj + 2, s1_ref, s0_ref)
            return carry

        lax.fori_loop(0, (i - 1) // 2, two_tiles, 0)

        @pl.when(i % 2 == 0)
        def _():
            phase(i - 1, s0_ref, s1_ref)
            diagonal(s1_ref, True)

        @pl.when(i % 2 == 1)
        def _():
            diagonal(s0_ref, True)


def _flash(q, k, vt, tile):
    n_h, s, _ = q.shape
    n_q = s // tile
    scores = pltpu.VMEM((tile, tile), F32)
    return pl.pallas_call(
        _flash_body, grid=(n_h, n_q),
        in_specs=[pl.BlockSpec((1, tile, LANES), lambda h, i: (h, i, 0)),
                  pl.BlockSpec((1, tile, LANES), lambda h, i: (h, jnp.minimum(i + 1, n_q - 1), 0)),
                  pl.BlockSpec((1, s, LANES), lambda h, i: (h, 0, 0)),
                  pl.BlockSpec((1, LANES, s), lambda h, i: (h, 0, 0))],
        out_specs=pl.BlockSpec((tile, LANES), lambda h, i: (i, h)),
        out_shape=jax.ShapeDtypeStruct((s, n_h * LANES), F32),
        scratch_shapes=[pltpu.VMEM((1, tile), F32), pltpu.VMEM((LANES, tile), F32), scores, scores, scores],
        compiler_params=_params("arbitrary", "arbitrary"), name="mla_flash",
    )(q, q, k, vt)


def _hgrn_body(x_ref, lb_ref, ng_ref, o_ref, state_ref):
    @pl.when(pl.program_id(0) == 0)
    def _():
        state_ref[...] = jnp.zeros_like(state_ref)

    c = C_CHUNK
    w = C_WIDTH
    x = x_ref[...]
    rows = x.shape[0]
    n_ch = rows // c
    lower = lb_ref[...]
    fz = x[:, w:2 * w]
    f = lower + (1.0 - lower) * _sigmoid(fz)
    logf = jnp.log(jnp.maximum(f, C_MIN_FORGET))
    kf = (1.0 - lower) * _sigmoid(-fz)
    cq = x[:, 0:w]
    q = cq * _sigmoid(cq)
    val = x[:, 2 * w:3 * w]
    cg = x[:, 3 * w:4 * w]

    def heads(t):
        return jnp.stack([t[ci * c:(ci + 1) * c, hd * C_DIM:(hd + 1) * C_DIM]
                          for ci in range(n_ch) for hd in range(C_HEADS)])

    ri = lax.broadcasted_iota(jnp.int32, (rows, rows), 0)
    rj = lax.broadcasted_iota(jnp.int32, (rows, rows), 1)
    row = lax.broadcasted_iota(jnp.int32, (rows, 1), 0)
    ti = lax.broadcasted_iota(jnp.int32, (c, c), 0)
    tj = lax.broadcasted_iota(jnp.int32, (c, c), 1)
    b = _sel_dot((((ri // c) == (rj // c)) & (rj <= ri)).astype(F32), logf, 3)

    def block_row(size, offset):
        return jnp.concatenate([jnp.broadcast_to(b[r + offset:r + offset + 1, :], (size, w))
                                for r in range(0, rows, size)], axis=0)

    b_last = block_row(c, c - 1)

    n0 = C_SUB
    ref = block_row(n0, n0 // 2 - 1)
    mask = ((ti // n0) == (tj // n0)) & (tj <= ti)
    attn = jnp.where(mask, _bmm_nt(heads(q * jnp.exp(b - ref)), heads(kf * jnp.exp(ref - b))), 0.0)
    n = n0
    while n < c:
        ref = block_row(2 * n, n - 1)
        right = ((row // n) % 2) == 1
        q_s = q * jnp.exp(jnp.where(right, b - ref, 0.0))
        k_s = kf * jnp.exp(jnp.where(right, 0.0, ref - b))
        mask = ((ti // (2 * n)) == (tj // (2 * n))) & (((ti // n) % 2) == 1) & (((tj // n) % 2) == 0)
        attn = jnp.where(mask, _bmm_nt(heads(q_s), heads(k_s)), attn)
        n *= 2

    vals = heads(val)
    intra = _bmm(attn, vals)
    q_in = heads(q * jnp.exp(b))
    k_end = heads(kf * jnp.exp(b_last - b))
    g_end = jnp.exp(b_last)
    gates = cg * _sigmoid(cg)
    st = state_ref[...]
    out_chunks = []
    for ci in range(n_ch):
        bs = slice(ci * C_HEADS, (ci + 1) * C_HEADS)
        o = intra[bs] + _bmm_nt(q_in[bs], st)
        vt = jnp.stack([vals[ci * C_HEADS + hd].T for hd in range(C_HEADS)])
        decay = jnp.stack([g_end[(ci + 1) * c - 1:(ci + 1) * c, hd * C_DIM:(hd + 1) * C_DIM] for hd in range(C_HEADS)])
        st = st * decay + _bmm(vt, k_end[bs])
        out_chunks.append(jnp.concatenate([_rms(o[hd], ng_ref[...]) for hd in range(C_HEADS)], axis=1))
    state_ref[...] = st
    o_ref[...] = jnp.concatenate(out_chunks, axis=0) * gates


def _hgrn(x, lower, norm_g):
    s = x.shape[0]
    rows = C_CHUNK * C_CHUNKS_PER_STEP
    return pl.pallas_call(
        _hgrn_body, grid=(s // rows,),
        in_specs=[pl.BlockSpec((rows, C_IN), lambda i: (i, 0)), pl.BlockSpec((1, C_WIDTH), lambda i: (0, 0)),
                  pl.BlockSpec((1, C_DIM), lambda i: (0, 0))],
        out_specs=pl.BlockSpec((rows, C_WIDTH), lambda i: (i, 0)),
        out_shape=jax.ShapeDtypeStruct((s, C_WIDTH), F32),
        scratch_shapes=[pltpu.VMEM((C_HEADS, C_DIM, C_DIM), F32)],
        compiler_params=_params("arbitrary"), name="hgrn2",
    )(x, lower, norm_g)


def _merge_body(h_ref, ya_ref, yb_ref, yc_ref, g_ref, wg_ref, wa_ref, wb_ref, wc_ref, wo_ref, pg_ref, o_ref):
    d = D_MODEL
    h = h_ref[...]
    u = _rms(h, g_ref[...]).astype(BF16)
    merged = None
    for n, (y_ref, w_ref) in enumerate(((ya_ref, wa_ref), (yb_ref, wb_ref), (yc_ref, wc_ref))):
        gate = _sigmoid(jnp.dot(u, wg_ref[:, n * d:(n + 1) * d], preferred_element_type=F32))
        term = gate * _bdot(y_ref[...], w_ref[...])
        merged = term if merged is None else merged + term
    o_ref[...] = h + _rms(_bdot(merged, wo_ref[...]), pg_ref[...])


def _merge(h, ya, yb, yc, pre_g, wg, wa, wb, wc, wo, post_g):
    s = h.shape[0]
    tm = ROW_TILE
    row = lambda n: pl.BlockSpec((tm, n), lambda i: (i, 0))
    const = lambda a: pl.BlockSpec(a.shape, lambda i: (0,) * a.ndim)
    consts = [pre_g, wg, wa, wb, wc, wo, post_g]
    return pl.pallas_call(
        _merge_body, grid=(s // tm,),
        in_specs=[row(D_MODEL), row(A_WIDTH), row(B_HEADS * LANES), row(C_WIDTH)] + [const(a) for a in consts],
        out_specs=row(D_MODEL), out_shape=jax.ShapeDtypeStruct((s, D_MODEL), F32),
        compiler_params=_params("arbitrary"), name="merge",
    )(h, ya, yb, yc, *consts)


def _pad_cols(a, n):
    return jnp.pad(a, ((0, 0), (0, n - a.shape[1])))


def _pad_rows(a, n):
    return jnp.pad(a, ((0, n - a.shape[0]), (0, 0)))


def _rope_half_swap(a):
    half = B_ROPE // 2
    return jnp.concatenate([a[:, half:], a[:, :half]], axis=1)


def _layer_weights(l, w_in, rwkv_mu, vres_down, vres_mu, w_uq, w_ukv, mla_out):
    wi = w_in[l]
    d = D_MODEL
    o = 3 * A_WIDTH
    lora = [wi[:, o:o + 64], wi[:, o + 64:o + 128], wi[:, o + 128:o + 256]]
    mus = [rwkv_mu[l][o:o + 64], rwkv_mu[l][o + 64:o + 128], rwkv_mu[l][o + 128:o + 256]]
    if l > 0:
        lora.append(vres_down[l - 1])
        mus.append(vres_mu[l - 1])
    else:
        lora.append(jnp.zeros((d, 0), F32))
        mus.append(jnp.zeros((0,), F32))
    w_a = jnp.concatenate([wi[:, :o]] + [_pad_cols(t, LANES) for t in lora], axis=1)
    mu_a = jnp.concatenate([rwkv_mu[l][:o]] + [jnp.pad(t, (0, LANES - t.shape[0])) for t in mus])[None, :]

    o = A_COLS
    w_kr = wi[:, o + B_Q_RANK + B_KV_RANK:o + B_Q_RANK + B_KV_RANK + B_ROPE]
    place = lambda t: jnp.pad(t, ((0, 0), (B_NOPE, LANES - B_NOPE - B_ROPE)))
    w_b = jnp.concatenate([wi[:, o:o + B_Q_RANK + B_KV_RANK], place(w_kr), place(_rope_half_swap(w_kr))], axis=1)
    o += B_Q_RANK + B_KV_RANK + B_ROPE
    w_c = wi[:, o:o + C_IN]
    w_g = wi[:, o + C_IN:o + C_IN + N_BRANCH * d]

    scale = (B_NOPE + B_ROPE) ** -0.5 * np.log2(np.e)
    uq = (w_uq[l] * scale).reshape(B_Q_RANK, B_HEADS, B_NOPE + B_ROPE)
    uq_sw = jnp.concatenate([jnp.zeros_like(uq[..., :B_NOPE]), uq[..., B_NOPE + B_ROPE // 2:],
                             uq[..., B_NOPE:B_NOPE + B_ROPE // 2]], axis=-1)
    pad_head = lambda t: jnp.pad(t, ((0, 0), (0, 0), (0, LANES - t.shape[-1]))).reshape(t.shape[0], B_HEADS * LANES)
    ukv = w_ukv[l].reshape(B_KV_RANK, B_HEADS, B_NOPE + B_V)
    mo = jnp.pad(mla_out[l].reshape(B_HEADS, B_V, d), ((0, 0), (0, LANES - B_V), (0, 0))).reshape(B_HEADS * LANES, d)
    bf = lambda t: t.astype(BF16)
    return dict(w_a=bf(w_a), mu_a=mu_a, w_b=bf(w_b), w_c=bf(w_c), w_g=bf(w_g),
                w_q=bf(pad_head(uq)), w_q_swap=bf(pad_head(uq_sw)),
                w_k=bf(pad_head(ukv[..., :B_NOPE])), w_v=bf(pad_head(ukv[..., B_NOPE:])), mla_out=bf(mo))


def _rope_tables(positions):
    inv_freq = ROPE_THETA ** (-jnp.arange(0, B_ROPE, 2, dtype=F32) / B_ROPE)
    ang = positions.astype(F32)[:, None] * inv_freq
    cos, sin = jnp.cos(ang), jnp.sin(ang)
    s = positions.shape[0]
    pad = jnp.zeros((s, LANES - B_NOPE - B_ROPE), F32)
    cos_q = jnp.concatenate([jnp.ones((s, B_NOPE), F32), cos, cos, pad], axis=1)
    sin_t = jnp.concatenate([jnp.zeros((s, B_NOPE), F32), -sin, sin, pad], axis=1)
    cos_k = jnp.concatenate([jnp.zeros((s, B_NOPE), F32), cos, cos, pad], axis=1)
    return cos_q, sin_t, cos_k, sin_t


def kernel(x, positions, ffn1_pre_g, ffn1_post_g, ffn1_w_gate, ffn1_w_up, ffn1_w_down, mix_pre_g, mix_post_g, w_in, rwkv_mu, rwkv_w0, rwkv_w_up, rwkv_a0, rwkv_a_up, rwkv_g_up, rwkv_k_k, rwkv_k_a, rwkv_r_k, rwkv_gn_g, rwkv_gn_b, rwkv_vres_down, rwkv_vres_mu, rwkv_vres_up, rwkv_v0, rwkv_out, mla_q_norm_g, mla_w_uq, mla_kv_norm_g, mla_w_ukv, mla_out, hgrn_lower_bounds, hgrn_norm_g, hgrn_out, w_o, ffn2_pre_g, ffn2_post_g, ffn2_w_gate, ffn2_w_up, ffn2_w_down):
    bsz, seq, d = x.shape
    assert bsz == 1 and d == D_MODEL and seq % max(ROW_TILE, C_CHUNK * C_CHUNKS_PER_STEP, A_CHUNK * A_CHUNKS_PER_STEP) == 0
    depth = w_in.shape[0]
    flash_tile = min(1024, seq)
    tabs = _rope_tables(positions[0])
    lb_p = jax.nn.softmax(hgrn_lower_bounds.astype(F32), axis=0)
    lower_bounds = jnp.cumsum(lb_p, axis=0) - lb_p[0]
    row = lambda t: t[None, :]
    bf = lambda t: t.astype(BF16)

    h = x[0]
    v_first = None
    for l in range(depth):
        h = _ffn(h, row(ffn1_pre_g[l]), bf(ffn1_w_gate[l]), bf(ffn1_w_up[l]), bf(ffn1_w_down[l]),
                 row(ffn1_post_g[l]))

        lw = _layer_weights(l, w_in, rwkv_mu, rwkv_vres_down, rwkv_vres_mu, mla_w_uq, mla_w_ukv, mla_out)
        pre_g = row(mix_pre_g[l])
        x_a = _proj(h, pre_g, lw["w_a"], lw["mu_a"])
        x_b = _proj(h, pre_g, lw["w_b"])
        x_c = _proj(h, pre_g, lw["w_c"])

        pa = dict(w0=row(rwkv_w0[l]), w_up=bf(_pad_rows(rwkv_w_up[l], LANES)), a0=row(rwkv_a0[l]),
                  a_up=bf(_pad_rows(rwkv_a_up[l], LANES)), g_up=bf(rwkv_g_up[l]), k_k=row(rwkv_k_k[l]),
                  k_a=row(rwkv_k_a[l]), r_k=row(rwkv_r_k[l].reshape(-1)), gn_g=row(rwkv_gn_g[l]),
                  gn_b=row(rwkv_gn_b[l]))
        if l > 0:
            pa.update(v0=row(rwkv_v0[l - 1]), vres_up=bf(_pad_rows(rwkv_vres_up[l - 1], LANES)))
        y_a, v_first = _rwkv(x_a, pa, v_first)

        pb = dict(q_norm_g=row(mla_q_norm_g[l]), kv_norm_g=row(mla_kv_norm_g[l]), w_q=lw["w_q"],
                  w_q_swap=lw["w_q_swap"], w_k=lw["w_k"], w_v=lw["w_v"])
        q_h, k_h, vt_h = _mla_prep(x_b, tabs, pb)
        y_b = _flash(q_h, k_h, vt_h, flash_tile)

        y_c = _hgrn(x_c, row(lower_bounds[l]), row(hgrn_norm_g[l]))

        h = _merge(h, y_a, y_b, y_c, pre_g, lw["w_g"], bf(rwkv_out[l]), lw["mla_out"], bf(hgrn_out[l]),
                   bf(w_o[l]), row(mix_post_g[l]))

        h = _ffn(h, row(ffn2_pre_g[l]), bf(ffn2_w_gate[l]), bf(ffn2_w_up[l]), bf(ffn2_w_down[l]),
                 row(ffn2_post_g[l]))
    return h[None]
```

```python
import functools

import jax
import jax.numpy as jnp
import numpy as np
from jax import lax
from jax.experimental import pallas as pl
from jax.experimental.pallas import tpu as pltpu

F32 = jnp.float32
BF16 = jnp.bfloat16

D_MODEL = 1024
D_FF = 2816
NORM_EPS = 1e-6
MACARON_WEIGHT = 0.5

A_HEADS = 8
A_HEAD_DIM = 64
A_WIDTH = 512
A_GN_EPS = 64e-5
A_CHUNK = 64
A_CHUNKS_PER_STEP = 4
A_COLS = 1792
A_IN = 2048

B_HEADS = 8
B_NOPE = 64
B_ROPE = 32
B_V = 64
B_Q_RANK = 384
B_KV_RANK = 256
B_IN = 896
ROPE_THETA = 10000.0
B_SUBTILES = 4
LANES = 128

C_HEADS = 4
C_DIM = 128
C_WIDTH = 512
C_IN = 2048
C_CHUNK = 128
C_CHUNKS_PER_STEP = 2
C_SUB = 8
C_MIN_FORGET = 1e-6

N_BRANCH = 3
ROW_TILE = 512
FFN_ROW_TILE = 1024
VMEM_LIMIT = 56 * 1024 * 1024

def _bdot(a, b):
    return jnp.dot(a.astype(BF16), b.astype(BF16), preferred_element_type=F32)


def _bdot_nt(a, b):
    return lax.dot_general(a.astype(BF16), b.astype(BF16), (((1,), (1,)), ((), ())),
                           preferred_element_type=F32)


def _bmm(a, b):
    return lax.dot_general(a.astype(BF16), b.astype(BF16), (((2,), (1,)), ((0,), (0,))),
                           preferred_element_type=F32)


def _bmm_nt(a, b):
    return lax.dot_general(a.astype(BF16), b.astype(BF16), (((2,), (2,)), ((0,), (0,))),
                           preferred_element_type=F32)


def _split_terms(data, parts):
    terms, rem = [], data
    for _ in range(parts):
        piece = rem.astype(BF16)
        terms.append(piece)
        rem = rem - piece.astype(F32)
    return terms


def _sel_dot(sel, data, parts):
    sel = sel.astype(BF16)
    return sum(jnp.dot(sel, t, preferred_element_type=F32) for t in _split_terms(data, parts))


def _dot_sel(data, sel, parts):
    sel = sel.astype(BF16)
    return sum(jnp.dot(t, sel, preferred_element_type=F32) for t in _split_terms(data, parts))


def _rms(x, g):
    return x * lax.rsqrt(jnp.mean(x * x, axis=-1, keepdims=True) + NORM_EPS) * g


def _sigmoid(x):
    return jax.nn.sigmoid(x)


def _params(*sem):
    return pltpu.CompilerParams(dimension_semantics=sem, vmem_limit_bytes=VMEM_LIMIT)


def _ffn_body(x_ref, pre_g_ref, wg_ref, wu_ref, wd_ref, post_g_ref, o_ref):
    x = x_ref[...]
    xn = _rms(x, pre_g_ref[...]).astype(BF16)
    gate = jnp.dot(xn, wg_ref[...], preferred_element_type=F32)
    up = jnp.dot(xn, wu_ref[...], preferred_element_type=F32)
    mid = (gate * _sigmoid(gate) * up).astype(BF16)
    y = jnp.dot(mid, wd_ref[...], preferred_element_type=F32)
    o_ref[...] = x + MACARON_WEIGHT * _rms(y, post_g_ref[...])


def _ffn(h, pre_g, wg, wu, wd, post_g):
    s = h.shape[0]
    tm = FFN_ROW_TILE
    const = lambda a: pl.BlockSpec(a.shape, lambda i: (0,) * a.ndim)
    return pl.pallas_call(
        _ffn_body,
        grid=(s // tm,),
        in_specs=[pl.BlockSpec((tm, D_MODEL), lambda i: (i, 0)), const(pre_g), const(wg), const(wu), const(wd),
                  const(post_g)],
        out_specs=pl.BlockSpec((tm, D_MODEL), lambda i: (i, 0)),
        out_shape=jax.ShapeDtypeStruct((s, D_MODEL), F32),
        compiler_params=_params("arbitrary"),
        name="ffn",
    )(h, pre_g, wg, wu, wd, post_g)


def _proj_body(x_ref, g_ref, w_ref, o_ref):
    xn = _rms(x_ref[...], g_ref[...]).astype(BF16)
    o_ref[...] = jnp.dot(xn, w_ref[...], preferred_element_type=F32)


def _proj_shift_body(x_ref, g_ref, w_ref, mu_ref, o_ref, carry_ref):
    @pl.when(pl.program_id(0) == 0)
    def _():
        carry_ref[...] = jnp.zeros_like(carry_ref)

    xn = _rms(x_ref[...], g_ref[...]).astype(BF16)
    p = jnp.dot(xn, w_ref[...], preferred_element_type=F32)
    rows = p.shape[0]
    prev = pltpu.roll(p, 1, axis=0)
    first = lax.broadcasted_iota(jnp.int32, p.shape, 0) == 0
    prev = jnp.where(first, carry_ref[0:1, :], prev)
    carry_ref[0:1, :] = p[rows - 1:rows, :]
    o_ref[...] = p + (prev - p) * mu_ref[...]


def _proj(h, pre_g, w, mu=None):
    s = h.shape[0]
    n = w.shape[1]
    x_spec = pl.BlockSpec((ROW_TILE, D_MODEL), lambda i: (i, 0))
    g_spec = pl.BlockSpec((1, D_MODEL), lambda i: (0, 0))
    w_spec = pl.BlockSpec((D_MODEL, n), lambda i: (0, 0))
    o_spec = pl.BlockSpec((ROW_TILE, n), lambda i: (i, 0))
    out_shape = jax.ShapeDtypeStruct((s, n), F32)
    if mu is None:
        return pl.pallas_call(
            _proj_body, grid=(s // ROW_TILE,), in_specs=[x_spec, g_spec, w_spec], out_specs=o_spec,
            out_shape=out_shape, compiler_params=_params("arbitrary"), name="proj",
        )(h, pre_g, w)
    return pl.pallas_call(
        _proj_shift_body, grid=(s // ROW_TILE,),
        in_specs=[x_spec, g_spec, w_spec, pl.BlockSpec((1, n), lambda i: (0, 0))], out_specs=o_spec,
        out_shape=out_shape, scratch_shapes=[pltpu.VMEM((8, n), F32)],
        compiler_params=_params("arbitrary"), name="proj_shift",
    )(h, pre_g, w, mu)


def _tri_inverse(a_strict, eye, blk16, lvl1, lvl2):
    d = a_strict * blk16
    d2 = _bmm(d, d)
    d4 = _bmm(d2, d2)
    d8 = _bmm(d4, d4)
    t = eye + d
    t = t + _bmm(t, d2)
    t = t + _bmm(t, d4)
    t = t + _bmm(t, d8)
    t = t + _bmm(_bmm(t, a_strict * lvl1), t)
    t = t + _bmm(_bmm(t, a_strict * lvl2), t)
    return t


def _rwkv_steps(has_vres, *refs):
    if has_vres:
        (x_ref, vf_ref, w0_ref, wup_ref, a0_ref, aup_ref, gup_ref, kk_ref, ka_ref, rk_ref, gng_ref, gnb_ref,
         v0_ref, vup_ref, y_ref, state_ref) = refs
    else:
        (x_ref, w0_ref, wup_ref, a0_ref, aup_ref, gup_ref, kk_ref, ka_ref, rk_ref, gng_ref, gnb_ref,
         y_ref, vf_out_ref, state_ref) = refs

    c = A_CHUNK
    w = A_WIDTH
    hd = A_HEAD_DIM
    x = x_ref[...]
    rows = x.shape[0]
    r = x[:, 0:w]
    k = x[:, w:2 * w]
    v = x[:, 2 * w:3 * w]
    wl = x[:, 3 * w:3 * w + LANES]
    al = x[:, 3 * w + LANES:3 * w + 2 * LANES]
    gl = x[:, 3 * w + 2 * LANES:3 * w + 3 * LANES]

    z = w0_ref[...] + _bdot(jnp.tanh(wl), wup_ref[...])
    softplus_neg = jnp.maximum(-z, 0.0) + jnp.log(1.0 + jnp.exp(-jnp.abs(z)))
    logw = -jnp.exp(-softplus_neg - 0.5)
    a = _sigmoid(a0_ref[...] + _bdot(al, aup_ref[...]))
    g = _bdot(_sigmoid(gl), gup_ref[...])
    if has_vres:
        vl = x[:, 3 * w + 3 * LANES:3 * w + 4 * LANES]
        v = v + (vf_ref[...] - v) * _sigmoid(v0_ref[...] + _bdot(vl, vup_ref[...]))
    else:
        vf_out_ref[...] = v

    li = lax.broadcasted_iota(jnp.int32, (LANES, LANES), 0)
    lj = lax.broadcasted_iota(jnp.int32, (LANES, LANES), 1)
    pair_bd = ((li < hd) == (lj < hd)).astype(F32)
    lane = lax.broadcasted_iota(jnp.int32, (1, LANES), 1)
    m0 = (lane < hd).astype(F32)
    m1 = (lane >= hd).astype(F32)

    def head_sum(t):
        return jnp.concatenate([_dot_sel(t[:, p * LANES:(p + 1) * LANES], pair_bd, 2) for p in range(w // LANES)],
                               axis=1)

    def by_head(t):
        return jnp.concatenate([t * m0, t * m1], axis=0)

    yield
    kkr = k * kk_ref[...]
    kk = kkr / jnp.maximum(jnp.sqrt(head_sum(kkr * kkr)), 1e-12)
    k = k * (1.0 + (a - 1.0) * ka_ref[...])

    ti = lax.broadcasted_iota(jnp.int32, (2 * c, 2 * c), 0)
    tj = lax.broadcasted_iota(jnp.int32, (2 * c, 2 * c), 1)
    same_head = (ti // c) == (tj // c)
    incl = (same_head & (tj <= ti)).astype(F32)
    strict = (same_head & (tj < ti)).astype(F32)
    eye = (tj == ti).astype(F32)
    blk16 = ((ti // 16) == (tj // 16)).astype(F32)
    lvl1 = (((ti // 32) == (tj // 32)) & ((ti // 16) != (tj // 16))).astype(F32)
    lvl2 = (same_head & ((ti // 32) != (tj // 32))).astype(F32)

    ri = lax.broadcasted_iota(jnp.int32, (rows, rows), 0)
    rj = lax.broadcasted_iota(jnp.int32, (rows, rows), 1)
    chunk_incl = (((ri // c) == (rj // c)) & (rj <= ri)).astype(F32)
    lb = _sel_dot(chunk_incl, logw, 3)
    e_neg = jnp.exp(-lb)
    alpha_t = -kk * jnp.exp(lb - logw)
    beta = kk * a
    beta_h = beta * e_neg
    k_h = k * e_neg
    r_t = r * jnp.exp(lb)

    n_ch = rows // c
    n_pair = w // LANES

    def slabs(t):
        return [t[ci * c:(ci + 1) * c, p * LANES:(p + 1) * LANES] for ci in range(n_ch) for p in range(n_pair)]

    def stack_by_head(t):
        return jnp.stack([by_head(s) for s in slabs(t)])

    yield
    xa2, v2 = stack_by_head(alpha_t), stack_by_head(v)
    gram = _bmm_nt(jnp.concatenate([xa2, stack_by_head(r_t)], axis=1),
                   jnp.concatenate([stack_by_head(beta_h), stack_by_head(k_h)], axis=1))
    a_ab = gram[:, 0:2 * c, 0:2 * c] * strict
    a_ak = gram[:, 0:2 * c, 2 * c:4 * c] * strict
    a_r = jnp.concatenate([gram[:, 2 * c:4 * c, 2 * c:4 * c] * incl, gram[:, 2 * c:4 * c, 0:2 * c] * incl], axis=2)
    yield
    t = _tri_inverse(a_ab, eye, blk16, lvl1, lvl2)
    yield
    wu = _bmm(t, jnp.concatenate([xa2, _bmm(a_ak, v2)], axis=2))
    wt = wu[:, 0:c, 0:LANES] + wu[:, c:2 * c, 0:LANES]
    ut = wu[:, 0:c, LANES:2 * LANES] + wu[:, c:2 * c, LANES:2 * LANES]
    xr = jnp.stack(slabs(r_t))
    vp = jnp.stack(slabs(v))

    yield
    st = state_ref[...]
    y_chunks = []
    for ci in range(n_ch):
        bs = slice(ci * n_pair, (ci + 1) * n_pair)
        lb_last = lb[(ci + 1) * c - 1:(ci + 1) * c, :]
        e_end = jnp.exp(lb_last - lb[ci * c:(ci + 1) * c])
        k_e = k[ci * c:(ci + 1) * c] * e_end
        beta_e = beta[ci * c:(ci + 1) * c] * e_end
        gamma_c = jnp.exp(lb_last)
        u = _bmm_nt(wt[bs], st) + ut[bs]
        u2 = jnp.concatenate([u * m0, u * m1], axis=1)
        y2 = _bmm(a_r[bs], jnp.concatenate([v2[bs], u2], axis=1))
        y = _bmm_nt(xr[bs], st) + y2[:, 0:c] + y2[:, c:2 * c]
        y_chunks.append(jnp.concatenate([y[p] for p in range(n_pair)], axis=1))
        vu_t = jnp.stack([jnp.concatenate([vp[ci * n_pair + p], u[p]], axis=0).T for p in range(n_pair)])
        ke_be = jnp.stack([jnp.concatenate([k_e[:, p * LANES:(p + 1) * LANES], beta_e[:, p * LANES:(p + 1) * LANES]],
                                           axis=0) for p in range(n_pair)])
        decay = jnp.stack([gamma_c[:, p * LANES:(p + 1) * LANES] for p in range(n_pair)])
        st = st * decay + pair_bd * _bmm(vu_t, ke_be)
        yield
    state_ref[...] = st
    y = jnp.concatenate(y_chunks, axis=0)

    inv_n = 1.0 / hd
    mean = head_sum(y) * inv_n
    yc = y - mean
    var = head_sum(yc * yc) * inv_n
    yn = yc * lax.rsqrt(var + A_GN_EPS) * gng_ref[...] + gnb_ref[...]
    bonus = head_sum(r * k * rk_ref[...]) * v
    y_ref[...] = (yn + bonus) * g


def _mla_prep_body(x_ref, cq_ref, sq_ref, ck_ref, sk_ref, qg_ref, wq_ref, wqs_ref, kg_ref, wk_ref, wv_ref,
                   q_ref, k_ref, vt_ref):
    x = x_ref[...]
    cq = x[:, 0:B_Q_RANK]
    ckv = x[:, B_Q_RANK:B_Q_RANK + B_KV_RANK]
    kr = x[:, B_Q_RANK + B_KV_RANK:B_Q_RANK + B_KV_RANK + LANES]
    kr_sw = x[:, B_Q_RANK + B_KV_RANK + LANES:B_Q_RANK + B_KV_RANK + 2 * LANES]
    cqn = _rms(cq, qg_ref[...]).astype(BF16)
    ckn = _rms(ckv, kg_ref[...]).astype(BF16)
    q = jnp.dot(cqn, wq_ref[...], preferred_element_type=F32)
    q_sw = jnp.dot(cqn, wqs_ref[...], preferred_element_type=F32)
    k_nope = jnp.dot(ckn, wk_ref[...], preferred_element_type=F32)
    val = jnp.dot(ckn, wv_ref[...], preferred_element_type=F32)
    k_rope = kr * ck_ref[...] + kr_sw * sk_ref[...]
    cq_t, sq_t = cq_ref[...], sq_ref[...]
    ones_col = (lax.broadcasted_iota(jnp.int32, (1, LANES), 1) == B_V).astype(F32)
    for hd in range(B_HEADS):
        sl = slice(hd * LANES, (hd + 1) * LANES)
        q_ref[hd] = (q[:, sl] * cq_t + q_sw[:, sl] * sq_t).astype(BF16)
        k_ref[hd] = (k_nope[:, sl] + k_rope).astype(BF16)
        vt_ref[hd] = (val[:, sl] + ones_col).T.astype(BF16)


def _mla_prep(x, tabs, p):
    s = x.shape[0]
    tm = ROW_TILE
    row = lambda n: pl.BlockSpec((tm, n), lambda i: (i, 0))
    const = lambda a: pl.BlockSpec(a.shape, lambda i: (0,) * a.ndim)
    consts = [p["q_norm_g"], p["w_q"], p["w_q_swap"], p["kv_norm_g"], p["w_k"], p["w_v"]]
    head_spec = pl.BlockSpec((B_HEADS, tm, LANES), lambda i: (0, i, 0))
    head_shape = jax.ShapeDtypeStruct((B_HEADS, s, LANES), BF16)
    head_t_spec = pl.BlockSpec((B_HEADS, LANES, tm), lambda i: (0, 0, i))
    head_t_shape = jax.ShapeDtypeStruct((B_HEADS, LANES, s), BF16)
    return pl.pallas_call(
        _mla_prep_body, grid=(s // tm,),
        in_specs=[row(B_IN)] + [row(LANES)] * 4 + [const(a) for a in consts],
        out_specs=[head_spec, head_spec, head_t_spec], out_shape=[head_shape, head_shape, head_t_shape],
        compiler_params=_params("arbitrary"), name="mla_prep",
    )(x, *tabs, *consts)


def _flash_body(q_ref, qn_ref, k_ref, vt_ref, o_ref, m_ref, acc_ref, s0_ref, s1_ref, s2_ref):
    i = pl.program_id(1)
    tile = q_ref.shape[1]
    sub = tile // B_SUBTILES
    m_ref[...] = jnp.full_like(m_ref, -jnp.inf)
    acc_ref[...] = jnp.zeros_like(acc_ref)

    def scores(queries_ref, j, s_ref, a):
        start = pl.multiple_of(j * tile, tile)
        s_ref[:, a * sub:(a + 1) * sub] = lax.dot_general(
            k_ref[0, pl.ds(start, tile), :], queries_ref[0, a * sub:(a + 1) * sub, :], (((1,), (1,)), ((), ())),
            preferred_element_type=F32)

    def softmax_pv(j, s_ref, a, on_diagonal):
        start = pl.multiple_of(j * tile, tile)
        cs = slice(a * sub, (a + 1) * sub)
        n_keys = (a + 1) * sub if on_diagonal else tile
        s = s_ref[0:n_keys, cs]
        if on_diagonal:
            keys = lax.broadcasted_iota(jnp.int32, s.shape, 0)
            queries = lax.broadcasted_iota(jnp.int32, s.shape, 1) + a * sub
            s = jnp.where(keys <= queries, s, -jnp.inf)
        m_old = m_ref[:, cs]
        m_new = jnp.maximum(m_old, jnp.max(s, axis=0, keepdims=True))
        p = jnp.exp2(s - m_new).astype(BF16)
        pv = jnp.dot(vt_ref[0, :, pl.ds(start, n_keys)], p, preferred_element_type=F32)
        acc_ref[:, cs] = jnp.exp2(m_old - m_new) * acc_ref[:, cs] + pv
        m_ref[:, cs] = m_new

    def phase(j, cur_ref, nxt_ref):
        for a in range(B_SUBTILES):
            scores(q_ref, j + 1, nxt_ref, a)
            softmax_pv(j, cur_ref, a, False)

    def diagonal(cur_ref, overlap_next):
        for a in range(B_SUBTILES):
            if overlap_next:
                scores(qn_ref, 0, s2_ref, a)
            softmax_pv(i, cur_ref, a, True)
        acc = acc_ref[...]
        o_ref[...] = (acc / acc[B_V:B_V + 1, :]).T

    @pl.when(i == 0)
    def _():
        for a in range(B_SUBTILES):
            scores(q_ref, 0, s2_ref, a)
        diagonal(s2_ref, False)
        for a in range(B_SUBTILES):
            scores(qn_ref, 0, s2_ref, a)

    @pl.when(i > 0)
    def _():
        phase(0, s2_ref, s0_ref)

        def two_tiles(jj, carry):
            phase(2 * jj + 1, s0_ref, s1_ref)
            phase(2 * jj + 2, s1_ref, s0_ref)
            return carry

        lax.fori_loop(0, (i - 1) // 2, two_tiles, 0)

        @pl.when(i % 2 == 0)
        def _():
            phase(i - 1, s0_ref, s1_ref)
            diagonal(s1_ref, True)

        @pl.when(i % 2 == 1)
        def _():
            diagonal(s0_ref, True)


def _flash(q, k, vt, tile):
    n_h, s, _ = q.shape
    n_q = s // tile
    scores = pltpu.VMEM((tile, tile), F32)
    return pl.pallas_call(
        _flash_body, grid=(n_h, n_q),
        in_specs=[pl.BlockSpec((1, tile, LANES), lambda h, i: (h, i, 0)),
                  pl.BlockSpec((1, tile, LANES), lambda h, i: (h, jnp.minimum(i + 1, n_q - 1), 0)),
                  pl.BlockSpec((1, s, LANES), lambda h, i: (h, 0, 0)),
                  pl.BlockSpec((1, LANES, s), lambda h, i: (h, 0, 0))],
        out_specs=pl.BlockSpec((tile, LANES), lambda h, i: (i, h)),
        out_shape=jax.ShapeDtypeStruct((s, n_h * LANES), F32),
        scratch_shapes=[pltpu.VMEM((1, tile), F32), pltpu.VMEM((LANES, tile), F32), scores, scores, scores],
        compiler_params=_params("arbitrary", "arbitrary"), name="mla_flash",
    )(q, q, k, vt)


def _hgrn_steps(x_ref, lb_ref, ng_ref, o_ref, state_ref):
    c = C_CHUNK
    w = C_WIDTH
    x = x_ref[...]
    rows = x.shape[0]
    n_ch = rows // c
    lower = lb_ref[...]
    fz = x[:, w:2 * w]
    f = lower + (1.0 - lower) * _sigmoid(fz)
    logf = jnp.log(jnp.maximum(f, C_MIN_FORGET))
    kf = (1.0 - lower) * _sigmoid(-fz)
    cq = x[:, 0:w]
    q = cq * _sigmoid(cq)
    val = x[:, 2 * w:3 * w]
    cg = x[:, 3 * w:4 * w]

    def heads(t):
        return jnp.stack([t[ci * c:(ci + 1) * c, hd * C_DIM:(hd + 1) * C_DIM]
                          for ci in range(n_ch) for hd in range(C_HEADS)])

    ri = lax.broadcasted_iota(jnp.int32, (rows, rows), 0)
    rj = lax.broadcasted_iota(jnp.int32, (rows, rows), 1)
    row = lax.broadcasted_iota(jnp.int32, (rows, 1), 0)
    ti = lax.broadcasted_iota(jnp.int32, (c, c), 0)
    tj = lax.broadcasted_iota(jnp.int32, (c, c), 1)
    b = _sel_dot((((ri // c) == (rj // c)) & (rj <= ri)).astype(F32), logf, 3)

    def block_row(size, offset):
        return jnp.concatenate([jnp.broadcast_to(b[r + offset:r + offset + 1, :], (size, w))
                                for r in range(0, rows, size)], axis=0)

    b_last = block_row(c, c - 1)

    yield
    n0 = C_SUB
    ref = block_row(n0, n0 // 2 - 1)
    mask = ((ti // n0) == (tj // n0)) & (tj <= ti)
    attn = jnp.where(mask, _bmm_nt(heads(q * jnp.exp(b - ref)), heads(kf * jnp.exp(ref - b))), 0.0)
    yield
    n = n0
    while n < c:
        ref = block_row(2 * n, n - 1)
        right = ((row // n) % 2) == 1
        q_s = q * jnp.exp(jnp.where(right, b - ref, 0.0))
        k_s = kf * jnp.exp(jnp.where(right, 0.0, ref - b))
        mask = ((ti // (2 * n)) == (tj // (2 * n))) & (((ti // n) % 2) == 1) & (((tj // n) % 2) == 0)
        attn = jnp.where(mask, _bmm_nt(heads(q_s), heads(k_s)), attn)
        n *= 2
        yield

    vals = heads(val)
    intra = _bmm(attn, vals)
    q_in = heads(q * jnp.exp(b))
    k_end = heads(kf * jnp.exp(b_last - b))
    g_end = jnp.exp(b_last)
    gates = cg * _sigmoid(cg)
    st = state_ref[...]
    out_chunks = []
    for ci in range(n_ch):
        bs = slice(ci * C_HEADS, (ci + 1) * C_HEADS)
        o = intra[bs] + _bmm_nt(q_in[bs], st)
        vt = jnp.stack([vals[ci * C_HEADS + hd].T for hd in range(C_HEADS)])
        decay = jnp.stack([g_end[(ci + 1) * c - 1:(ci + 1) * c, hd * C_DIM:(hd + 1) * C_DIM] for hd in range(C_HEADS)])
        st = st * decay + _bmm(vt, k_end[bs])
        out_chunks.append(jnp.concatenate([_rms(o[hd], ng_ref[...]) for hd in range(C_HEADS)], axis=1))
        yield
    state_ref[...] = st
    o_ref[...] = jnp.concatenate(out_chunks, axis=0) * gates


def _recurrent_body(has_vres, n_rwkv_in, *refs):
    n_rwkv_out = 1 if has_vres else 2
    rwkv_in, rest = refs[:n_rwkv_in], refs[n_rwkv_in:]
    hgrn_in, rest = rest[:3], rest[3:]
    rwkv_out, hgrn_out, (rwkv_state, hgrn_state) = rest[:n_rwkv_out], rest[n_rwkv_out], rest[n_rwkv_out + 1:]

    @pl.when(pl.program_id(0) == 0)
    def _():
        rwkv_state[...] = jnp.zeros_like(rwkv_state)
        hgrn_state[...] = jnp.zeros_like(hgrn_state)

    rwkv = _rwkv_steps(has_vres, *rwkv_in, *rwkv_out, rwkv_state)
    hgrn = _hgrn_steps(*hgrn_in, hgrn_out, hgrn_state)
    pending = [rwkv, hgrn]
    while pending:
        for steps in list(pending):
            if next(steps, StopIteration) is StopIteration:
                pending.remove(steps)


def _recurrent_mixers(x_a, x_c, p, v_first, lower, norm_g):
    s = x_a.shape[0]
    rows = A_CHUNK * A_CHUNKS_PER_STEP
    assert rows == C_CHUNK * C_CHUNKS_PER_STEP
    has_vres = v_first is not None
    row = lambda n: pl.BlockSpec((rows, n), lambda i: (i, 0))
    const = lambda a: pl.BlockSpec(a.shape, lambda i: (0,) * a.ndim)
    names = ["w0", "w_up", "a0", "a_up", "g_up", "k_k", "k_a", "r_k", "gn_g", "gn_b"]
    if has_vres:
        names += ["v0", "vres_up"]
    consts = [p[n] for n in names]
    rwkv_ins = [x_a] + ([v_first] if has_vres else []) + consts
    rwkv_specs = [row(A_IN)] + ([row(A_WIDTH)] if has_vres else []) + [const(a) for a in consts]
    n_y = 1 if has_vres else 2
    y_shape = jax.ShapeDtypeStruct((s, A_WIDTH), F32)
    out = pl.pallas_call(
        functools.partial(_recurrent_body, has_vres, len(rwkv_ins)),
        grid=(s // rows,),
        in_specs=rwkv_specs + [row(C_IN), const(lower), const(norm_g)],
        out_specs=[row(A_WIDTH)] * n_y + [row(C_WIDTH)],
        out_shape=[y_shape] * n_y + [jax.ShapeDtypeStruct((s, C_WIDTH), F32)],
        scratch_shapes=[pltpu.VMEM((A_WIDTH // LANES, LANES, LANES), F32), pltpu.VMEM((C_HEADS, C_DIM, C_DIM), F32)],
        compiler_params=_params("arbitrary"), name="rwkv7_hgrn2",
    )(*rwkv_ins, x_c, lower, norm_g)
    if has_vres:
        return out[0], v_first, out[1]
    return out[0], out[1], out[2]


def _merge_body(h_ref, ya_ref, yb_ref, yc_ref, g_ref, wg_ref, wa_ref, wb_ref, wc_ref, wo_ref, pg_ref, o_ref):
    d = D_MODEL
    h = h_ref[...]
    u = _rms(h, g_ref[...]).astype(BF16)
    merged = None
    for n, (y_ref, w_ref) in enumerate(((ya_ref, wa_ref), (yb_ref, wb_ref), (yc_ref, wc_ref))):
        gate = _sigmoid(jnp.dot(u, wg_ref[:, n * d:(n + 1) * d], preferred_element_type=F32))
        term = gate * _bdot(y_ref[...], w_ref[...])
        merged = term if merged is None else merged + term
    o_ref[...] = h + _rms(_bdot(merged, wo_ref[...]), pg_ref[...])


def _merge(h, ya, yb, yc, pre_g, wg, wa, wb, wc, wo, post_g):
    s = h.shape[0]
    tm = ROW_TILE
    row = lambda n: pl.BlockSpec((tm, n), lambda i: (i, 0))
    const = lambda a: pl.BlockSpec(a.shape, lambda i: (0,) * a.ndim)
    consts = [pre_g, wg, wa, wb, wc, wo, post_g]
    return pl.pallas_call(
        _merge_body, grid=(s // tm,),
        in_specs=[row(D_MODEL), row(A_WIDTH), row(B_HEADS * LANES), row(C_WIDTH)] + [const(a) for a in consts],
        out_specs=row(D_MODEL), out_shape=jax.ShapeDtypeStruct((s, D_MODEL), F32),
        compiler_params=_params("arbitrary"), name="merge",
    )(h, ya, yb, yc, *consts)


def _pad_cols(a, n):
    return jnp.pad(a, ((0, 0), (0, n - a.shape[1])))


def _pad_rows(a, n):
    return jnp.pad(a, ((0, n - a.shape[0]), (0, 0)))


def _rope_half_swap(a):
    half = B_ROPE // 2
    return jnp.concatenate([a[:, half:], a[:, :half]], axis=1)


def _layer_weights(l, w_in, rwkv_mu, vres_down, vres_mu, w_uq, w_ukv, mla_out):
    wi = w_in[l]
    d = D_MODEL
    o = 3 * A_WIDTH
    lora = [wi[:, o:o + 64], wi[:, o + 64:o + 128], wi[:, o + 128:o + 256]]
    mus = [rwkv_mu[l][o:o + 64], rwkv_mu[l][o + 64:o + 128], rwkv_mu[l][o + 128:o + 256]]
    if l > 0:
        lora.append(vres_down[l - 1])
        mus.append(vres_mu[l - 1])
    else:
        lora.append(jnp.zeros((d, 0), F32))
        mus.append(jnp.zeros((0,), F32))
    w_a = jnp.concatenate([wi[:, :o]] + [_pad_cols(t, LANES) for t in lora], axis=1)
    mu_a = jnp.concatenate([rwkv_mu[l][:o]] + [jnp.pad(t, (0, LANES - t.shape[0])) for t in mus])[None, :]

    o = A_COLS
    w_kr = wi[:, o + B_Q_RANK + B_KV_RANK:o + B_Q_RANK + B_KV_RANK + B_ROPE]
    place = lambda t: jnp.pad(t, ((0, 0), (B_NOPE, LANES - B_NOPE - B_ROPE)))
    w_b = jnp.concatenate([wi[:, o:o + B_Q_RANK + B_KV_RANK], place(w_kr), place(_rope_half_swap(w_kr))], axis=1)
    o += B_Q_RANK + B_KV_RANK + B_ROPE
    w_c = wi[:, o:o + C_IN]
    w_g = wi[:, o + C_IN:o + C_IN + N_BRANCH * d]

    scale = (B_NOPE + B_ROPE) ** -0.5 * np.log2(np.e)
    uq = (w_uq[l] * scale).reshape(B_Q_RANK, B_HEADS, B_NOPE + B_ROPE)
    uq_sw = jnp.concatenate([jnp.zeros_like(uq[..., :B_NOPE]), uq[..., B_NOPE + B_ROPE // 2:],
                             uq[..., B_NOPE:B_NOPE + B_ROPE // 2]], axis=-1)
    pad_head = lambda t: jnp.pad(t, ((0, 0), (0, 0), (0, LANES - t.shape[-1]))).reshape(t.shape[0], B_HEADS * LANES)
    ukv = w_ukv[l].reshape(B_KV_RANK, B_HEADS, B_NOPE + B_V)
    mo = jnp.pad(mla_out[l].reshape(B_HEADS, B_V, d), ((0, 0), (0, LANES - B_V), (0, 0))).reshape(B_HEADS * LANES, d)
    bf = lambda t: t.astype(BF16)
    return dict(w_a=bf(w_a), mu_a=mu_a, w_b=bf(w_b), w_c=bf(w_c), w_g=bf(w_g),
                w_q=bf(pad_head(uq)), w_q_swap=bf(pad_head(uq_sw)),
                w_k=bf(pad_head(ukv[..., :B_NOPE])), w_v=bf(pad_head(ukv[..., B_NOPE:])), mla_out=bf(mo))


def _rope_tables(positions):
    inv_freq = ROPE_THETA ** (-jnp.arange(0, B_ROPE, 2, dtype=F32) / B_ROPE)
    ang = positions.astype(F32)[:, None] * inv_freq
    cos, sin = jnp.cos(ang), jnp.sin(ang)
    s = positions.shape[0]
    pad = jnp.zeros((s, LANES - B_NOPE - B_ROPE), F32)
    cos_q = jnp.concatenate([jnp.ones((s, B_NOPE), F32), cos, cos, pad], axis=1)
    sin_t = jnp.concatenate([jnp.zeros((s, B_NOPE), F32), -sin, sin, pad], axis=1)
    cos_k = jnp.concatenate([jnp.zeros((s, B_NOPE), F32), cos, cos, pad], axis=1)
    return cos_q, sin_t, cos_k, sin_t


def kernel(x, positions, ffn1_pre_g, ffn1_post_g, ffn1_w_gate, ffn1_w_up, ffn1_w_down, mix_pre_g, mix_post_g, w_in, rwkv_mu, rwkv_w0, rwkv_w_up, rwkv_a0, rwkv_a_up, rwkv_g_up, rwkv_k_k, rwkv_k_a, rwkv_r_k, rwkv_gn_g, rwkv_gn_b, rwkv_vres_down, rwkv_vres_mu, rwkv_vres_up, rwkv_v0, rwkv_out, mla_q_norm_g, mla_w_uq, mla_kv_norm_g, mla_w_ukv, mla_out, hgrn_lower_bounds, hgrn_norm_g, hgrn_out, w_o, ffn2_pre_g, ffn2_post_g, ffn2_w_gate, ffn2_w_up, ffn2_w_down):
    bsz, seq, d = x.shape
    assert bsz == 1 and d == D_MODEL and seq % max(ROW_TILE, C_CHUNK * C_CHUNKS_PER_STEP, A_CHUNK * A_CHUNKS_PER_STEP) == 0
    depth = w_in.shape[0]
    flash_tile = min(1024, seq)
    tabs = _rope_tables(positions[0])
    lb_p = jax.nn.softmax(hgrn_lower_bounds.astype(F32), axis=0)
    lower_bounds = jnp.cumsum(lb_p, axis=0) - lb_p[0]
    row = lambda t: t[None, :]
    bf = lambda t: t.astype(BF16)

    h = x[0]
    v_first = None
    for l in range(depth):
        h = _ffn(h, row(ffn1_pre_g[l]), bf(ffn1_w_gate[l]), bf(ffn1_w_up[l]), bf(ffn1_w_down[l]),
                 row(ffn1_post_g[l]))

        lw = _layer_weights(l, w_in, rwkv_mu, rwkv_vres_down, rwkv_vres_mu, mla_w_uq, mla_w_ukv, mla_out)
        pre_g = row(mix_pre_g[l])
        x_a = _proj(h, pre_g, lw["w_a"], lw["mu_a"])
        x_b = _proj(h, pre_g, lw["w_b"])
        x_c = _proj(h, pre_g, lw["w_c"])

        pa = dict(w0=row(rwkv_w0[l]), w_up=bf(_pad_rows(rwkv_w_up[l], LANES)), a0=row(rwkv_a0[l]),
                  a_up=bf(_pad_rows(rwkv_a_up[l], LANES)), g_up=bf(rwkv_g_up[l]), k_k=row(rwkv_k_k[l]),
                  k_a=row(rwkv_k_a[l]), r_k=row(rwkv_r_k[l].reshape(-1)), gn_g=row(rwkv_gn_g[l]),
                  gn_b=row(rwkv_gn_b[l]))
        if l > 0:
            pa.update(v0=row(rwkv_v0[l - 1]), vres_up=bf(_pad_rows(rwkv_vres_up[l - 1], LANES)))
        y_a, v_first, y_c = _recurrent_mixers(x_a, x_c, pa, v_first, row(lower_bounds[l]), row(hgrn_norm_g[l]))

        pb = dict(q_norm_g=row(mla_q_norm_g[l]), kv_norm_g=row(mla_kv_norm_g[l]), w_q=lw["w_q"],
                  w_q_swap=lw["w_q_swap"], w_k=lw["w_k"], w_v=lw["w_v"])
        q_h, k_h, vt_h = _mla_prep(x_b, tabs, pb)
        y_b = _flash(q_h, k_h, vt_h, flash_tile)

        h = _merge(h, y_a, y_b, y_c, pre_g, lw["w_g"], bf(rwkv_out[l]), lw["mla_out"], bf(hgrn_out[l]),
                   bf(w_o[l]), row(mix_post_g[l]))

        h = _ffn(h, row(ffn2_pre_g[l]), bf(ffn2_w_gate[l]), bf(ffn2_w_up[l]), bf(ffn2_w_down[l]),
                 row(ffn2_post_g[l]))
    return h[None]
```

```python
import functools

import jax
import jax.numpy as jnp
import numpy as np
from jax import lax
from jax.experimental import pallas as pl
from jax.experimental.pallas import tpu as pltpu

F32 = jnp.float32
BF16 = jnp.bfloat16

D_MODEL = 1024
D_FF = 2816
NORM_EPS = 1e-6
MACARON_WEIGHT = 0.5

A_HEADS = 8
A_HEAD_DIM = 64
A_WIDTH = 512
A_GN_EPS = 64e-5
A_CHUNK = 64
A_CHUNKS_PER_STEP = 4
A_COLS = 1792
A_IN = 2048

B_HEADS = 8
B_NOPE = 64
B_ROPE = 32
B_V = 64
B_Q_RANK = 384
B_KV_RANK = 256
B_IN = 896
ROPE_THETA = 10000.0
B_SUBTILES = 4
LANES = 128

C_HEADS = 4
C_DIM = 128
C_WIDTH = 512
C_IN = 2048
C_CHUNK = 128
C_CHUNKS_PER_STEP = 2
C_SUB = 8
C_MIN_FORGET = 1e-6

N_BRANCH = 3
ROW_TILE = 512
FFN_ROW_TILE = 1024
VMEM_LIMIT = 56 * 1024 * 1024

assert C_SUB / 2 * -np.log(C_MIN_FORGET) < 80.0


def _bdot(a, b):
    return jnp.dot(a.astype(BF16), b.astype(BF16), preferred_element_type=F32)


def _bdot_nt(a, b):
    return lax.dot_general(a.astype(BF16), b.astype(BF16), (((1,), (1,)), ((), ())),
                           preferred_element_type=F32)


def _bmm(a, b):
    return lax.dot_general(a.astype(BF16), b.astype(BF16), (((2,), (1,)), ((0,), (0,))),
                           preferred_element_type=F32)


def _bmm_nt(a, b):
    return lax.dot_general(a.astype(BF16), b.astype(BF16), (((2,), (2,)), ((0,), (0,))),
                           preferred_element_type=F32)


def _split_terms(data, parts):
    terms, rem = [], data
    for _ in range(parts):
        piece = rem.astype(BF16)
        terms.append(piece)
        rem = rem - piece.astype(F32)
    return terms


def _sel_dot(sel, data, parts):
    sel = sel.astype(BF16)
    return sum(jnp.dot(sel, t, preferred_element_type=F32) for t in _split_terms(data, parts))


def _dot_sel(data, sel, parts):
    sel = sel.astype(BF16)
    return sum(jnp.dot(t, sel, preferred_element_type=F32) for t in _split_terms(data, parts))


def _rms(x, g):
    return x * lax.rsqrt(jnp.mean(x * x, axis=-1, keepdims=True) + NORM_EPS) * g


def _sigmoid(x):
    return jax.nn.sigmoid(x)


def _params(*sem):
    return pltpu.CompilerParams(dimension_semantics=sem, vmem_limit_bytes=VMEM_LIMIT)


def _ffn_body(x_ref, pre_g_ref, wg_ref, wu_ref, wd_ref, post_g_ref, o_ref):
    x = x_ref[...]
    xn = _rms(x, pre_g_ref[...]).astype(BF16)
    gate = jnp.dot(xn, wg_ref[...], preferred_element_type=F32)
    up = jnp.dot(xn, wu_ref[...], preferred_element_type=F32)
    mid = (gate * _sigmoid(gate) * up).astype(BF16)
    y = jnp.dot(mid, wd_ref[...], preferred_element_type=F32)
    o_ref[...] = x + MACARON_WEIGHT * _rms(y, post_g_ref[...])


def _ffn(h, pre_g, wg, wu, wd, post_g):
    s = h.shape[0]
    tm = FFN_ROW_TILE
    const = lambda a: pl.BlockSpec(a.shape, lambda i: (0,) * a.ndim)
    return pl.pallas_call(
        _ffn_body,
        grid=(s // tm,),
        in_specs=[pl.BlockSpec((tm, D_MODEL), lambda i: (i, 0)), const(pre_g), const(wg), const(wu), const(wd),
                  const(post_g)],
        out_specs=pl.BlockSpec((tm, D_MODEL), lambda i: (i, 0)),
        out_shape=jax.ShapeDtypeStruct((s, D_MODEL), F32),
        compiler_params=_params("arbitrary"),
        name="ffn",
    )(h, pre_g, wg, wu, wd, post_g)


def _proj_body(x_ref, g_ref, w_ref, o_ref):
    xn = _rms(x_ref[...], g_ref[...]).astype(BF16)
    o_ref[...] = jnp.dot(xn, w_ref[...], preferred_element_type=F32)


def _proj_shift_body(x_ref, g_ref, w_ref, mu_ref, o_ref, carry_ref):
    @pl.when(pl.program_id(0) == 0)
    def _():
        carry_ref[...] = jnp.zeros_like(carry_ref)

    xn = _rms(x_ref[...], g_ref[...]).astype(BF16)
    p = jnp.dot(xn, w_ref[...], preferred_element_type=F32)
    rows = p.shape[0]
    prev = pltpu.roll(p, 1, axis=0)
    first = lax.broadcasted_iota(jnp.int32, p.shape, 0) == 0
    prev = jnp.where(first, carry_ref[0:1, :], prev)
    carry_ref[0:1, :] = p[rows - 1:rows, :]
    o_ref[...] = p + (prev - p) * mu_ref[...]


def _proj(h, pre_g, w, mu=None):
    s = h.shape[0]
    n = w.shape[1]
    x_spec = pl.BlockSpec((ROW_TILE, D_MODEL), lambda i: (i, 0))
    g_spec = pl.BlockSpec((1, D_MODEL), lambda i: (0, 0))
    w_spec = pl.BlockSpec((D_MODEL, n), lambda i: (0, 0))
    o_spec = pl.BlockSpec((ROW_TILE, n), lambda i: (i, 0))
    out_shape = jax.ShapeDtypeStruct((s, n), F32)
    if mu is None:
        return pl.pallas_call(
            _proj_body, grid=(s // ROW_TILE,), in_specs=[x_spec, g_spec, w_spec], out_specs=o_spec,
            out_shape=out_shape, compiler_params=_params("arbitrary"), name="proj",
        )(h, pre_g, w)
    return pl.pallas_call(
        _proj_shift_body, grid=(s // ROW_TILE,),
        in_specs=[x_spec, g_spec, w_spec, pl.BlockSpec((1, n), lambda i: (0, 0))], out_specs=o_spec,
        out_shape=out_shape, scratch_shapes=[pltpu.VMEM((8, n), F32)],
        compiler_params=_params("arbitrary"), name="proj_shift",
    )(h, pre_g, w, mu)


def _tri_inverse(a_strict, eye, blk16, lvl1, lvl2):
    d = a_strict * blk16
    d2 = _bmm(d, d)
    d4 = _bmm(d2, d2)
    d8 = _bmm(d4, d4)
    t = eye + d
    t = t + _bmm(t, d2)
    t = t + _bmm(t, d4)
    t = t + _bmm(t, d8)
    t = t + _bmm(_bmm(t, a_strict * lvl1), t)
    t = t + _bmm(_bmm(t, a_strict * lvl2), t)
    return t


def _rwkv_steps(has_vres, *refs):
    if has_vres:
        (x_ref, vf_ref, w0_ref, wup_ref, a0_ref, aup_ref, gup_ref, kk_ref, ka_ref, rk_ref, gng_ref, gnb_ref,
         v0_ref, vup_ref, y_ref, state_ref) = refs
    else:
        (x_ref, w0_ref, wup_ref, a0_ref, aup_ref, gup_ref, kk_ref, ka_ref, rk_ref, gng_ref, gnb_ref,
         y_ref, vf_out_ref, state_ref) = refs

    c = A_CHUNK
    w = A_WIDTH
    hd = A_HEAD_DIM
    x = x_ref[...]
    rows = x.shape[0]
    r = x[:, 0:w]
    k = x[:, w:2 * w]
    v = x[:, 2 * w:3 * w]
    wl = x[:, 3 * w:3 * w + LANES]
    al = x[:, 3 * w + LANES:3 * w + 2 * LANES]
    gl = x[:, 3 * w + 2 * LANES:3 * w + 3 * LANES]

    z = w0_ref[...] + _bdot(jnp.tanh(wl), wup_ref[...])
    softplus_neg = jnp.maximum(-z, 0.0) + jnp.log(1.0 + jnp.exp(-jnp.abs(z)))
    logw = -jnp.exp(-softplus_neg - 0.5)
    a = _sigmoid(a0_ref[...] + _bdot(al, aup_ref[...]))
    g = _bdot(_sigmoid(gl), gup_ref[...])
    if has_vres:
        vl = x[:, 3 * w + 3 * LANES:3 * w + 4 * LANES]
        v = v + (vf_ref[...] - v) * _sigmoid(v0_ref[...] + _bdot(vl, vup_ref[...]))
    else:
        vf_out_ref[...] = v

    li = lax.broadcasted_iota(jnp.int32, (LANES, LANES), 0)
    lj = lax.broadcasted_iota(jnp.int32, (LANES, LANES), 1)
    pair_bd = ((li < hd) == (lj < hd)).astype(F32)
    lane = lax.broadcasted_iota(jnp.int32, (1, LANES), 1)
    m0 = (lane < hd).astype(F32)
    m1 = (lane >= hd).astype(F32)

    def head_sum(t):
        return jnp.concatenate([_dot_sel(t[:, p * LANES:(p + 1) * LANES], pair_bd, 1) for p in range(w // LANES)],
                               axis=1)

    def by_head(t):
        return jnp.concatenate([t * m0, t * m1], axis=0)

    yield
    kkr = k * kk_ref[...]
    kk = kkr / jnp.maximum(jnp.sqrt(head_sum(kkr * kkr)), 1e-12)
    k = k * (1.0 + (a - 1.0) * ka_ref[...])

    ti = lax.broadcasted_iota(jnp.int32, (2 * c, 2 * c), 0)
    tj = lax.broadcasted_iota(jnp.int32, (2 * c, 2 * c), 1)
    same_head = (ti // c) == (tj // c)
    incl = (same_head & (tj <= ti)).astype(F32)
    strict = (same_head & (tj < ti)).astype(F32)
    eye = (tj == ti).astype(F32)
    blk16 = ((ti // 16) == (tj // 16)).astype(F32)
    lvl1 = (((ti // 32) == (tj // 32)) & ((ti // 16) != (tj // 16))).astype(F32)
    lvl2 = (same_head & ((ti // 32) != (tj // 32))).astype(F32)

    ri = lax.broadcasted_iota(jnp.int32, (rows, rows), 0)
    rj = lax.broadcasted_iota(jnp.int32, (rows, rows), 1)
    chunk_incl = (((ri // c) == (rj // c)) & (rj <= ri)).astype(F32)
    lb = _sel_dot(chunk_incl, logw, 3)
    e_neg = jnp.exp(-lb)
    alpha_t = -kk * jnp.exp(lb - logw)
    beta = kk * a
    beta_h = beta * e_neg
    k_h = k * e_neg
    r_t = r * jnp.exp(lb)

    n_ch = rows // c
    n_pair = w // LANES

    def slabs(t):
        return [t[ci * c:(ci + 1) * c, p * LANES:(p + 1) * LANES] for ci in range(n_ch) for p in range(n_pair)]

    def stack_by_head(t):
        return jnp.stack([by_head(s) for s in slabs(t)])

    yield
    xa2, v2 = stack_by_head(alpha_t), stack_by_head(v)
    gram = _bmm_nt(jnp.concatenate([xa2, stack_by_head(r_t)], axis=1),
                   jnp.concatenate([stack_by_head(beta_h), stack_by_head(k_h)], axis=1))
    a_ab = gram[:, 0:2 * c, 0:2 * c] * strict
    a_ak = gram[:, 0:2 * c, 2 * c:4 * c] * strict
    a_r = jnp.concatenate([gram[:, 2 * c:4 * c, 2 * c:4 * c] * incl, gram[:, 2 * c:4 * c, 0:2 * c] * incl], axis=2)
    yield
    t = _tri_inverse(a_ab, eye, blk16, lvl1, lvl2)
    yield
    wu = _bmm(t, jnp.concatenate([xa2, _bmm(a_ak, v2)], axis=2))
    wt = wu[:, 0:c, 0:LANES] + wu[:, c:2 * c, 0:LANES]
    ut = wu[:, 0:c, LANES:2 * LANES] + wu[:, c:2 * c, LANES:2 * LANES]
    xr = jnp.stack(slabs(r_t))
    vp = jnp.stack(slabs(v))

    yield
    st = state_ref[...]
    y_chunks = []
    for ci in range(n_ch):
        bs = slice(ci * n_pair, (ci + 1) * n_pair)
        lb_last = lb[(ci + 1) * c - 1:(ci + 1) * c, :]
        e_end = jnp.exp(lb_last - lb[ci * c:(ci + 1) * c])
        k_e = k[ci * c:(ci + 1) * c] * e_end
        beta_e = beta[ci * c:(ci + 1) * c] * e_end
        gamma_c = jnp.exp(lb_last)
        u = _bmm_nt(wt[bs], st) + ut[bs]
        u2 = jnp.concatenate([u * m0, u * m1], axis=1)
        y2 = _bmm(a_r[bs], jnp.concatenate([v2[bs], u2], axis=1))
        y = _bmm_nt(xr[bs], st) + y2[:, 0:c] + y2[:, c:2 * c]
        y_chunks.append(jnp.concatenate([y[p] for p in range(n_pair)], axis=1))
        vu_t = jnp.stack([jnp.concatenate([vp[ci * n_pair + p], u[p]], axis=0).T for p in range(n_pair)])
        ke_be = jnp.stack([jnp.concatenate([k_e[:, p * LANES:(p + 1) * LANES], beta_e[:, p * LANES:(p + 1) * LANES]],
                                           axis=0) for p in range(n_pair)])
        decay = jnp.stack([gamma_c[:, p * LANES:(p + 1) * LANES] for p in range(n_pair)])
        st = st * decay + pair_bd * _bmm(vu_t, ke_be)
        yield
    state_ref[...] = st
    y = jnp.concatenate(y_chunks, axis=0)

    inv_n = 1.0 / hd
    mean = head_sum(y) * inv_n
    yc = y - mean
    var = head_sum(yc * yc) * inv_n
    yn = yc * lax.rsqrt(var + A_GN_EPS) * gng_ref[...] + gnb_ref[...]
    bonus = head_sum(r * k * rk_ref[...]) * v
    y_ref[...] = (yn + bonus) * g


def _mla_prep_body(x_ref, cq_ref, sq_ref, ck_ref, sk_ref, qg_ref, wq_ref, wqs_ref, kg_ref, wk_ref, wv_ref,
                   q_ref, k_ref, vt_ref):
    x = x_ref[...]
    cq = x[:, 0:B_Q_RANK]
    ckv = x[:, B_Q_RANK:B_Q_RANK + B_KV_RANK]
    kr = x[:, B_Q_RANK + B_KV_RANK:B_Q_RANK + B_KV_RANK + LANES]
    kr_sw = x[:, B_Q_RANK + B_KV_RANK + LANES:B_Q_RANK + B_KV_RANK + 2 * LANES]
    cqn = _rms(cq, qg_ref[...]).astype(BF16)
    ckn = _rms(ckv, kg_ref[...]).astype(BF16)
    q = jnp.dot(cqn, wq_ref[...], preferred_element_type=F32)
    q_sw = jnp.dot(cqn, wqs_ref[...], preferred_element_type=F32)
    k_nope = jnp.dot(ckn, wk_ref[...], preferred_element_type=F32)
    val = jnp.dot(ckn, wv_ref[...], preferred_element_type=F32)
    k_rope = kr * ck_ref[...] + kr_sw * sk_ref[...]
    cq_t, sq_t = cq_ref[...], sq_ref[...]
    ones_col = (lax.broadcasted_iota(jnp.int32, (1, LANES), 1) == B_V).astype(F32)
    for hd in range(B_HEADS):
        sl = slice(hd * LANES, (hd + 1) * LANES)
        q_ref[hd] = (q[:, sl] * cq_t + q_sw[:, sl] * sq_t).astype(BF16)
        k_ref[hd] = (k_nope[:, sl] + k_rope).astype(BF16)
        vt_ref[hd] = (val[:, sl] + ones_col).T.astype(BF16)


def _mla_prep(x, tabs, p):
    s = x.shape[0]
    tm = ROW_TILE
    row = lambda n: pl.BlockSpec((tm, n), lambda i: (i, 0))
    const = lambda a: pl.BlockSpec(a.shape, lambda i: (0,) * a.ndim)
    consts = [p["q_norm_g"], p["w_q"], p["w_q_swap"], p["kv_norm_g"], p["w_k"], p["w_v"]]
    head_spec = pl.BlockSpec((B_HEADS, tm, LANES), lambda i: (0, i, 0))
    head_shape = jax.ShapeDtypeStruct((B_HEADS, s, LANES), BF16)
    head_t_spec = pl.BlockSpec((B_HEADS, LANES, tm), lambda i: (0, 0, i))
    head_t_shape = jax.ShapeDtypeStruct((B_HEADS, LANES, s), BF16)
    return pl.pallas_call(
        _mla_prep_body, grid=(s // tm,),
        in_specs=[row(B_IN)] + [row(LANES)] * 4 + [const(a) for a in consts],
        out_specs=[head_spec, head_spec, head_t_spec], out_shape=[head_shape, head_shape, head_t_shape],
        compiler_params=_params("arbitrary"), name="mla_prep",
    )(x, *tabs, *consts)


def _flash_body(q_ref, qn_ref, k_ref, vt_ref, o_ref, m_ref, acc_ref, s0_ref, s1_ref, s2_ref):
    i = pl.program_id(1)
    tile = q_ref.shape[1]
    sub = tile // B_SUBTILES
    m_ref[...] = jnp.full_like(m_ref, -jnp.inf)
    acc_ref[...] = jnp.zeros_like(acc_ref)

    def scores(queries_ref, j, s_ref, a):
        start = pl.multiple_of(j * tile, tile)
        s_ref[:, a * sub:(a + 1) * sub] = lax.dot_general(
            k_ref[0, pl.ds(start, tile), :], queries_ref[0, a * sub:(a + 1) * sub, :], (((1,), (1,)), ((), ())),
            preferred_element_type=F32)

    def softmax_pv(j, s_ref, a, on_diagonal):
        start = pl.multiple_of(j * tile, tile)
        cs = slice(a * sub, (a + 1) * sub)
        n_keys = (a + 1) * sub if on_diagonal else tile
        s = s_ref[0:n_keys, cs]
        if on_diagonal:
            keys = lax.broadcasted_iota(jnp.int32, s.shape, 0)
            queries = lax.broadcasted_iota(jnp.int32, s.shape, 1) + a * sub
            s = jnp.where(keys <= queries, s, -jnp.inf)
        m_old = m_ref[:, cs]
        m_new = jnp.maximum(m_old, jnp.max(s, axis=0, keepdims=True))
        p = jnp.exp2(s - m_new).astype(BF16)
        pv = jnp.dot(vt_ref[0, :, pl.ds(start, n_keys)], p, preferred_element_type=F32)
        acc_ref[:, cs] = jnp.exp2(m_old - m_new) * acc_ref[:, cs] + pv
        m_ref[:, cs] = m_new

    def phase(j, cur_ref, nxt_ref):
        for a in range(B_SUBTILES):
            scores(q_ref, j + 1, nxt_ref, a)
            softmax_pv(j, cur_ref, a, False)

    def diagonal(cur_ref, overlap_next):
        for a in range(B_SUBTILES):
            if overlap_next:
                scores(qn_ref, 0, s2_ref, a)
            softmax_pv(i, cur_ref, a, True)
        acc = acc_ref[...]
        o_ref[...] = (acc / acc[B_V:B_V + 1, :]).T

    @pl.when(i == 0)
    def _():
        for a in range(B_SUBTILES):
            scores(q_ref, 0, s2_ref, a)
        diagonal(s2_ref, False)
        for a in range(B_SUBTILES):
            scores(qn_ref, 0, s2_ref, a)

    @pl.when(i > 0)
    def _():
        phase(0, s2_ref, s0_ref)

        def two_tiles(jj, carry):
            phase(2 * jj + 1, s0_ref, s1_ref)
            phase(2 * jj + 2, s1_ref, s0_ref)
            return carry

        lax.fori_loop(0, (i - 1) // 2, two_tiles, 0)

        @pl.when(i % 2 == 0)
        def _():
            phase(i - 1, s0_ref, s1_ref)
            diagonal(s1_ref, True)

        @pl.when(i % 2 == 1)
        def _():
            diagonal(s0_ref, True)


def _flash(q, k, vt, tile):
    n_h, s, _ = q.shape
    n_q = s // tile
    scores = pltpu.VMEM((tile, tile), F32)
    return pl.pallas_call(
        _flash_body, grid=(n_h, n_q),
        in_specs=[pl.BlockSpec((1, tile, LANES), lambda h, i: (h, i, 0)),
                  pl.BlockSpec((1, tile, LANES), lambda h, i: (h, jnp.minimum(i + 1, n_q - 1), 0)),
                  pl.BlockSpec((1, s, LANES), lambda h, i: (h, 0, 0)),
                  pl.BlockSpec((1, LANES, s), lambda h, i: (h, 0, 0))],
        out_specs=pl.BlockSpec((tile, LANES), lambda h, i: (i, h)),
        out_shape=jax.ShapeDtypeStruct((s, n_h * LANES), F32),
        scratch_shapes=[pltpu.VMEM((1, tile), F32), pltpu.VMEM((LANES, tile), F32), scores, scores, scores],
        compiler_params=_params("arbitrary", "arbitrary"), name="mla_flash",
    )(q, q, k, vt)


def _hgrn_steps(x_ref, lb_ref, ng_ref, o_ref, state_ref):
    c = C_CHUNK
    w = C_WIDTH
    x = x_ref[...]
    rows = x.shape[0]
    n_ch = rows // c
    lower = lb_ref[...]
    fz = x[:, w:2 * w]
    f = lower + (1.0 - lower) * _sigmoid(fz)
    logf = jnp.log(jnp.maximum(f, C_MIN_FORGET))
    kf = (1.0 - lower) * _sigmoid(-fz)
    cq = x[:, 0:w]
    q = cq * _sigmoid(cq)
    val = x[:, 2 * w:3 * w]
    cg = x[:, 3 * w:4 * w]

    def heads(t):
        return jnp.stack([t[ci * c:(ci + 1) * c, hd * C_DIM:(hd + 1) * C_DIM]
                          for ci in range(n_ch) for hd in range(C_HEADS)])

    ri = lax.broadcasted_iota(jnp.int32, (rows, rows), 0)
    rj = lax.broadcasted_iota(jnp.int32, (rows, rows), 1)
    row = lax.broadcasted_iota(jnp.int32, (rows, 1), 0)
    ti = lax.broadcasted_iota(jnp.int32, (c, c), 0)
    tj = lax.broadcasted_iota(jnp.int32, (c, c), 1)
    b = _sel_dot((((ri // c) == (rj // c)) & (rj <= ri)).astype(F32), logf, 3)

    def block_row(size, offset):
        return jnp.concatenate([jnp.broadcast_to(b[r + offset:r + offset + 1, :], (size, w))
                                for r in range(0, rows, size)], axis=0)

    b_last = block_row(c, c - 1)

    yield
    n0 = C_SUB
    ref = block_row(n0, n0 // 2 - 1)
    mask = ((ti // n0) == (tj // n0)) & (tj <= ti)
    attn = jnp.where(mask, _bmm_nt(heads(q * jnp.exp(b - ref)), heads(kf * jnp.exp(ref - b))), 0.0)
    yield
    n = n0
    while n < c:
        ref = block_row(2 * n, n - 1)
        right = ((row // n) % 2) == 1
        decay = jnp.exp(jnp.where(right, b - ref, ref - b))
        mask = ((ti // (2 * n)) == (tj // (2 * n))) & (((ti // n) % 2) == 1) & (((tj // n) % 2) == 0)
        attn = jnp.where(mask, _bmm_nt(heads(q * decay), heads(kf * decay)), attn)
        n *= 2
        yield

    vals = heads(val)
    intra = _bmm(attn, vals)
    q_in = heads(q * jnp.exp(b))
    k_end = heads(kf * jnp.exp(b_last - b))
    g_end = jnp.exp(b_last)
    gates = cg * _sigmoid(cg)
    st = state_ref[...]
    out_chunks = []
    for ci in range(n_ch):
        bs = slice(ci * C_HEADS, (ci + 1) * C_HEADS)
        o = intra[bs] + _bmm_nt(q_in[bs], st)
        vt = jnp.stack([vals[ci * C_HEADS + hd].T for hd in range(C_HEADS)])
        decay = jnp.stack([g_end[(ci + 1) * c - 1:(ci + 1) * c, hd * C_DIM:(hd + 1) * C_DIM] for hd in range(C_HEADS)])
        st = st * decay + _bmm(vt, k_end[bs])
        out_chunks.append(jnp.concatenate([_rms(o[hd], ng_ref[...]) for hd in range(C_HEADS)], axis=1))
        yield
    state_ref[...] = st
    o_ref[...] = jnp.concatenate(out_chunks, axis=0) * gates


def _recurrent_body(has_vres, n_rwkv_in, *refs):
    n_rwkv_out = 1 if has_vres else 2
    rwkv_in, rest = refs[:n_rwkv_in], refs[n_rwkv_in:]
    hgrn_in, rest = rest[:3], rest[3:]
    rwkv_out, hgrn_out, (rwkv_state, hgrn_state) = rest[:n_rwkv_out], rest[n_rwkv_out], rest[n_rwkv_out + 1:]

    @pl.when(pl.program_id(0) == 0)
    def _():
        rwkv_state[...] = jnp.zeros_like(rwkv_state)
        hgrn_state[...] = jnp.zeros_like(hgrn_state)

    rwkv = _rwkv_steps(has_vres, *rwkv_in, *rwkv_out, rwkv_state)
    hgrn = _hgrn_steps(*hgrn_in, hgrn_out, hgrn_state)
    pending = [rwkv, hgrn]
    while pending:
        for steps in list(pending):
            if next(steps, StopIteration) is StopIteration:
                pending.remove(steps)


def _recurrent_mixers(x_a, x_c, p, v_first, lower, norm_g):
    s = x_a.shape[0]
    rows = A_CHUNK * A_CHUNKS_PER_STEP
    assert rows == C_CHUNK * C_CHUNKS_PER_STEP
    has_vres = v_first is not None
    row = lambda n: pl.BlockSpec((rows, n), lambda i: (i, 0))
    const = lambda a: pl.BlockSpec(a.shape, lambda i: (0,) * a.ndim)
    names = ["w0", "w_up", "a0", "a_up", "g_up", "k_k", "k_a", "r_k", "gn_g", "gn_b"]
    if has_vres:
        names += ["v0", "vres_up"]
    consts = [p[n] for n in names]
    rwkv_ins = [x_a] + ([v_first] if has_vres else []) + consts
    rwkv_specs = [row(A_IN)] + ([row(A_WIDTH)] if has_vres else []) + [const(a) for a in consts]
    n_y = 1 if has_vres else 2
    y_shape = jax.ShapeDtypeStruct((s, A_WIDTH), F32)
    out = pl.pallas_call(
        functools.partial(_recurrent_body, has_vres, len(rwkv_ins)),
        grid=(s // rows,),
        in_specs=rwkv_specs + [row(C_IN), const(lower), const(norm_g)],
        out_specs=[row(A_WIDTH)] * n_y + [row(C_WIDTH)],
        out_shape=[y_shape] * n_y + [jax.ShapeDtypeStruct((s, C_WIDTH), F32)],
        scratch_shapes=[pltpu.VMEM((A_WIDTH // LANES, LANES, LANES), F32), pltpu.VMEM((C_HEADS, C_DIM, C_DIM), F32)],
        compiler_params=_params("arbitrary"), name="rwkv7_hgrn2",
    )(*rwkv_ins, x_c, lower, norm_g)
    if has_vres:
        return out[0], v_first, out[1]
    return out[0], out[1], out[2]


def _merge_body(h_ref, ya_ref, yb_ref, yc_ref, g_ref, wg_ref, wa_ref, wb_ref, wc_ref, wo_ref, pg_ref, o_ref):
    d = D_MODEL
    h = h_ref[...]
    u = _rms(h, g_ref[...]).astype(BF16)
    merged = None
    for n, (y_ref, w_ref) in enumerate(((ya_ref, wa_ref), (yb_ref, wb_ref), (yc_ref, wc_ref))):
        gate = _sigmoid(jnp.dot(u, wg_ref[:, n * d:(n + 1) * d], preferred_element_type=F32))
        term = gate * _bdot(y_ref[...], w_ref[...])
        merged = term if merged is None else merged + term
    o_ref[...] = h + _rms(_bdot(merged, wo_ref[...]), pg_ref[...])


def _merge(h, ya, yb, yc, pre_g, wg, wa, wb, wc, wo, post_g):
    s = h.shape[0]
    tm = ROW_TILE
    row = lambda n: pl.BlockSpec((tm, n), lambda i: (i, 0))
    const = lambda a: pl.BlockSpec(a.shape, lambda i: (0,) * a.ndim)
    consts = [pre_g, wg, wa, wb, wc, wo, post_g]
    return pl.pallas_call(
        _merge_body, grid=(s // tm,),
        in_specs=[row(D_MODEL), row(A_WIDTH), row(B_HEADS * LANES), row(C_WIDTH)] + [const(a) for a in consts],
        out_specs=row(D_MODEL), out_shape=jax.ShapeDtypeStruct((s, D_MODEL), F32),
        compiler_params=_params("arbitrary"), name="merge",
    )(h, ya, yb, yc, *consts)


def _pad_cols(a, n):
    return jnp.pad(a, ((0, 0), (0, n - a.shape[1])))


def _pad_rows(a, n):
    return jnp.pad(a, ((0, n - a.shape[0]), (0, 0)))


def _rope_half_swap(a):
    half = B_ROPE // 2
    return jnp.concatenate([a[:, half:], a[:, :half]], axis=1)


def _layer_weights(l, w_in, rwkv_mu, vres_down, vres_mu, w_uq, w_ukv, mla_out):
    wi = w_in[l]
    d = D_MODEL
    o = 3 * A_WIDTH
    lora = [wi[:, o:o + 64], wi[:, o + 64:o + 128], wi[:, o + 128:o + 256]]
    mus = [rwkv_mu[l][o:o + 64], rwkv_mu[l][o + 64:o + 128], rwkv_mu[l][o + 128:o + 256]]
    if l > 0:
        lora.append(vres_down[l - 1])
        mus.append(vres_mu[l - 1])
    else:
        lora.append(jnp.zeros((d, 0), F32))
        mus.append(jnp.zeros((0,), F32))
    w_a = jnp.concatenate([wi[:, :o]] + [_pad_cols(t, LANES) for t in lora], axis=1)
    mu_a = jnp.concatenate([rwkv_mu[l][:o]] + [jnp.pad(t, (0, LANES - t.shape[0])) for t in mus])[None, :]

    o = A_COLS
    w_kr = wi[:, o + B_Q_RANK + B_KV_RANK:o + B_Q_RANK + B_KV_RANK + B_ROPE]
    place = lambda t: jnp.pad(t, ((0, 0), (B_NOPE, LANES - B_NOPE - B_ROPE)))
    w_b = jnp.concatenate([wi[:, o:o + B_Q_RANK + B_KV_RANK], place(w_kr), place(_rope_half_swap(w_kr))], axis=1)
    o += B_Q_RANK + B_KV_RANK + B_ROPE
    w_c = wi[:, o:o + C_IN]
    w_g = wi[:, o + C_IN:o + C_IN + N_BRANCH * d]

    scale = (B_NOPE + B_ROPE) ** -0.5 * np.log2(np.e)
    uq = (w_uq[l] * scale).reshape(B_Q_RANK, B_HEADS, B_NOPE + B_ROPE)
    uq_sw = jnp.concatenate([jnp.zeros_like(uq[..., :B_NOPE]), uq[..., B_NOPE + B_ROPE // 2:],
                             uq[..., B_NOPE:B_NOPE + B_ROPE // 2]], axis=-1)
    pad_head = lambda t: jnp.pad(t, ((0, 0), (0, 0), (0, LANES - t.shape[-1]))).reshape(t.shape[0], B_HEADS * LANES)
    ukv = w_ukv[l].reshape(B_KV_RANK, B_HEADS, B_NOPE + B_V)
    mo = jnp.pad(mla_out[l].reshape(B_HEADS, B_V, d), ((0, 0), (0, LANES - B_V), (0, 0))).reshape(B_HEADS * LANES, d)
    bf = lambda t: t.astype(BF16)
    return dict(w_a=bf(w_a), mu_a=mu_a, w_b=bf(w_b), w_c=bf(w_c), w_g=bf(w_g),
                w_q=bf(pad_head(uq)), w_q_swap=bf(pad_head(uq_sw)),
                w_k=bf(pad_head(ukv[..., :B_NOPE])), w_v=bf(pad_head(ukv[..., B_NOPE:])), mla_out=bf(mo))


def _rope_tables(positions):
    inv_freq = ROPE_THETA ** (-jnp.arange(0, B_ROPE, 2, dtype=F32) / B_ROPE)
    ang = positions.astype(F32)[:, None] * inv_freq
    cos, sin = jnp.cos(ang), jnp.sin(ang)
    s = positions.shape[0]
    pad = jnp.zeros((s, LANES - B_NOPE - B_ROPE), F32)
    cos_q = jnp.concatenate([jnp.ones((s, B_NOPE), F32), cos, cos, pad], axis=1)
    sin_t = jnp.concatenate([jnp.zeros((s, B_NOPE), F32), -sin, sin, pad], axis=1)
    cos_k = jnp.concatenate([jnp.zeros((s, B_NOPE), F32), cos, cos, pad], axis=1)
    return cos_q, sin_t, cos_k, sin_t


def kernel(x, positions, ffn1_pre_g, ffn1_post_g, ffn1_w_gate, ffn1_w_up, ffn1_w_down, mix_pre_g, mix_post_g, w_in, rwkv_mu, rwkv_w0, rwkv_w_up, rwkv_a0, rwkv_a_up, rwkv_g_up, rwkv_k_k, rwkv_k_a, rwkv_r_k, rwkv_gn_g, rwkv_gn_b, rwkv_vres_down, rwkv_vres_mu, rwkv_vres_up, rwkv_v0, rwkv_out, mla_q_norm_g, mla_w_uq, mla_kv_norm_g, mla_w_ukv, mla_out, hgrn_lower_bounds, hgrn_norm_g, hgrn_out, w_o, ffn2_pre_g, ffn2_post_g, ffn2_w_gate, ffn2_w_up, ffn2_w_down):
    bsz, seq, d = x.shape
    assert bsz == 1 and d == D_MODEL and seq % max(ROW_TILE, C_CHUNK * C_CHUNKS_PER_STEP, A_CHUNK * A_CHUNKS_PER_STEP) == 0
    depth = w_in.shape[0]
    flash_tile = min(1024, seq)
    tabs = _rope_tables(positions[0])
    lb_p = jax.nn.softmax(hgrn_lower_bounds.astype(F32), axis=0)
    lower_bounds = jnp.cumsum(lb_p, axis=0) - lb_p[0]
    row = lambda t: t[None, :]
    bf = lambda t: t.astype(BF16)

    h = x[0]
    v_first = None
    for l in range(depth):
        h = _ffn(h, row(ffn1_pre_g[l]), bf(ffn1_w_gate[l]), bf(ffn1_w_up[l]), bf(ffn1_w_down[l]),
                 row(ffn1_post_g[l]))

        lw = _layer_weights(l, w_in, rwkv_mu, rwkv_vres_down, rwkv_vres_mu, mla_w_uq, mla_w_ukv, mla_out)
        pre_g = row(mix_pre_g[l])
        x_a = _proj(h, pre_g, lw["w_a"], lw["mu_a"])
        x_b = _proj(h, pre_g, lw["w_b"])
        x_c = _proj(h, pre_g, lw["w_c"])

        pa = dict(w0=row(rwkv_w0[l]), w_up=bf(_pad_rows(rwkv_w_up[l], LANES)), a0=row(rwkv_a0[l]),
                  a_up=bf(_pad_rows(rwkv_a_up[l], LANES)), g_up=bf(rwkv_g_up[l]), k_k=row(rwkv_k_k[l]),
                  k_a=row(rwkv_k_a[l]), r_k=row(rwkv_r_k[l].reshape(-1)), gn_g=row(rwkv_gn_g[l]),
                  gn_b=row(rwkv_gn_b[l]))
        if l > 0:
            pa.update(v0=row(rwkv_v0[l - 1]), vres_up=bf(_pad_rows(rwkv_vres_up[l - 1], LANES)))
        y_a, v_first, y_c = _recurrent_mixers(x_a, x_c, pa, v_first, row(lower_bounds[l]), row(hgrn_norm_g[l]))

        pb = dict(q_norm_g=row(mla_q_norm_g[l]), kv_norm_g=row(mla_kv_norm_g[l]), w_q=lw["w_q"],
                  w_q_swap=lw["w_q_swap"], w_k=lw["w_k"], w_v=lw["w_v"])
        q_h, k_h, vt_h = _mla_prep(x_b, tabs, pb)
        y_b = _flash(q_h, k_h, vt_h, flash_tile)

        h = _merge(h, y_a, y_b, y_c, pre_g, lw["w_g"], bf(rwkv_out[l]), lw["mla_out"], bf(hgrn_out[l]),
                   bf(w_o[l]), row(mix_post_g[l]))

        h = _ffn(h, row(ffn2_pre_g[l]), bf(ffn2_w_gate[l]), bf(ffn2_w_up[l]), bf(ffn2_w_down[l]),
                 row(ffn2_post_g[l]))
    return h[None]
```

```python
import functools

import jax
import jax.numpy as jnp
import numpy as np
from jax import lax
from jax.experimental import pallas as pl
from jax.experimental.pallas import tpu as pltpu

F32 = jnp.float32
BF16 = jnp.bfloat16

D_MODEL = 1024
D_FF = 2816
NORM_EPS = 1e-6
MACARON_WEIGHT = 0.5

A_HEADS = 8
A_HEAD_DIM = 64
A_WIDTH = 512
A_GN_EPS = 64e-5
A_CHUNK = 64
A_CHUNKS_PER_STEP = 4
A_COLS = 1792
A_IN = 2048

B_HEADS = 8
B_NOPE = 64
B_ROPE = 32
B_V = 64
B_Q_RANK = 384
B_KV_RANK = 256
B_IN = 896
ROPE_THETA = 10000.0
B_SUBTILES = 4
B_HEADS_PER_STEP = 2
LANES = 128

C_HEADS = 4
C_DIM = 128
C_WIDTH = 512
C_IN = 2048
C_CHUNK = 128
C_CHUNKS_PER_STEP = 2
C_SUB = 8
C_MIN_FORGET = 1e-6

N_BRANCH = 3
ROW_TILE = 512
FFN_ROW_TILE = 1024
VMEM_LIMIT = 56 * 1024 * 1024

assert C_SUB / 2 * -np.log(C_MIN_FORGET) < 80.0


def _bdot(a, b):
    return jnp.dot(a.astype(BF16), b.astype(BF16), preferred_element_type=F32)


def _bdot_nt(a, b):
    return lax.dot_general(a.astype(BF16), b.astype(BF16), (((1,), (1,)), ((), ())),
                           preferred_element_type=F32)


def _bmm(a, b):
    return lax.dot_general(a.astype(BF16), b.astype(BF16), (((2,), (1,)), ((0,), (0,))),
                           preferred_element_type=F32)


def _bmm_nt(a, b):
    return lax.dot_general(a.astype(BF16), b.astype(BF16), (((2,), (2,)), ((0,), (0,))),
                           preferred_element_type=F32)


def _split_terms(data, parts):
    terms, rem = [], data
    for _ in range(parts):
        piece = rem.astype(BF16)
        terms.append(piece)
        rem = rem - piece.astype(F32)
    return terms


def _sel_dot(sel, data, parts):
    sel = sel.astype(BF16)
    return sum(jnp.dot(sel, t, preferred_element_type=F32) for t in _split_terms(data, parts))


def _dot_sel(data, sel, parts):
    sel = sel.astype(BF16)
    return sum(jnp.dot(t, sel, preferred_element_type=F32) for t in _split_terms(data, parts))


def _rms(x, g):
    return x * lax.rsqrt(jnp.mean(x * x, axis=-1, keepdims=True) + NORM_EPS) * g


def _sigmoid(x):
    return jax.nn.sigmoid(x)


def _params(*sem):
    return pltpu.CompilerParams(dimension_semantics=sem, vmem_limit_bytes=VMEM_LIMIT)


def _ffn_body(x_ref, pre_g_ref, wg_ref, wu_ref, wd_ref, post_g_ref, o_ref):
    x = x_ref[...]
    xn = _rms(x, pre_g_ref[...]).astype(BF16)
    gate = jnp.dot(xn, wg_ref[...], preferred_element_type=F32)
    up = jnp.dot(xn, wu_ref[...], preferred_element_type=F32)
    mid = (gate * _sigmoid(gate) * up).astype(BF16)
    y = jnp.dot(mid, wd_ref[...], preferred_element_type=F32)
    o_ref[...] = x + MACARON_WEIGHT * _rms(y, post_g_ref[...])


def _ffn(h, pre_g, wg, wu, wd, post_g):
    s = h.shape[0]
    tm = FFN_ROW_TILE
    const = lambda a: pl.BlockSpec(a.shape, lambda i: (0,) * a.ndim)
    return pl.pallas_call(
        _ffn_body,
        grid=(s // tm,),
        in_specs=[pl.BlockSpec((tm, D_MODEL), lambda i: (i, 0)), const(pre_g), const(wg), const(wu), const(wd),
                  const(post_g)],
        out_specs=pl.BlockSpec((tm, D_MODEL), lambda i: (i, 0)),
        out_shape=jax.ShapeDtypeStruct((s, D_MODEL), F32),
        compiler_params=_params("arbitrary"),
        name="ffn",
    )(h, pre_g, wg, wu, wd, post_g)


def _proj_body(x_ref, g_ref, w_ref, o_ref):
    xn = _rms(x_ref[...], g_ref[...]).astype(BF16)
    o_ref[...] = jnp.dot(xn, w_ref[...], preferred_element_type=F32)


def _proj_shift_body(x_ref, g_ref, w_ref, mu_ref, o_ref, carry_ref):
    @pl.when(pl.program_id(0) == 0)
    def _():
        carry_ref[...] = jnp.zeros_like(carry_ref)

    xn = _rms(x_ref[...], g_ref[...]).astype(BF16)
    p = jnp.dot(xn, w_ref[...], preferred_element_type=F32)
    rows = p.shape[0]
    prev = pltpu.roll(p, 1, axis=0)
    first = lax.broadcasted_iota(jnp.int32, p.shape, 0) == 0
    prev = jnp.where(first, carry_ref[0:1, :], prev)
    carry_ref[0:1, :] = p[rows - 1:rows, :]
    o_ref[...] = p + (prev - p) * mu_ref[...]


def _proj(h, pre_g, w, mu=None):
    s = h.shape[0]
    n = w.shape[1]
    x_spec = pl.BlockSpec((ROW_TILE, D_MODEL), lambda i: (i, 0))
    g_spec = pl.BlockSpec((1, D_MODEL), lambda i: (0, 0))
    w_spec = pl.BlockSpec((D_MODEL, n), lambda i: (0, 0))
    o_spec = pl.BlockSpec((ROW_TILE, n), lambda i: (i, 0))
    out_shape = jax.ShapeDtypeStruct((s, n), F32)
    if mu is None:
        return pl.pallas_call(
            _proj_body, grid=(s // ROW_TILE,), in_specs=[x_spec, g_spec, w_spec], out_specs=o_spec,
            out_shape=out_shape, compiler_params=_params("arbitrary"), name="proj",
        )(h, pre_g, w)
    return pl.pallas_call(
        _proj_shift_body, grid=(s // ROW_TILE,),
        in_specs=[x_spec, g_spec, w_spec, pl.BlockSpec((1, n), lambda i: (0, 0))], out_specs=o_spec,
        out_shape=out_shape, scratch_shapes=[pltpu.VMEM((8, n), F32)],
        compiler_params=_params("arbitrary"), name="proj_shift",
    )(h, pre_g, w, mu)


def _tri_inverse(a_strict, eye, blk16, lvl1, lvl2):
    d = a_strict * blk16
    d2 = _bmm(d, d)
    d4 = _bmm(d2, d2)
    d8 = _bmm(d4, d4)
    t = eye + d
    t = t + _bmm(t, d2)
    t = t + _bmm(t, d4)
    t = t + _bmm(t, d8)
    t = t + _bmm(_bmm(t, a_strict * lvl1), t)
    t = t + _bmm(_bmm(t, a_strict * lvl2), t)
    return t


def _rwkv_steps(has_vres, *refs):
    if has_vres:
        (x_ref, vf_ref, w0_ref, wup_ref, a0_ref, aup_ref, gup_ref, kk_ref, ka_ref, rk_ref, gng_ref, gnb_ref,
         v0_ref, vup_ref, y_ref, state_ref) = refs
    else:
        (x_ref, w0_ref, wup_ref, a0_ref, aup_ref, gup_ref, kk_ref, ka_ref, rk_ref, gng_ref, gnb_ref,
         y_ref, vf_out_ref, state_ref) = refs

    c = A_CHUNK
    w = A_WIDTH
    hd = A_HEAD_DIM
    x = x_ref[...]
    rows = x.shape[0]
    r = x[:, 0:w]
    k = x[:, w:2 * w]
    v = x[:, 2 * w:3 * w]
    wl = x[:, 3 * w:3 * w + LANES]
    al = x[:, 3 * w + LANES:3 * w + 2 * LANES]
    gl = x[:, 3 * w + 2 * LANES:3 * w + 3 * LANES]

    z = w0_ref[...] + _bdot(jnp.tanh(wl), wup_ref[...])
    softplus_neg = jnp.maximum(-z, 0.0) + jnp.log(1.0 + jnp.exp(-jnp.abs(z)))
    logw = -jnp.exp(-softplus_neg - 0.5)
    a = _sigmoid(a0_ref[...] + _bdot(al, aup_ref[...]))
    g = _bdot(_sigmoid(gl), gup_ref[...])
    if has_vres:
        vl = x[:, 3 * w + 3 * LANES:3 * w + 4 * LANES]
        v = v + (vf_ref[...] - v) * _sigmoid(v0_ref[...] + _bdot(vl, vup_ref[...]))
    else:
        vf_out_ref[...] = v

    li = lax.broadcasted_iota(jnp.int32, (LANES, LANES), 0)
    lj = lax.broadcasted_iota(jnp.int32, (LANES, LANES), 1)
    pair_bd = ((li < hd) == (lj < hd)).astype(F32)
    lane = lax.broadcasted_iota(jnp.int32, (1, LANES), 1)
    m0 = (lane < hd).astype(F32)
    m1 = (lane >= hd).astype(F32)

    def head_sum(t):
        return jnp.concatenate([_dot_sel(t[:, p * LANES:(p + 1) * LANES], pair_bd, 1) for p in range(w // LANES)],
                               axis=1)

    def by_head(t):
        return jnp.concatenate([t * m0, t * m1], axis=0)

    yield
    kkr = k * kk_ref[...]
    kk = kkr / jnp.maximum(jnp.sqrt(head_sum(kkr * kkr)), 1e-12)
    k = k * (1.0 + (a - 1.0) * ka_ref[...])

    ti = lax.broadcasted_iota(jnp.int32, (2 * c, 2 * c), 0)
    tj = lax.broadcasted_iota(jnp.int32, (2 * c, 2 * c), 1)
    same_head = (ti // c) == (tj // c)
    incl = (same_head & (tj <= ti)).astype(F32)
    strict = (same_head & (tj < ti)).astype(F32)
    eye = (tj == ti).astype(F32)
    blk16 = ((ti // 16) == (tj // 16)).astype(F32)
    lvl1 = (((ti // 32) == (tj // 32)) & ((ti // 16) != (tj // 16))).astype(F32)
    lvl2 = (same_head & ((ti // 32) != (tj // 32))).astype(F32)

    ri = lax.broadcasted_iota(jnp.int32, (rows, rows), 0)
    rj = lax.broadcasted_iota(jnp.int32, (rows, rows), 1)
    chunk_incl = (((ri // c) == (rj // c)) & (rj <= ri)).astype(F32)
    lb = _sel_dot(chunk_incl, logw, 3)
    e_neg = jnp.exp(-lb)
    alpha_t = -kk * jnp.exp(lb - logw)
    beta = kk * a
    beta_h = beta * e_neg
    k_h = k * e_neg
    r_t = r * jnp.exp(lb)

    n_ch = rows // c
    n_pair = w // LANES

    def slabs(t):
        return [t[ci * c:(ci + 1) * c, p * LANES:(p + 1) * LANES] for ci in range(n_ch) for p in range(n_pair)]

    def stack_by_head(t):
        return jnp.stack([by_head(s) for s in slabs(t)])

    yield
    xa2, v2 = stack_by_head(alpha_t), stack_by_head(v)
    gram = _bmm_nt(jnp.concatenate([xa2, stack_by_head(r_t)], axis=1),
                   jnp.concatenate([stack_by_head(beta_h), stack_by_head(k_h)], axis=1))
    a_ab = gram[:, 0:2 * c, 0:2 * c] * strict
    a_ak = gram[:, 0:2 * c, 2 * c:4 * c] * strict
    a_r = jnp.concatenate([gram[:, 2 * c:4 * c, 2 * c:4 * c] * incl, gram[:, 2 * c:4 * c, 0:2 * c] * incl], axis=2)
    yield
    t = _tri_inverse(a_ab, eye, blk16, lvl1, lvl2)
    yield
    wu = _bmm(t, jnp.concatenate([xa2, _bmm(a_ak, v2)], axis=2))
    wt = wu[:, 0:c, 0:LANES] + wu[:, c:2 * c, 0:LANES]
    ut = wu[:, 0:c, LANES:2 * LANES] + wu[:, c:2 * c, LANES:2 * LANES]
    xr = jnp.stack(slabs(r_t))
    vp = jnp.stack(slabs(v))

    yield
    st = state_ref[...]
    y_chunks = []
    for ci in range(n_ch):
        bs = slice(ci * n_pair, (ci + 1) * n_pair)
        lb_last = lb[(ci + 1) * c - 1:(ci + 1) * c, :]
        e_end = jnp.exp(lb_last - lb[ci * c:(ci + 1) * c])
        k_e = k[ci * c:(ci + 1) * c] * e_end
        beta_e = beta[ci * c:(ci + 1) * c] * e_end
        gamma_c = jnp.exp(lb_last)
        u = _bmm_nt(wt[bs], st) + ut[bs]
        u2 = jnp.concatenate([u * m0, u * m1], axis=1)
        y2 = _bmm(a_r[bs], jnp.concatenate([v2[bs], u2], axis=1))
        y = _bmm_nt(xr[bs], st) + y2[:, 0:c] + y2[:, c:2 * c]
        y_chunks.append(jnp.concatenate([y[p] for p in range(n_pair)], axis=1))
        vu_t = jnp.stack([jnp.concatenate([vp[ci * n_pair + p], u[p]], axis=0).T for p in range(n_pair)])
        ke_be = jnp.stack([jnp.concatenate([k_e[:, p * LANES:(p + 1) * LANES], beta_e[:, p * LANES:(p + 1) * LANES]],
                                           axis=0) for p in range(n_pair)])
        decay = jnp.stack([gamma_c[:, p * LANES:(p + 1) * LANES] for p in range(n_pair)])
        st = st * decay + pair_bd * _bmm(vu_t, ke_be)
        yield
    state_ref[...] = st
    y = jnp.concatenate(y_chunks, axis=0)

    inv_n = 1.0 / hd
    mean = head_sum(y) * inv_n
    yc = y - mean
    var = head_sum(yc * yc) * inv_n
    yn = yc * lax.rsqrt(var + A_GN_EPS) * gng_ref[...] + gnb_ref[...]
    bonus = head_sum(r * k * rk_ref[...]) * v
    y_ref[...] = (yn + bonus) * g


def _mla_prep_body(x_ref, cq_ref, sq_ref, ck_ref, sk_ref, qg_ref, wq_ref, wqs_ref, kg_ref, wk_ref, wv_ref,
                   q_ref, k_ref, vt_ref):
    x = x_ref[...]
    cq = x[:, 0:B_Q_RANK]
    ckv = x[:, B_Q_RANK:B_Q_RANK + B_KV_RANK]
    kr = x[:, B_Q_RANK + B_KV_RANK:B_Q_RANK + B_KV_RANK + LANES]
    kr_sw = x[:, B_Q_RANK + B_KV_RANK + LANES:B_Q_RANK + B_KV_RANK + 2 * LANES]
    cqn = _rms(cq, qg_ref[...]).astype(BF16)
    ckn = _rms(ckv, kg_ref[...]).astype(BF16)
    q = jnp.dot(cqn, wq_ref[...], preferred_element_type=F32)
    q_sw = jnp.dot(cqn, wqs_ref[...], preferred_element_type=F32)
    k_nope = jnp.dot(ckn, wk_ref[...], preferred_element_type=F32)
    val = jnp.dot(ckn, wv_ref[...], preferred_element_type=F32)
    k_rope = kr * ck_ref[...] + kr_sw * sk_ref[...]
    cq_t, sq_t = cq_ref[...], sq_ref[...]
    ones_col = (lax.broadcasted_iota(jnp.int32, (1, LANES), 1) == B_V).astype(F32)
    for hd in range(B_HEADS):
        sl = slice(hd * LANES, (hd + 1) * LANES)
        q_ref[hd] = (q[:, sl] * cq_t + q_sw[:, sl] * sq_t).astype(BF16)
        k_ref[hd] = (k_nope[:, sl] + k_rope).astype(BF16)
        vt_ref[hd] = (val[:, sl] + ones_col).T.astype(BF16)


def _mla_prep(x, tabs, p):
    s = x.shape[0]
    tm = ROW_TILE
    row = lambda n: pl.BlockSpec((tm, n), lambda i: (i, 0))
    const = lambda a: pl.BlockSpec(a.shape, lambda i: (0,) * a.ndim)
    consts = [p["q_norm_g"], p["w_q"], p["w_q_swap"], p["kv_norm_g"], p["w_k"], p["w_v"]]
    head_spec = pl.BlockSpec((B_HEADS, tm, LANES), lambda i: (0, i, 0))
    head_shape = jax.ShapeDtypeStruct((B_HEADS, s, LANES), BF16)
    head_t_spec = pl.BlockSpec((B_HEADS, LANES, tm), lambda i: (0, 0, i))
    head_t_shape = jax.ShapeDtypeStruct((B_HEADS, LANES, s), BF16)
    return pl.pallas_call(
        _mla_prep_body, grid=(s // tm,),
        in_specs=[row(B_IN)] + [row(LANES)] * 4 + [const(a) for a in consts],
        out_specs=[head_spec, head_spec, head_t_spec], out_shape=[head_shape, head_shape, head_t_shape],
        compiler_params=_params("arbitrary"), name="mla_prep",
    )(x, *tabs, *consts)


def _flash_body(q_ref, qn_ref, k_ref, vt_ref, o_ref, m_ref, acc_ref, s0_ref, s1_ref, s2_ref):
    i = pl.program_id(1)
    tile = q_ref.shape[1]
    sub = tile // B_SUBTILES
    units = [(a, hh) for a in range(B_SUBTILES) for hh in range(B_HEADS_PER_STEP)]
    m_ref[...] = jnp.full_like(m_ref, -jnp.inf)
    acc_ref[...] = jnp.zeros_like(acc_ref)

    def scores(queries_ref, j, s_ref, a, hh):
        start = pl.multiple_of(j * tile, tile)
        s_ref[hh, :, a * sub:(a + 1) * sub] = lax.dot_general(
            k_ref[hh, pl.ds(start, tile), :], queries_ref[hh, a * sub:(a + 1) * sub, :], (((1,), (1,)), ((), ())),
            preferred_element_type=F32)

    def softmax_pv(j, s_ref, a, hh, on_diagonal):
        start = pl.multiple_of(j * tile, tile)
        cs = slice(a * sub, (a + 1) * sub)
        n_keys = (a + 1) * sub if on_diagonal else tile
        s = s_ref[hh, 0:n_keys, cs]
        if on_diagonal:
            keys = lax.broadcasted_iota(jnp.int32, s.shape, 0)
            queries = lax.broadcasted_iota(jnp.int32, s.shape, 1) + a * sub
            s = jnp.where(keys <= queries, s, -jnp.inf)
        m_old = m_ref[hh:hh + 1, cs]
        m_new = jnp.maximum(m_old, jnp.max(s, axis=0, keepdims=True))
        p = jnp.exp2(s - m_new).astype(BF16)
        pv = jnp.dot(vt_ref[hh, :, pl.ds(start, n_keys)], p, preferred_element_type=F32)
        acc_ref[hh, :, cs] = jnp.exp2(m_old - m_new) * acc_ref[hh, :, cs] + pv
        m_ref[hh:hh + 1, cs] = m_new

    def phase(j, cur_ref, nxt_ref):
        for a, hh in units:
            scores(q_ref, j + 1, nxt_ref, a, hh)
            softmax_pv(j, cur_ref, a, hh, False)

    def diagonal(cur_ref, overlap_next):
        for a, hh in units:
            if overlap_next:
                scores(qn_ref, 0, s2_ref, a, hh)
            softmax_pv(i, cur_ref, a, hh, True)
        for hh in range(B_HEADS_PER_STEP):
            acc = acc_ref[hh]
            o_ref[:, hh * LANES:(hh + 1) * LANES] = (acc / acc[B_V:B_V + 1, :]).T

    @pl.when(i == 0)
    def _():
        for a, hh in units:
            scores(q_ref, 0, s2_ref, a, hh)
        diagonal(s2_ref, False)
        for a, hh in units:
            scores(qn_ref, 0, s2_ref, a, hh)

    @pl.when(i > 0)
    def _():
        phase(0, s2_ref, s0_ref)

        def two_tiles(jj, carry):
            phase(2 * jj + 1, s0_ref, s1_ref)
            phase(2 * jj + 2, s1_ref, s0_ref)
            return carry

        lax.fori_loop(0, (i - 1) // 2, two_tiles, 0)

        @pl.when(i % 2 == 0)
        def _():
            phase(i - 1, s0_ref, s1_ref)
            diagonal(s1_ref, True)

        @pl.when(i % 2 == 1)
        def _():
            diagonal(s0_ref, True)


def _flash(q, k, vt, tile):
    n_h, s, _ = q.shape
    n_q = s // tile
    hs = B_HEADS_PER_STEP
    scores = pltpu.VMEM((hs, tile, tile), F32)
    resident = pl.Buffered(1)
    return pl.pallas_call(
        _flash_body, grid=(n_h // hs, n_q),
        in_specs=[pl.BlockSpec((hs, tile, LANES), lambda h, i: (h, i, 0)),
                  pl.BlockSpec((hs, tile, LANES), lambda h, i: (h, jnp.minimum(i + 1, n_q - 1), 0)),
                  pl.BlockSpec((hs, s, LANES), lambda h, i: (h, 0, 0), pipeline_mode=resident),
                  pl.BlockSpec((hs, LANES, s), lambda h, i: (h, 0, 0), pipeline_mode=resident)],
        out_specs=pl.BlockSpec((tile, hs * LANES), lambda h, i: (i, h)),
        out_shape=jax.ShapeDtypeStruct((s, n_h * LANES), F32),
        scratch_shapes=[pltpu.VMEM((hs, tile), F32), pltpu.VMEM((hs, LANES, tile), F32), scores, scores, scores],
        compiler_params=_params("arbitrary", "arbitrary"), name="mla_flash",
    )(q, q, k, vt)


def _hgrn_steps(x_ref, lb_ref, ng_ref, o_ref, state_ref):
    c = C_CHUNK
    w = C_WIDTH
    x = x_ref[...]
    rows = x.shape[0]
    n_ch = rows // c
    lower = lb_ref[...]
    fz = x[:, w:2 * w]
    f = lower + (1.0 - lower) * _sigmoid(fz)
    logf = jnp.log(jnp.maximum(f, C_MIN_FORGET))
    kf = (1.0 - lower) * _sigmoid(-fz)
    cq = x[:, 0:w]
    q = cq * _sigmoid(cq)
    val = x[:, 2 * w:3 * w]
    cg = x[:, 3 * w:4 * w]

    def heads(t):
        return jnp.stack([t[ci * c:(ci + 1) * c, hd * C_DIM:(hd + 1) * C_DIM]
                          for ci in range(n_ch) for hd in range(C_HEADS)])

    ri = lax.broadcasted_iota(jnp.int32, (rows, rows), 0)
    rj = lax.broadcasted_iota(jnp.int32, (rows, rows), 1)
    row = lax.broadcasted_iota(jnp.int32, (rows, 1), 0)
    ti = lax.broadcasted_iota(jnp.int32, (c, c), 0)
    tj = lax.broadcasted_iota(jnp.int32, (c, c), 1)
    b = _sel_dot((((ri // c) == (rj // c)) & (rj <= ri)).astype(F32), logf, 3)

    def block_row(size, offset):
        return jnp.concatenate([jnp.broadcast_to(b[r + offset:r + offset + 1, :], (size, w))
                                for r in range(0, rows, size)], axis=0)

    b_last = block_row(c, c - 1)

    yield
    n0 = C_SUB
    ref = block_row(n0, n0 // 2 - 1)
    mask = ((ti // n0) == (tj // n0)) & (tj <= ti)
    attn = jnp.where(mask, _bmm_nt(heads(q * jnp.exp(b - ref)), heads(kf * jnp.exp(ref - b))), 0.0)
    yield
    n = n0
    while n < c:
        ref = block_row(2 * n, n - 1)
        right = ((row // n) % 2) == 1
        decay = jnp.exp(jnp.where(right, b - ref, ref - b))
        mask = ((ti // (2 * n)) == (tj // (2 * n))) & (((ti // n) % 2) == 1) & (((tj // n) % 2) == 0)
        attn = jnp.where(mask, _bmm_nt(heads(q * decay), heads(kf * decay)), attn)
        n *= 2
        yield

    vals = heads(val)
    intra = _bmm(attn, vals)
    q_in = heads(q * jnp.exp(b))
    k_end = heads(kf * jnp.exp(b_last - b))
    g_end = jnp.exp(b_last)
    gates = cg * _sigmoid(cg)
    st = state_ref[...]
    out_chunks = []
    for ci in range(n_ch):
        bs = slice(ci * C_HEADS, (ci + 1) * C_HEADS)
        o = intra[bs] + _bmm_nt(q_in[bs], st)
        vt = jnp.stack([vals[ci * C_HEADS + hd].T for hd in range(C_HEADS)])
        decay = jnp.stack([g_end[(ci + 1) * c - 1:(ci + 1) * c, hd * C_DIM:(hd + 1) * C_DIM] for hd in range(C_HEADS)])
        st = st * decay + _bmm(vt, k_end[bs])
        out_chunks.append(jnp.concatenate([_rms(o[hd], ng_ref[...]) for hd in range(C_HEADS)], axis=1))
        yield
    state_ref[...] = st
    o_ref[...] = jnp.concatenate(out_chunks, axis=0) * gates


def _recurrent_body(has_vres, n_rwkv_in, *refs):
    n_rwkv_out = 1 if has_vres else 2
    rwkv_in, rest = refs[:n_rwkv_in], refs[n_rwkv_in:]
    hgrn_in, rest = rest[:3], rest[3:]
    rwkv_out, hgrn_out, (rwkv_state, hgrn_state) = rest[:n_rwkv_out], rest[n_rwkv_out], rest[n_rwkv_out + 1:]

    @pl.when(pl.program_id(0) == 0)
    def _():
        rwkv_state[...] = jnp.zeros_like(rwkv_state)
        hgrn_state[...] = jnp.zeros_like(hgrn_state)

    rwkv = _rwkv_steps(has_vres, *rwkv_in, *rwkv_out, rwkv_state)
    hgrn = _hgrn_steps(*hgrn_in, hgrn_out, hgrn_state)
    pending = [rwkv, hgrn]
    while pending:
        for steps in list(pending):
            if next(steps, StopIteration) is StopIteration:
                pending.remove(steps)


def _recurrent_mixers(x_a, x_c, p, v_first, lower, norm_g):
    s = x_a.shape[0]
    rows = A_CHUNK * A_CHUNKS_PER_STEP
    assert rows == C_CHUNK * C_CHUNKS_PER_STEP
    has_vres = v_first is not None
    row = lambda n: pl.BlockSpec((rows, n), lambda i: (i, 0))
    const = lambda a: pl.BlockSpec(a.shape, lambda i: (0,) * a.ndim)
    names = ["w0", "w_up", "a0", "a_up", "g_up", "k_k", "k_a", "r_k", "gn_g", "gn_b"]
    if has_vres:
        names += ["v0", "vres_up"]
    consts = [p[n] for n in names]
    rwkv_ins = [x_a] + ([v_first] if has_vres else []) + consts
    rwkv_specs = [row(A_IN)] + ([row(A_WIDTH)] if has_vres else []) + [const(a) for a in consts]
    n_y = 1 if has_vres else 2
    y_shape = jax.ShapeDtypeStruct((s, A_WIDTH), F32)
    out = pl.pallas_call(
        functools.partial(_recurrent_body, has_vres, len(rwkv_ins)),
        grid=(s // rows,),
        in_specs=rwkv_specs + [row(C_IN), const(lower), const(norm_g)],
        out_specs=[row(A_WIDTH)] * n_y + [row(C_WIDTH)],
        out_shape=[y_shape] * n_y + [jax.ShapeDtypeStruct((s, C_WIDTH), F32)],
        scratch_shapes=[pltpu.VMEM((A_WIDTH // LANES, LANES, LANES), F32), pltpu.VMEM((C_HEADS, C_DIM, C_DIM), F32)],
        compiler_params=_params("arbitrary"), name="rwkv7_hgrn2",
    )(*rwkv_ins, x_c, lower, norm_g)
    if has_vres:
        return out[0], v_first, out[1]
    return out[0], out[1], out[2]


def _merge_body(h_ref, ya_ref, yb_ref, yc_ref, g_ref, wg_ref, wa_ref, wb_ref, wc_ref, wo_ref, pg_ref, o_ref):
    d = D_MODEL
    h = h_ref[...]
    u = _rms(h, g_ref[...]).astype(BF16)
    merged = None
    for n, (y_ref, w_ref) in enumerate(((ya_ref, wa_ref), (yb_ref, wb_ref), (yc_ref, wc_ref))):
        gate = _sigmoid(jnp.dot(u, wg_ref[:, n * d:(n + 1) * d], preferred_element_type=F32))
        term = gate * _bdot(y_ref[...], w_ref[...])
        merged = term if merged is None else merged + term
    o_ref[...] = h + _rms(_bdot(merged, wo_ref[...]), pg_ref[...])


def _merge(h, ya, yb, yc, pre_g, wg, wa, wb, wc, wo, post_g):
    s = h.shape[0]
    tm = ROW_TILE
    row = lambda n: pl.BlockSpec((tm, n), lambda i: (i, 0))
    const = lambda a: pl.BlockSpec(a.shape, lambda i: (0,) * a.ndim)
    consts = [pre_g, wg, wa, wb, wc, wo, post_g]
    return pl.pallas_call(
        _merge_body, grid=(s // tm,),
        in_specs=[row(D_MODEL), row(A_WIDTH), row(B_HEADS * LANES), row(C_WIDTH)] + [const(a) for a in consts],
        out_specs=row(D_MODEL), out_shape=jax.ShapeDtypeStruct((s, D_MODEL), F32),
        compiler_params=_params("arbitrary"), name="merge",
    )(h, ya, yb, yc, *consts)


def _pad_cols(a, n):
    return jnp.pad(a, ((0, 0), (0, n - a.shape[1])))


def _pad_rows(a, n):
    return jnp.pad(a, ((0, n - a.shape[0]), (0, 0)))


def _rope_half_swap(a):
    half = B_ROPE // 2
    return jnp.concatenate([a[:, half:], a[:, :half]], axis=1)


def _layer_weights(l, w_in, rwkv_mu, vres_down, vres_mu, w_uq, w_ukv, mla_out):
    wi = w_in[l]
    d = D_MODEL
    o = 3 * A_WIDTH
    lora = [wi[:, o:o + 64], wi[:, o + 64:o + 128], wi[:, o + 128:o + 256]]
    mus = [rwkv_mu[l][o:o + 64], rwkv_mu[l][o + 64:o + 128], rwkv_mu[l][o + 128:o + 256]]
    if l > 0:
        lora.append(vres_down[l - 1])
        mus.append(vres_mu[l - 1])
    else:
        lora.append(jnp.zeros((d, 0), F32))
        mus.append(jnp.zeros((0,), F32))
    w_a = jnp.concatenate([wi[:, :o]] + [_pad_cols(t, LANES) for t in lora], axis=1)
    mu_a = jnp.concatenate([rwkv_mu[l][:o]] + [jnp.pad(t, (0, LANES - t.shape[0])) for t in mus])[None, :]

    o = A_COLS
    w_kr = wi[:, o + B_Q_RANK + B_KV_RANK:o + B_Q_RANK + B_KV_RANK + B_ROPE]
    place = lambda t: jnp.pad(t, ((0, 0), (B_NOPE, LANES - B_NOPE - B_ROPE)))
    w_b = jnp.concatenate([wi[:, o:o + B_Q_RANK + B_KV_RANK], place(w_kr), place(_rope_half_swap(w_kr))], axis=1)
    o += B_Q_RANK + B_KV_RANK + B_ROPE
    w_c = wi[:, o:o + C_IN]
    w_g = wi[:, o + C_IN:o + C_IN + N_BRANCH * d]

    scale = (B_NOPE + B_ROPE) ** -0.5 * np.log2(np.e)
    uq = (w_uq[l] * scale).reshape(B_Q_RANK, B_HEADS, B_NOPE + B_ROPE)
    uq_sw = jnp.concatenate([jnp.zeros_like(uq[..., :B_NOPE]), uq[..., B_NOPE + B_ROPE // 2:],
                             uq[..., B_NOPE:B_NOPE + B_ROPE // 2]], axis=-1)
    pad_head = lambda t: jnp.pad(t, ((0, 0), (0, 0), (0, LANES - t.shape[-1]))).reshape(t.shape[0], B_HEADS * LANES)
    ukv = w_ukv[l].reshape(B_KV_RANK, B_HEADS, B_NOPE + B_V)
    mo = jnp.pad(mla_out[l].reshape(B_HEADS, B_V, d), ((0, 0), (0, LANES - B_V), (0, 0))).reshape(B_HEADS * LANES, d)
    bf = lambda t: t.astype(BF16)
    return dict(w_a=bf(w_a), mu_a=mu_a, w_b=bf(w_b), w_c=bf(w_c), w_g=bf(w_g),
                w_q=bf(pad_head(uq)), w_q_swap=bf(pad_head(uq_sw)),
                w_k=bf(pad_head(ukv[..., :B_NOPE])), w_v=bf(pad_head(ukv[..., B_NOPE:])), mla_out=bf(mo))


def _rope_tables(positions):
    inv_freq = ROPE_THETA ** (-jnp.arange(0, B_ROPE, 2, dtype=F32) / B_ROPE)
    ang = positions.astype(F32)[:, None] * inv_freq
    cos, sin = jnp.cos(ang), jnp.sin(ang)
    s = positions.shape[0]
    pad = jnp.zeros((s, LANES - B_NOPE - B_ROPE), F32)
    cos_q = jnp.concatenate([jnp.ones((s, B_NOPE), F32), cos, cos, pad], axis=1)
    sin_t = jnp.concatenate([jnp.zeros((s, B_NOPE), F32), -sin, sin, pad], axis=1)
    cos_k = jnp.concatenate([jnp.zeros((s, B_NOPE), F32), cos, cos, pad], axis=1)
    return cos_q, sin_t, cos_k, sin_t


def kernel(x, positions, ffn1_pre_g, ffn1_post_g, ffn1_w_gate, ffn1_w_up, ffn1_w_down, mix_pre_g, mix_post_g, w_in, rwkv_mu, rwkv_w0, rwkv_w_up, rwkv_a0, rwkv_a_up, rwkv_g_up, rwkv_k_k, rwkv_k_a, rwkv_r_k, rwkv_gn_g, rwkv_gn_b, rwkv_vres_down, rwkv_vres_mu, rwkv_vres_up, rwkv_v0, rwkv_out, mla_q_norm_g, mla_w_uq, mla_kv_norm_g, mla_w_ukv, mla_out, hgrn_lower_bounds, hgrn_norm_g, hgrn_out, w_o, ffn2_pre_g, ffn2_post_g, ffn2_w_gate, ffn2_w_up, ffn2_w_down):
    bsz, seq, d = x.shape
    assert bsz == 1 and d == D_MODEL and seq % max(ROW_TILE, C_CHUNK * C_CHUNKS_PER_STEP, A_CHUNK * A_CHUNKS_PER_STEP) == 0
    depth = w_in.shape[0]
    flash_tile = min(1024, seq)
    tabs = _rope_tables(positions[0])
    lb_p = jax.nn.softmax(hgrn_lower_bounds.astype(F32), axis=0)
    lower_bounds = jnp.cumsum(lb_p, axis=0) - lb_p[0]
    row = lambda t: t[None, :]
    bf = lambda t: t.astype(BF16)

    h = x[0]
    v_first = None
    for l in range(depth):
        h = _ffn(h, row(ffn1_pre_g[l]), bf(ffn1_w_gate[l]), bf(ffn1_w_up[l]), bf(ffn1_w_down[l]),
                 row(ffn1_post_g[l]))

        lw = _layer_weights(l, w_in, rwkv_mu, rwkv_vres_down, rwkv_vres_mu, mla_w_uq, mla_w_ukv, mla_out)
        pre_g = row(mix_pre_g[l])
        x_a = _proj(h, pre_g, lw["w_a"], lw["mu_a"])
        x_b = _proj(h, pre_g, lw["w_b"])
        x_c = _proj(h, pre_g, lw["w_c"])

        pa = dict(w0=row(rwkv_w0[l]), w_up=bf(_pad_rows(rwkv_w_up[l], LANES)), a0=row(rwkv_a0[l]),
                  a_up=bf(_pad_rows(rwkv_a_up[l], LANES)), g_up=bf(rwkv_g_up[l]), k_k=row(rwkv_k_k[l]),
                  k_a=row(rwkv_k_a[l]), r_k=row(rwkv_r_k[l].reshape(-1)), gn_g=row(rwkv_gn_g[l]),
                  gn_b=row(rwkv_gn_b[l]))
        if l > 0:
            pa.update(v0=row(rwkv_v0[l - 1]), vres_up=bf(_pad_rows(rwkv_vres_up[l - 1], LANES)))
        y_a, v_first, y_c = _recurrent_mixers(x_a, x_c, pa, v_first, row(lower_bounds[l]), row(hgrn_norm_g[l]))

        pb = dict(q_norm_g=row(mla_q_norm_g[l]), kv_norm_g=row(mla_kv_norm_g[l]), w_q=lw["w_q"],
                  w_q_swap=lw["w_q_swap"], w_k=lw["w_k"], w_v=lw["w_v"])
        q_h, k_h, vt_h = _mla_prep(x_b, tabs, pb)
        y_b = _flash(q_h, k_h, vt_h, flash_tile)

        h = _merge(h, y_a, y_b, y_c, pre_g, lw["w_g"], bf(rwkv_out[l]), lw["mla_out"], bf(hgrn_out[l]),
                   bf(w_o[l]), row(mix_post_g[l]))

        h = _ffn(h, row(ffn2_pre_g[l]), bf(ffn2_w_gate[l]), bf(ffn2_w_up[l]), bf(ffn2_w_down[l]),
                 row(ffn2_post_g[l]))
    return h[None]
```

```python
import functools

import jax
import jax.numpy as jnp
import numpy as np
from jax import lax
from jax.experimental import pallas as pl
from jax.experimental.pallas import tpu as pltpu

F32 = jnp.float32
BF16 = jnp.bfloat16

D_MODEL = 1024
D_FF = 2816
NORM_EPS = 1e-6
MACARON_WEIGHT = 0.5

A_HEADS = 8
A_HEAD_DIM = 64
A_WIDTH = 512
A_GN_EPS = 64e-5
A_CHUNK = 64
A_CHUNKS_PER_STEP = 4
A_COLS = 1792
A_IN = 2048

B_HEADS = 8
B_NOPE = 64
B_ROPE = 32
B_V = 64
B_Q_RANK = 384
B_KV_RANK = 256
B_IN = 896
ROPE_THETA = 10000.0
B_SUBTILES = 4
B_HEADS_PER_STEP = 2
LANES = 128

C_HEADS = 4
C_DIM = 128
C_WIDTH = 512
C_IN = 2048
C_CHUNK = 128
C_CHUNKS_PER_STEP = 2
C_SUB = 8
C_MIN_FORGET = 1e-6

N_BRANCH = 3
ROW_TILE = 512
FFN_ROW_TILE = 1024
VMEM_LIMIT = 56 * 1024 * 1024

assert C_SUB / 2 * -np.log(C_MIN_FORGET) < 80.0


def _bdot(a, b):
    return jnp.dot(a.astype(BF16), b.astype(BF16), preferred_element_type=F32)


def _bdot_nt(a, b):
    return lax.dot_general(a.astype(BF16), b.astype(BF16), (((1,), (1,)), ((), ())),
                           preferred_element_type=F32)


def _bmm(a, b):
    return lax.dot_general(a.astype(BF16), b.astype(BF16), (((2,), (1,)), ((0,), (0,))),
                           preferred_element_type=F32)


def _bmm_nt(a, b):
    return lax.dot_general(a.astype(BF16), b.astype(BF16), (((2,), (2,)), ((0,), (0,))),
                           preferred_element_type=F32)


def _split_terms(data, parts):
    terms, rem = [], data
    for _ in range(parts):
        piece = rem.astype(BF16)
        terms.append(piece)
        rem = rem - piece.astype(F32)
    return terms


def _sel_dot(sel, data, parts):
    sel = sel.astype(BF16)
    return sum(jnp.dot(sel, t, preferred_element_type=F32) for t in _split_terms(data, parts))


def _dot_sel(data, sel, parts):
    sel = sel.astype(BF16)
    return sum(jnp.dot(t, sel, preferred_element_type=F32) for t in _split_terms(data, parts))


def _rms(x, g):
    return x * lax.rsqrt(jnp.mean(x * x, axis=-1, keepdims=True) + NORM_EPS) * g


def _sigmoid(x):
    return jax.nn.sigmoid(x)


def _params(*sem):
    return pltpu.CompilerParams(dimension_semantics=sem, vmem_limit_bytes=VMEM_LIMIT)


def _ffn_body(x_ref, pre_g_ref, wg_ref, wu_ref, wd_ref, post_g_ref, o_ref):
    x = x_ref[...]
    xn = _rms(x, pre_g_ref[...]).astype(BF16)
    gate = jnp.dot(xn, wg_ref[...], preferred_element_type=F32)
    up = jnp.dot(xn, wu_ref[...], preferred_element_type=F32)
    mid = (gate * _sigmoid(gate) * up).astype(BF16)
    y = jnp.dot(mid, wd_ref[...], preferred_element_type=F32)
    o_ref[...] = x + MACARON_WEIGHT * _rms(y, post_g_ref[...])


def _ffn(h, pre_g, wg, wu, wd, post_g):
    s = h.shape[0]
    tm = FFN_ROW_TILE
    const = lambda a: pl.BlockSpec(a.shape, lambda i: (0,) * a.ndim)
    return pl.pallas_call(
        _ffn_body,
        grid=(s // tm,),
        in_specs=[pl.BlockSpec((tm, D_MODEL), lambda i: (i, 0)), const(pre_g), const(wg), const(wu), const(wd),
                  const(post_g)],
        out_specs=pl.BlockSpec((tm, D_MODEL), lambda i: (i, 0)),
        out_shape=jax.ShapeDtypeStruct((s, D_MODEL), F32),
        compiler_params=_params("arbitrary"),
        name="ffn",
    )(h, pre_g, wg, wu, wd, post_g)


def _proj_body(x_ref, g_ref, w_ref, o_ref):
    xn = _rms(x_ref[...], g_ref[...]).astype(BF16)
    o_ref[...] = jnp.dot(xn, w_ref[...], preferred_element_type=F32)


def _proj(h, pre_g, w):
    s = h.shape[0]
    n = w.shape[1]
    return pl.pallas_call(
        _proj_body, grid=(s // ROW_TILE,),
        in_specs=[pl.BlockSpec((ROW_TILE, D_MODEL), lambda i: (i, 0)), pl.BlockSpec((1, D_MODEL), lambda i: (0, 0)),
                  pl.BlockSpec((D_MODEL, n), lambda i: (0, 0))],
        out_specs=pl.BlockSpec((ROW_TILE, n), lambda i: (i, 0)),
        out_shape=jax.ShapeDtypeStruct((s, n), F32), compiler_params=_params("arbitrary"), name="proj",
    )(h, pre_g, w)


def _tri_inverse(a_strict, eye, blk16, lvl1, lvl2):
    d = a_strict * blk16
    d2 = _bmm(d, d)
    d4 = _bmm(d2, d2)
    d8 = _bmm(d4, d4)
    t = eye + d
    t = t + _bmm(t, d2)
    t = t + _bmm(t, d4)
    t = t + _bmm(t, d8)
    t = t + _bmm(_bmm(t, a_strict * lvl1), t)
    t = t + _bmm(_bmm(t, a_strict * lvl2), t)
    return t


def _rwkv_steps(has_vres, project, *refs):
    if has_vres:
        (vf_ref, w0_ref, wup_ref, a0_ref, aup_ref, gup_ref, kk_ref, ka_ref, rk_ref, gng_ref, gnb_ref,
         v0_ref, vup_ref, y_ref, state_ref) = refs
    else:
        (w0_ref, wup_ref, a0_ref, aup_ref, gup_ref, kk_ref, ka_ref, rk_ref, gng_ref, gnb_ref,
         y_ref, vf_out_ref, state_ref) = refs

    c = A_CHUNK
    w = A_WIDTH
    hd = A_HEAD_DIM
    x = project()
    yield
    rows = x.shape[0]
    r = x[:, 0:w]
    k = x[:, w:2 * w]
    v = x[:, 2 * w:3 * w]
    wl = x[:, 3 * w:3 * w + LANES]
    al = x[:, 3 * w + LANES:3 * w + 2 * LANES]
    gl = x[:, 3 * w + 2 * LANES:3 * w + 3 * LANES]

    z = w0_ref[...] + _bdot(jnp.tanh(wl), wup_ref[...])
    softplus_neg = jnp.maximum(-z, 0.0) + jnp.log(1.0 + jnp.exp(-jnp.abs(z)))
    logw = -jnp.exp(-softplus_neg - 0.5)
    a = _sigmoid(a0_ref[...] + _bdot(al, aup_ref[...]))
    g = _bdot(_sigmoid(gl), gup_ref[...])
    if has_vres:
        vl = x[:, 3 * w + 3 * LANES:3 * w + 4 * LANES]
        v = v + (vf_ref[...] - v) * _sigmoid(v0_ref[...] + _bdot(vl, vup_ref[...]))
    else:
        vf_out_ref[...] = v

    li = lax.broadcasted_iota(jnp.int32, (LANES, LANES), 0)
    lj = lax.broadcasted_iota(jnp.int32, (LANES, LANES), 1)
    pair_bd = ((li < hd) == (lj < hd)).astype(F32)
    lane = lax.broadcasted_iota(jnp.int32, (1, LANES), 1)
    m0 = (lane < hd).astype(F32)
    m1 = (lane >= hd).astype(F32)

    def head_sum(t):
        return jnp.concatenate([_dot_sel(t[:, p * LANES:(p + 1) * LANES], pair_bd, 1) for p in range(w // LANES)],
                               axis=1)

    def by_head(t):
        return jnp.concatenate([t * m0, t * m1], axis=0)

    yield
    kkr = k * kk_ref[...]
    kk = kkr / jnp.maximum(jnp.sqrt(head_sum(kkr * kkr)), 1e-12)
    k = k * (1.0 + (a - 1.0) * ka_ref[...])

    ti = lax.broadcasted_iota(jnp.int32, (2 * c, 2 * c), 0)
    tj = lax.broadcasted_iota(jnp.int32, (2 * c, 2 * c), 1)
    same_head = (ti // c) == (tj // c)
    incl = (same_head & (tj <= ti)).astype(F32)
    strict = (same_head & (tj < ti)).astype(F32)
    eye = (tj == ti).astype(F32)
    blk16 = ((ti // 16) == (tj // 16)).astype(F32)
    lvl1 = (((ti // 32) == (tj // 32)) & ((ti // 16) != (tj // 16))).astype(F32)
    lvl2 = (same_head & ((ti // 32) != (tj // 32))).astype(F32)

    ri = lax.broadcasted_iota(jnp.int32, (rows, rows), 0)
    rj = lax.broadcasted_iota(jnp.int32, (rows, rows), 1)
    chunk_incl = (((ri // c) == (rj // c)) & (rj <= ri)).astype(F32)
    lb = _sel_dot(chunk_incl, logw, 3)
    e_neg = jnp.exp(-lb)
    alpha_t = -kk * jnp.exp(lb - logw)
    beta = kk * a
    beta_h = beta * e_neg
    k_h = k * e_neg
    r_t = r * jnp.exp(lb)

    n_ch = rows // c
    n_pair = w // LANES

    def slabs(t):
        return [t[ci * c:(ci + 1) * c, p * LANES:(p + 1) * LANES] for ci in range(n_ch) for p in range(n_pair)]

    def stack_by_head(t):
        return jnp.stack([by_head(s) for s in slabs(t)])

    yield
    xa2, v2 = stack_by_head(alpha_t), stack_by_head(v)
    gram = _bmm_nt(jnp.concatenate([xa2, stack_by_head(r_t)], axis=1),
                   jnp.concatenate([stack_by_head(beta_h), stack_by_head(k_h)], axis=1))
    a_ab = gram[:, 0:2 * c, 0:2 * c] * strict
    a_ak = gram[:, 0:2 * c, 2 * c:4 * c] * strict
    a_r = jnp.concatenate([gram[:, 2 * c:4 * c, 2 * c:4 * c] * incl, gram[:, 2 * c:4 * c, 0:2 * c] * incl], axis=2)
    yield
    t = _tri_inverse(a_ab, eye, blk16, lvl1, lvl2)
    yield
    wu = _bmm(t, jnp.concatenate([xa2, _bmm(a_ak, v2)], axis=2))
    wt = wu[:, 0:c, 0:LANES] + wu[:, c:2 * c, 0:LANES]
    ut = wu[:, 0:c, LANES:2 * LANES] + wu[:, c:2 * c, LANES:2 * LANES]
    xr = jnp.stack(slabs(r_t))
    vp = jnp.stack(slabs(v))

    yield
    st = state_ref[...]
    y_chunks = []
    for ci in range(n_ch):
        bs = slice(ci * n_pair, (ci + 1) * n_pair)
        lb_last = lb[(ci + 1) * c - 1:(ci + 1) * c, :]
        e_end = jnp.exp(lb_last - lb[ci * c:(ci + 1) * c])
        k_e = k[ci * c:(ci + 1) * c] * e_end
        beta_e = beta[ci * c:(ci + 1) * c] * e_end
        gamma_c = jnp.exp(lb_last)
        u = _bmm_nt(wt[bs], st) + ut[bs]
        u2 = jnp.concatenate([u * m0, u * m1], axis=1)
        y2 = _bmm(a_r[bs], jnp.concatenate([v2[bs], u2], axis=1))
        y = _bmm_nt(xr[bs], st) + y2[:, 0:c] + y2[:, c:2 * c]
        y_chunks.append(jnp.concatenate([y[p] for p in range(n_pair)], axis=1))
        vu_t = jnp.stack([jnp.concatenate([vp[ci * n_pair + p], u[p]], axis=0).T for p in range(n_pair)])
        ke_be = jnp.stack([jnp.concatenate([k_e[:, p * LANES:(p + 1) * LANES], beta_e[:, p * LANES:(p + 1) * LANES]],
                                           axis=0) for p in range(n_pair)])
        decay = jnp.stack([gamma_c[:, p * LANES:(p + 1) * LANES] for p in range(n_pair)])
        st = st * decay + pair_bd * _bmm(vu_t, ke_be)
        yield
    state_ref[...] = st
    y = jnp.concatenate(y_chunks, axis=0)

    inv_n = 1.0 / hd
    mean = head_sum(y) * inv_n
    yc = y - mean
    var = head_sum(yc * yc) * inv_n
    yn = yc * lax.rsqrt(var + A_GN_EPS) * gng_ref[...] + gnb_ref[...]
    bonus = head_sum(r * k * rk_ref[...]) * v
    y_ref[...] = (yn + bonus) * g


def _mla_prep_body(x_ref, cq_ref, sq_ref, ck_ref, sk_ref, qg_ref, wq_ref, wqs_ref, kg_ref, wk_ref, wv_ref,
                   q_ref, k_ref, vt_ref):
    x = x_ref[...]
    cq = x[:, 0:B_Q_RANK]
    ckv = x[:, B_Q_RANK:B_Q_RANK + B_KV_RANK]
    kr = x[:, B_Q_RANK + B_KV_RANK:B_Q_RANK + B_KV_RANK + LANES]
    kr_sw = x[:, B_Q_RANK + B_KV_RANK + LANES:B_Q_RANK + B_KV_RANK + 2 * LANES]
    cqn = _rms(cq, qg_ref[...]).astype(BF16)
    ckn = _rms(ckv, kg_ref[...]).astype(BF16)
    q = jnp.dot(cqn, wq_ref[...], preferred_element_type=F32)
    q_sw = jnp.dot(cqn, wqs_ref[...], preferred_element_type=F32)
    k_nope = jnp.dot(ckn, wk_ref[...], preferred_element_type=F32)
    val = jnp.dot(ckn, wv_ref[...], preferred_element_type=F32)
    k_rope = kr * ck_ref[...] + kr_sw * sk_ref[...]
    cq_t, sq_t = cq_ref[...], sq_ref[...]
    ones_col = (lax.broadcasted_iota(jnp.int32, (1, LANES), 1) == B_V).astype(F32)
    for hd in range(B_HEADS):
        sl = slice(hd * LANES, (hd + 1) * LANES)
        q_ref[hd] = (q[:, sl] * cq_t + q_sw[:, sl] * sq_t).astype(BF16)
        k_ref[hd] = (k_nope[:, sl] + k_rope).astype(BF16)
        vt_ref[hd] = (val[:, sl] + ones_col).T.astype(BF16)


def _mla_prep(x, tabs, p):
    s = x.shape[0]
    tm = ROW_TILE
    row = lambda n: pl.BlockSpec((tm, n), lambda i: (i, 0))
    const = lambda a: pl.BlockSpec(a.shape, lambda i: (0,) * a.ndim)
    consts = [p["q_norm_g"], p["w_q"], p["w_q_swap"], p["kv_norm_g"], p["w_k"], p["w_v"]]
    head_spec = pl.BlockSpec((B_HEADS, tm, LANES), lambda i: (0, i, 0))
    head_shape = jax.ShapeDtypeStruct((B_HEADS, s, LANES), BF16)
    head_t_spec = pl.BlockSpec((B_HEADS, LANES, tm), lambda i: (0, 0, i))
    head_t_shape = jax.ShapeDtypeStruct((B_HEADS, LANES, s), BF16)
    return pl.pallas_call(
        _mla_prep_body, grid=(s // tm,),
        in_specs=[row(B_IN)] + [row(LANES)] * 4 + [const(a) for a in consts],
        out_specs=[head_spec, head_spec, head_t_spec], out_shape=[head_shape, head_shape, head_t_shape],
        compiler_params=_params("arbitrary"), name="mla_prep",
    )(x, *tabs, *consts)


def _flash_body(q_ref, qn_ref, k_ref, vt_ref, o_ref, m_ref, acc_ref, s0_ref, s1_ref, s2_ref):
    i = pl.program_id(1)
    tile = q_ref.shape[1]
    sub = tile // B_SUBTILES
    units = [(a, hh) for a in range(B_SUBTILES) for hh in range(B_HEADS_PER_STEP)]
    m_ref[...] = jnp.full_like(m_ref, -jnp.inf)
    acc_ref[...] = jnp.zeros_like(acc_ref)

    def scores(queries_ref, j, s_ref, a, hh):
        start = pl.multiple_of(j * tile, tile)
        s_ref[hh, :, a * sub:(a + 1) * sub] = lax.dot_general(
            k_ref[hh, pl.ds(start, tile), :], queries_ref[hh, a * sub:(a + 1) * sub, :], (((1,), (1,)), ((), ())),
            preferred_element_type=F32)

    def softmax_pv(j, s_ref, a, hh, on_diagonal):
        start = pl.multiple_of(j * tile, tile)
        cs = slice(a * sub, (a + 1) * sub)
        n_keys = (a + 1) * sub if on_diagonal else tile
        s = s_ref[hh, 0:n_keys, cs]
        if on_diagonal:
            keys = lax.broadcasted_iota(jnp.int32, s.shape, 0)
            queries = lax.broadcasted_iota(jnp.int32, s.shape, 1) + a * sub
            s = jnp.where(keys <= queries, s, -jnp.inf)
        m_old = m_ref[hh:hh + 1, cs]
        m_new = jnp.maximum(m_old, jnp.max(s, axis=0, keepdims=True))
        p = jnp.exp2(s - m_new).astype(BF16)
        pv = jnp.dot(vt_ref[hh, :, pl.ds(start, n_keys)], p, preferred_element_type=F32)
        acc_ref[hh, :, cs] = jnp.exp2(m_old - m_new) * acc_ref[hh, :, cs] + pv
        m_ref[hh:hh + 1, cs] = m_new

    def phase(j, cur_ref, nxt_ref):
        for a, hh in units:
            scores(q_ref, j + 1, nxt_ref, a, hh)
            softmax_pv(j, cur_ref, a, hh, False)

    def diagonal(cur_ref, overlap_next):
        for a, hh in units:
            if overlap_next:
                scores(qn_ref, 0, s2_ref, a, hh)
            softmax_pv(i, cur_ref, a, hh, True)
        for hh in range(B_HEADS_PER_STEP):
            acc = acc_ref[hh]
            o_ref[:, hh * LANES:(hh + 1) * LANES] = (acc / acc[B_V:B_V + 1, :]).T

    @pl.when(i == 0)
    def _():
        for a, hh in units:
            scores(q_ref, 0, s2_ref, a, hh)
        diagonal(s2_ref, False)
        for a, hh in units:
            scores(qn_ref, 0, s2_ref, a, hh)

    @pl.when(i > 0)
    def _():
        phase(0, s2_ref, s0_ref)

        def two_tiles(jj, carry):
            phase(2 * jj + 1, s0_ref, s1_ref)
            phase(2 * jj + 2, s1_ref, s0_ref)
            return carry

        lax.fori_loop(0, (i - 1) // 2, two_tiles, 0)

        @pl.when(i % 2 == 0)
        def _():
            phase(i - 1, s0_ref, s1_ref)
            diagonal(s1_ref, True)

        @pl.when(i % 2 == 1)
        def _():
            diagonal(s0_ref, True)


def _flash(q, k, vt, tile):
    n_h, s, _ = q.shape
    n_q = s // tile
    hs = B_HEADS_PER_STEP
    scores = pltpu.VMEM((hs, tile, tile), F32)
    resident = pl.Buffered(1)
    return pl.pallas_call(
        _flash_body, grid=(n_h // hs, n_q),
        in_specs=[pl.BlockSpec((hs, tile, LANES), lambda h, i: (h, i, 0)),
                  pl.BlockSpec((hs, tile, LANES), lambda h, i: (h, jnp.minimum(i + 1, n_q - 1), 0)),
                  pl.BlockSpec((hs, s, LANES), lambda h, i: (h, 0, 0), pipeline_mode=resident),
                  pl.BlockSpec((hs, LANES, s), lambda h, i: (h, 0, 0), pipeline_mode=resident)],
        out_specs=pl.BlockSpec((tile, hs * LANES), lambda h, i: (i, h)),
        out_shape=jax.ShapeDtypeStruct((s, n_h * LANES), F32),
        scratch_shapes=[pltpu.VMEM((hs, tile), F32), pltpu.VMEM((hs, LANES, tile), F32), scores, scores, scores],
        compiler_params=_params("arbitrary", "arbitrary"), name="mla_flash",
    )(q, q, k, vt)


def _hgrn_steps(project, lb_ref, ng_ref, o_ref, state_ref):
    c = C_CHUNK
    w = C_WIDTH
    x = project()
    yield
    rows = x.shape[0]
    n_ch = rows // c
    lower = lb_ref[...]
    fz = x[:, w:2 * w]
    f = lower + (1.0 - lower) * _sigmoid(fz)
    logf = jnp.log(jnp.maximum(f, C_MIN_FORGET))
    kf = (1.0 - lower) * _sigmoid(-fz)
    cq = x[:, 0:w]
    q = cq * _sigmoid(cq)
    val = x[:, 2 * w:3 * w]
    cg = x[:, 3 * w:4 * w]

    def heads(t):
        return jnp.stack([t[ci * c:(ci + 1) * c, hd * C_DIM:(hd + 1) * C_DIM]
                          for ci in range(n_ch) for hd in range(C_HEADS)])

    ri = lax.broadcasted_iota(jnp.int32, (rows, rows), 0)
    rj = lax.broadcasted_iota(jnp.int32, (rows, rows), 1)
    row = lax.broadcasted_iota(jnp.int32, (rows, 1), 0)
    ti = lax.broadcasted_iota(jnp.int32, (c, c), 0)
    tj = lax.broadcasted_iota(jnp.int32, (c, c), 1)
    b = _sel_dot((((ri // c) == (rj // c)) & (rj <= ri)).astype(F32), logf, 3)

    def block_row(size, offset):
        return jnp.concatenate([jnp.broadcast_to(b[r + offset:r + offset + 1, :], (size, w))
                                for r in range(0, rows, size)], axis=0)

    b_last = block_row(c, c - 1)

    yield
    n0 = C_SUB
    ref = block_row(n0, n0 // 2 - 1)
    mask = ((ti // n0) == (tj // n0)) & (tj <= ti)
    attn = jnp.where(mask, _bmm_nt(heads(q * jnp.exp(b - ref)), heads(kf * jnp.exp(ref - b))), 0.0)
    yield
    n = n0
    while n < c:
        ref = block_row(2 * n, n - 1)
        right = ((row // n) % 2) == 1
        decay = jnp.exp(jnp.where(right, b - ref, ref - b))
        mask = ((ti // (2 * n)) == (tj // (2 * n))) & (((ti // n) % 2) == 1) & (((tj // n) % 2) == 0)
        attn = jnp.where(mask, _bmm_nt(heads(q * decay), heads(kf * decay)), attn)
        n *= 2
        yield

    vals = heads(val)
    intra = _bmm(attn, vals)
    q_in = heads(q * jnp.exp(b))
    k_end = heads(kf * jnp.exp(b_last - b))
    g_end = jnp.exp(b_last)
    gates = cg * _sigmoid(cg)
    st = state_ref[...]
    out_chunks = []
    for ci in range(n_ch):
        bs = slice(ci * C_HEADS, (ci + 1) * C_HEADS)
        o = intra[bs] + _bmm_nt(q_in[bs], st)
        vt = jnp.stack([vals[ci * C_HEADS + hd].T for hd in range(C_HEADS)])
        decay = jnp.stack([g_end[(ci + 1) * c - 1:(ci + 1) * c, hd * C_DIM:(hd + 1) * C_DIM] for hd in range(C_HEADS)])
        st = st * decay + _bmm(vt, k_end[bs])
        out_chunks.append(jnp.concatenate([_rms(o[hd], ng_ref[...]) for hd in range(C_HEADS)], axis=1))
        yield
    state_ref[...] = st
    o_ref[...] = jnp.concatenate(out_chunks, axis=0) * gates


def _recurrent_body(has_vres, n_rwkv_in, *refs):
    n_rwkv_out = 1 if has_vres else 2
    (h_ref, g_ref, wa_ref, mu_ref, wc_ref), rest = refs[:5], refs[5:]
    rwkv_in, rest = rest[:n_rwkv_in], rest[n_rwkv_in:]
    hgrn_in, rest = rest[:2], rest[2:]
    rwkv_out, hgrn_out = rest[:n_rwkv_out], rest[n_rwkv_out]
    rwkv_state, hgrn_state, carry_ref = rest[n_rwkv_out + 1:]

    @pl.when(pl.program_id(0) == 0)
    def _():
        rwkv_state[...] = jnp.zeros_like(rwkv_state)
        hgrn_state[...] = jnp.zeros_like(hgrn_state)
        carry_ref[...] = jnp.zeros_like(carry_ref)

    xn = _rms(h_ref[...], g_ref[...]).astype(BF16)

    def project_shifted():
        p = jnp.dot(xn, wa_ref[...], preferred_element_type=F32)
        rows = p.shape[0]
        prev = pltpu.roll(p, 1, axis=0)
        first = lax.broadcasted_iota(jnp.int32, p.shape, 0) == 0
        prev = jnp.where(first, carry_ref[0:1, :], prev)
        carry_ref[0:1, :] = p[rows - 1:rows, :]
        return p + (prev - p) * mu_ref[...]

    def project_plain():
        return jnp.dot(xn, wc_ref[...], preferred_element_type=F32)

    rwkv = _rwkv_steps(has_vres, project_shifted, *rwkv_in, *rwkv_out, rwkv_state)
    hgrn = _hgrn_steps(project_plain, *hgrn_in, hgrn_out, hgrn_state)
    pending = [rwkv, hgrn]
    while pending:
        for steps in list(pending):
            if next(steps, StopIteration) is StopIteration:
                pending.remove(steps)


def _recurrent_mixers(h, pre_g, w_a, mu_a, w_c, p, v_first, lower, norm_g):
    s = h.shape[0]
    rows = A_CHUNK * A_CHUNKS_PER_STEP
    assert rows == C_CHUNK * C_CHUNKS_PER_STEP
    has_vres = v_first is not None
    row = lambda n: pl.BlockSpec((rows, n), lambda i: (i, 0))
    const = lambda a: pl.BlockSpec(a.shape, lambda i: (0,) * a.ndim)
    names = ["w0", "w_up", "a0", "a_up", "g_up", "k_k", "k_a", "r_k", "gn_g", "gn_b"]
    if has_vres:
        names += ["v0", "vres_up"]
    consts = [p[n] for n in names]
    proj_ins = [h, pre_g, w_a, mu_a, w_c]
    proj_specs = [row(D_MODEL)] + [const(a) for a in proj_ins[1:]]
    rwkv_ins = ([v_first] if has_vres else []) + consts
    rwkv_specs = ([row(A_WIDTH)] if has_vres else []) + [const(a) for a in consts]
    n_y = 1 if has_vres else 2
    y_shape = jax.ShapeDtypeStruct((s, A_WIDTH), F32)
    out = pl.pallas_call(
        functools.partial(_recurrent_body, has_vres, len(rwkv_ins)),
        grid=(s // rows,),
        in_specs=proj_specs + rwkv_specs + [const(lower), const(norm_g)],
        out_specs=[row(A_WIDTH)] * n_y + [row(C_WIDTH)],
        out_shape=[y_shape] * n_y + [jax.ShapeDtypeStruct((s, C_WIDTH), F32)],
        scratch_shapes=[pltpu.VMEM((A_WIDTH // LANES, LANES, LANES), F32), pltpu.VMEM((C_HEADS, C_DIM, C_DIM), F32),
                        pltpu.VMEM((8, A_IN), F32)],
        compiler_params=_params("arbitrary"), name="rwkv7_hgrn2",
    )(*proj_ins, *rwkv_ins, lower, norm_g)
    if has_vres:
        return out[0], v_first, out[1]
    return out[0], out[1], out[2]


def _merge_body(h_ref, ya_ref, yb_ref, yc_ref, g_ref, wg_ref, wa_ref, wb_ref, wc_ref, wo_ref, pg_ref, o_ref):
    d = D_MODEL
    h = h_ref[...]
    u = _rms(h, g_ref[...]).astype(BF16)
    merged = None
    for n, (y_ref, w_ref) in enumerate(((ya_ref, wa_ref), (yb_ref, wb_ref), (yc_ref, wc_ref))):
        gate = _sigmoid(jnp.dot(u, wg_ref[:, n * d:(n + 1) * d], preferred_element_type=F32))
        term = gate * _bdot(y_ref[...], w_ref[...])
        merged = term if merged is None else merged + term
    o_ref[...] = h + _rms(_bdot(merged, wo_ref[...]), pg_ref[...])


def _merge(h, ya, yb, yc, pre_g, wg, wa, wb, wc, wo, post_g):
    s = h.shape[0]
    tm = ROW_TILE
    row = lambda n: pl.BlockSpec((tm, n), lambda i: (i, 0))
    const = lambda a: pl.BlockSpec(a.shape, lambda i: (0,) * a.ndim)
    consts = [pre_g, wg, wa, wb, wc, wo, post_g]
    return pl.pallas_call(
        _merge_body, grid=(s // tm,),
        in_specs=[row(D_MODEL), row(A_WIDTH), row(B_HEADS * LANES), row(C_WIDTH)] + [const(a) for a in consts],
        out_specs=row(D_MODEL), out_shape=jax.ShapeDtypeStruct((s, D_MODEL), F32),
        compiler_params=_params("arbitrary"), name="merge",
    )(h, ya, yb, yc, *consts)


def _pad_cols(a, n):
    return jnp.pad(a, ((0, 0), (0, n - a.shape[1])))


def _pad_rows(a, n):
    return jnp.pad(a, ((0, n - a.shape[0]), (0, 0)))


def _rope_half_swap(a):
    half = B_ROPE // 2
    return jnp.concatenate([a[:, half:], a[:, :half]], axis=1)


def _layer_weights(l, w_in, rwkv_mu, vres_down, vres_mu, w_uq, w_ukv, mla_out):
    wi = w_in[l]
    d = D_MODEL
    o = 3 * A_WIDTH
    lora = [wi[:, o:o + 64], wi[:, o + 64:o + 128], wi[:, o + 128:o + 256]]
    mus = [rwkv_mu[l][o:o + 64], rwkv_mu[l][o + 64:o + 128], rwkv_mu[l][o + 128:o + 256]]
    if l > 0:
        lora.append(vres_down[l - 1])
        mus.append(vres_mu[l - 1])
    else:
        lora.append(jnp.zeros((d, 0), F32))
        mus.append(jnp.zeros((0,), F32))
    w_a = jnp.concatenate([wi[:, :o]] + [_pad_cols(t, LANES) for t in lora], axis=1)
    mu_a = jnp.concatenate([rwkv_mu[l][:o]] + [jnp.pad(t, (0, LANES - t.shape[0])) for t in mus])[None, :]

    o = A_COLS
    w_kr = wi[:, o + B_Q_RANK + B_KV_RANK:o + B_Q_RANK + B_KV_RANK + B_ROPE]
    place = lambda t: jnp.pad(t, ((0, 0), (B_NOPE, LANES - B_NOPE - B_ROPE)))
    w_b = jnp.concatenate([wi[:, o:o + B_Q_RANK + B_KV_RANK], place(w_kr), place(_rope_half_swap(w_kr))], axis=1)
    o += B_Q_RANK + B_KV_RANK + B_ROPE
    w_c = wi[:, o:o + C_IN]
    w_g = wi[:, o + C_IN:o + C_IN + N_BRANCH * d]

    scale = (B_NOPE + B_ROPE) ** -0.5 * np.log2(np.e)
    uq = (w_uq[l] * scale).reshape(B_Q_RANK, B_HEADS, B_NOPE + B_ROPE)
    uq_sw = jnp.concatenate([jnp.zeros_like(uq[..., :B_NOPE]), uq[..., B_NOPE + B_ROPE // 2:],
                             uq[..., B_NOPE:B_NOPE + B_ROPE // 2]], axis=-1)
    pad_head = lambda t: jnp.pad(t, ((0, 0), (0, 0), (0, LANES - t.shape[-1]))).reshape(t.shape[0], B_HEADS * LANES)
    ukv = w_ukv[l].reshape(B_KV_RANK, B_HEADS, B_NOPE + B_V)
    mo = jnp.pad(mla_out[l].reshape(B_HEADS, B_V, d), ((0, 0), (0, LANES - B_V), (0, 0))).reshape(B_HEADS * LANES, d)
    bf = lambda t: t.astype(BF16)
    return dict(w_a=bf(w_a), mu_a=mu_a, w_b=bf(w_b), w_c=bf(w_c), w_g=bf(w_g),
                w_q=bf(pad_head(uq)), w_q_swap=bf(pad_head(uq_sw)),
                w_k=bf(pad_head(ukv[..., :B_NOPE])), w_v=bf(pad_head(ukv[..., B_NOPE:])), mla_out=bf(mo))


def _rope_tables(positions):
    inv_freq = ROPE_THETA ** (-jnp.arange(0, B_ROPE, 2, dtype=F32) / B_ROPE)
    ang = positions.astype(F32)[:, None] * inv_freq
    cos, sin = jnp.cos(ang), jnp.sin(ang)
    s = positions.shape[0]
    pad = jnp.zeros((s, LANES - B_NOPE - B_ROPE), F32)
    cos_q = jnp.concatenate([jnp.ones((s, B_NOPE), F32), cos, cos, pad], axis=1)
    sin_t = jnp.concatenate([jnp.zeros((s, B_NOPE), F32), -sin, sin, pad], axis=1)
    cos_k = jnp.concatenate([jnp.zeros((s, B_NOPE), F32), cos, cos, pad], axis=1)
    return cos_q, sin_t, cos_k, sin_t


def kernel(x, positions, ffn1_pre_g, ffn1_post_g, ffn1_w_gate, ffn1_w_up, ffn1_w_down, mix_pre_g, mix_post_g, w_in, rwkv_mu, rwkv_w0, rwkv_w_up, rwkv_a0, rwkv_a_up, rwkv_g_up, rwkv_k_k, rwkv_k_a, rwkv_r_k, rwkv_gn_g, rwkv_gn_b, rwkv_vres_down, rwkv_vres_mu, rwkv_vres_up, rwkv_v0, rwkv_out, mla_q_norm_g, mla_w_uq, mla_kv_norm_g, mla_w_ukv, mla_out, hgrn_lower_bounds, hgrn_norm_g, hgrn_out, w_o, ffn2_pre_g, ffn2_post_g, ffn2_w_gate, ffn2_w_up, ffn2_w_down):
    bsz, seq, d = x.shape
    assert bsz == 1 and d == D_MODEL and seq % max(ROW_TILE, C_CHUNK * C_CHUNKS_PER_STEP, A_CHUNK * A_CHUNKS_PER_STEP) == 0
    depth = w_in.shape[0]
    flash_tile = min(1024, seq)
    tabs = _rope_tables(positions[0])
    lb_p = jax.nn.softmax(hgrn_lower_bounds.astype(F32), axis=0)
    lower_bounds = jnp.cumsum(lb_p, axis=0) - lb_p[0]
    row = lambda t: t[None, :]
    bf = lambda t: t.astype(BF16)

    h = x[0]
    v_first = None
    for l in range(depth):
        h = _ffn(h, row(ffn1_pre_g[l]), bf(ffn1_w_gate[l]), bf(ffn1_w_up[l]), bf(ffn1_w_down[l]),
                 row(ffn1_post_g[l]))

        lw = _layer_weights(l, w_in, rwkv_mu, rwkv_vres_down, rwkv_vres_mu, mla_w_uq, mla_w_ukv, mla_out)
        pre_g = row(mix_pre_g[l])
        x_b = _proj(h, pre_g, lw["w_b"])

        pa = dict(w0=row(rwkv_w0[l]), w_up=bf(_pad_rows(rwkv_w_up[l], LANES)), a0=row(rwkv_a0[l]),
                  a_up=bf(_pad_rows(rwkv_a_up[l], LANES)), g_up=bf(rwkv_g_up[l]), k_k=row(rwkv_k_k[l]),
                  k_a=row(rwkv_k_a[l]), r_k=row(rwkv_r_k[l].reshape(-1)), gn_g=row(rwkv_gn_g[l]),
                  gn_b=row(rwkv_gn_b[l]))
        if l > 0:
            pa.update(v0=row(rwkv_v0[l - 1]), vres_up=bf(_pad_rows(rwkv_vres_up[l - 1], LANES)))
        y_a, v_first, y_c = _recurrent_mixers(h, pre_g, lw["w_a"], lw["mu_a"], lw["w_c"], pa, v_first,
                                              row(lower_bounds[l]), row(hgrn_norm_g[l]))

        pb = dict(q_norm_g=row(mla_q_norm_g[l]), kv_norm_g=row(mla_kv_norm_g[l]), w_q=lw["w_q"],
                  w_q_swap=lw["w_q_swap"], w_k=lw["w_k"], w_v=lw["w_v"])
        q_h, k_h, vt_h = _mla_prep(x_b, tabs, pb)
        y_b = _flash(q_h, k_h, vt_h, flash_tile)

        h = _merge(h, y_a, y_b, y_c, pre_g, lw["w_g"], bf(rwkv_out[l]), lw["mla_out"], bf(hgrn_out[l]),
                   bf(w_o[l]), row(mix_post_g[l]))

        h = _ffn(h, row(ffn2_pre_g[l]), bf(ffn2_w_gate[l]), bf(ffn2_w_up[l]), bf(ffn2_w_down[l]),
                 row(ffn2_post_g[l]))
    return h[None]
```

```python
import functools

import jax
import jax.numpy as jnp
import numpy as np
from jax import lax
from jax.experimental import pallas as pl
from jax.experimental.pallas import tpu as pltpu

F32 = jnp.float32
BF16 = jnp.bfloat16

D_MODEL = 1024
D_FF = 2816
NORM_EPS = 1e-6
MACARON_WEIGHT = 0.5

A_HEADS = 8
A_HEAD_DIM = 64
A_WIDTH = 512
A_GN_EPS = 64e-5
A_CHUNK = 64
A_CHUNKS_PER_STEP = 4
A_COLS = 1792
A_IN = 2048

B_HEADS = 8
B_NOPE = 64
B_ROPE = 32
B_V = 64
B_Q_RANK = 384
B_KV_RANK = 256
B_IN = 896
ROPE_THETA = 10000.0
B_SUBTILES = 4
B_HEADS_PER_STEP = 2
LANES = 128

C_HEADS = 4
C_DIM = 128
C_WIDTH = 512
C_IN = 2048
C_CHUNK = 128
C_CHUNKS_PER_STEP = 2
C_SUB = 8
C_MIN_FORGET = 1e-6

N_BRANCH = 3
ROW_TILE = 512
FFN_ROW_TILE = 1024
VMEM_LIMIT = 56 * 1024 * 1024

assert C_SUB / 2 * -np.log(C_MIN_FORGET) < 80.0


def _bdot(a, b):
    return jnp.dot(a.astype(BF16), b.astype(BF16), preferred_element_type=F32)


def _bdot_nt(a, b):
    return lax.dot_general(a.astype(BF16), b.astype(BF16), (((1,), (1,)), ((), ())),
                           preferred_element_type=F32)


def _bmm(a, b):
    return lax.dot_general(a.astype(BF16), b.astype(BF16), (((2,), (1,)), ((0,), (0,))),
                           preferred_element_type=F32)


def _bmm_nt(a, b):
    return lax.dot_general(a.astype(BF16), b.astype(BF16), (((2,), (2,)), ((0,), (0,))),
                           preferred_element_type=F32)


def _split_terms(data, parts):
    terms, rem = [], data
    for _ in range(parts):
        piece = rem.astype(BF16)
        terms.append(piece)
        rem = rem - piece.astype(F32)
    return terms


def _sel_dot(sel, data, parts):
    sel = sel.astype(BF16)
    return sum(jnp.dot(sel, t, preferred_element_type=F32) for t in _split_terms(data, parts))


def _dot_sel(data, sel, parts):
    sel = sel.astype(BF16)
    return sum(jnp.dot(t, sel, preferred_element_type=F32) for t in _split_terms(data, parts))


def _rms(x, g):
    return x * lax.rsqrt(jnp.mean(x * x, axis=-1, keepdims=True) + NORM_EPS) * g


def _sigmoid(x):
    return jax.nn.sigmoid(x)


def _params(*sem):
    return pltpu.CompilerParams(dimension_semantics=sem, vmem_limit_bytes=VMEM_LIMIT)


def _ffn_body(x_ref, pre_g_ref, wg_ref, wu_ref, wd_ref, post_g_ref, o_ref):
    x = x_ref[...]
    xn = _rms(x, pre_g_ref[...]).astype(BF16)
    gate = jnp.dot(xn, wg_ref[...], preferred_element_type=F32)
    up = jnp.dot(xn, wu_ref[...], preferred_element_type=F32)
    mid = (gate * _sigmoid(gate) * up).astype(BF16)
    y = jnp.dot(mid, wd_ref[...], preferred_element_type=F32)
    o_ref[...] = x + MACARON_WEIGHT * _rms(y, post_g_ref[...])


def _ffn(h, pre_g, wg, wu, wd, post_g):
    s = h.shape[0]
    tm = FFN_ROW_TILE
    const = lambda a: pl.BlockSpec(a.shape, lambda i: (0,) * a.ndim)
    return pl.pallas_call(
        _ffn_body,
        grid=(s // tm,),
        in_specs=[pl.BlockSpec((tm, D_MODEL), lambda i: (i, 0)), const(pre_g), const(wg), const(wu), const(wd),
                  const(post_g)],
        out_specs=pl.BlockSpec((tm, D_MODEL), lambda i: (i, 0)),
        out_shape=jax.ShapeDtypeStruct((s, D_MODEL), F32),
        compiler_params=_params("arbitrary"),
        name="ffn",
    )(h, pre_g, wg, wu, wd, post_g)


def _tri_inverse(a_strict, eye, blk16, lvl1, lvl2):
    d = a_strict * blk16
    d2 = _bmm(d, d)
    d4 = _bmm(d2, d2)
    d8 = _bmm(d4, d4)
    t = eye + d
    t = t + _bmm(t, d2)
    t = t + _bmm(t, d4)
    t = t + _bmm(t, d8)
    t = t + _bmm(_bmm(t, a_strict * lvl1), t)
    t = t + _bmm(_bmm(t, a_strict * lvl2), t)
    return t


def _rwkv_steps(has_vres, project, *refs):
    if has_vres:
        (vf_ref, w0_ref, wup_ref, a0_ref, aup_ref, gup_ref, kk_ref, ka_ref, rk_ref, gng_ref, gnb_ref,
         v0_ref, vup_ref, y_ref, state_ref) = refs
    else:
        (w0_ref, wup_ref, a0_ref, aup_ref, gup_ref, kk_ref, ka_ref, rk_ref, gng_ref, gnb_ref,
         y_ref, vf_out_ref, state_ref) = refs

    c = A_CHUNK
    w = A_WIDTH
    hd = A_HEAD_DIM
    x = project()
    yield
    rows = x.shape[0]
    r = x[:, 0:w]
    k = x[:, w:2 * w]
    v = x[:, 2 * w:3 * w]
    wl = x[:, 3 * w:3 * w + LANES]
    al = x[:, 3 * w + LANES:3 * w + 2 * LANES]
    gl = x[:, 3 * w + 2 * LANES:3 * w + 3 * LANES]

    z = w0_ref[...] + _bdot(jnp.tanh(wl), wup_ref[...])
    softplus_neg = jnp.maximum(-z, 0.0) + jnp.log(1.0 + jnp.exp(-jnp.abs(z)))
    logw = -jnp.exp(-softplus_neg - 0.5)
    a = _sigmoid(a0_ref[...] + _bdot(al, aup_ref[...]))
    g = _bdot(_sigmoid(gl), gup_ref[...])
    if has_vres:
        vl = x[:, 3 * w + 3 * LANES:3 * w + 4 * LANES]
        v = v + (vf_ref[...] - v) * _sigmoid(v0_ref[...] + _bdot(vl, vup_ref[...]))
    else:
        vf_out_ref[...] = v

    li = lax.broadcasted_iota(jnp.int32, (LANES, LANES), 0)
    lj = lax.broadcasted_iota(jnp.int32, (LANES, LANES), 1)
    pair_bd = ((li < hd) == (lj < hd)).astype(F32)
    lane = lax.broadcasted_iota(jnp.int32, (1, LANES), 1)
    m0 = (lane < hd).astype(F32)
    m1 = (lane >= hd).astype(F32)

    def head_sum(t):
        return jnp.concatenate([_dot_sel(t[:, p * LANES:(p + 1) * LANES], pair_bd, 1) for p in range(w // LANES)],
                               axis=1)

    def by_head(t):
        return jnp.concatenate([t * m0, t * m1], axis=0)

    yield
    kkr = k * kk_ref[...]
    kk = kkr / jnp.maximum(jnp.sqrt(head_sum(kkr * kkr)), 1e-12)
    k = k * (1.0 + (a - 1.0) * ka_ref[...])

    ti = lax.broadcasted_iota(jnp.int32, (2 * c, 2 * c), 0)
    tj = lax.broadcasted_iota(jnp.int32, (2 * c, 2 * c), 1)
    same_head = (ti // c) == (tj // c)
    incl = (same_head & (tj <= ti)).astype(F32)
    strict = (same_head & (tj < ti)).astype(F32)
    eye = (tj == ti).astype(F32)
    blk16 = ((ti // 16) == (tj // 16)).astype(F32)
    lvl1 = (((ti // 32) == (tj // 32)) & ((ti // 16) != (tj // 16))).astype(F32)
    lvl2 = (same_head & ((ti // 32) != (tj // 32))).astype(F32)

    ri = lax.broadcasted_iota(jnp.int32, (rows, rows), 0)
    rj = lax.broadcasted_iota(jnp.int32, (rows, rows), 1)
    chunk_incl = (((ri // c) == (rj // c)) & (rj <= ri)).astype(F32)
    lb = _sel_dot(chunk_incl, logw, 3)
    e_neg = jnp.exp(-lb)
    alpha_t = -kk * jnp.exp(lb - logw)
    beta = kk * a
    beta_h = beta * e_neg
    k_h = k * e_neg
    r_t = r * jnp.exp(lb)

    n_ch = rows // c
    n_pair = w // LANES

    def slabs(t):
        return [t[ci * c:(ci + 1) * c, p * LANES:(p + 1) * LANES] for ci in range(n_ch) for p in range(n_pair)]

    def stack_by_head(t):
        return jnp.stack([by_head(s) for s in slabs(t)])

    yield
    xa2, v2 = stack_by_head(alpha_t), stack_by_head(v)
    gram = _bmm_nt(jnp.concatenate([xa2, stack_by_head(r_t)], axis=1),
                   jnp.concatenate([stack_by_head(beta_h), stack_by_head(k_h)], axis=1))
    a_ab = gram[:, 0:2 * c, 0:2 * c] * strict
    a_ak = gram[:, 0:2 * c, 2 * c:4 * c] * strict
    a_r = jnp.concatenate([gram[:, 2 * c:4 * c, 2 * c:4 * c] * incl, gram[:, 2 * c:4 * c, 0:2 * c] * incl], axis=2)
    yield
    t = _tri_inverse(a_ab, eye, blk16, lvl1, lvl2)
    yield
    wu = _bmm(t, jnp.concatenate([xa2, _bmm(a_ak, v2)], axis=2))
    wt = wu[:, 0:c, 0:LANES] + wu[:, c:2 * c, 0:LANES]
    ut = wu[:, 0:c, LANES:2 * LANES] + wu[:, c:2 * c, LANES:2 * LANES]
    xr = jnp.stack(slabs(r_t))
    vp = jnp.stack(slabs(v))

    yield
    st = state_ref[...]
    y_chunks = []
    for ci in range(n_ch):
        bs = slice(ci * n_pair, (ci + 1) * n_pair)
        lb_last = lb[(ci + 1) * c - 1:(ci + 1) * c, :]
        e_end = jnp.exp(lb_last - lb[ci * c:(ci + 1) * c])
        k_e = k[ci * c:(ci + 1) * c] * e_end
        beta_e = beta[ci * c:(ci + 1) * c] * e_end
        gamma_c = jnp.exp(lb_last)
        u = _bmm_nt(wt[bs], st) + ut[bs]
        u2 = jnp.concatenate([u * m0, u * m1], axis=1)
        y2 = _bmm(a_r[bs], jnp.concatenate([v2[bs], u2], axis=1))
        y = _bmm_nt(xr[bs], st) + y2[:, 0:c] + y2[:, c:2 * c]
        y_chunks.append(jnp.concatenate([y[p] for p in range(n_pair)], axis=1))
        vu_t = jnp.stack([jnp.concatenate([vp[ci * n_pair + p], u[p]], axis=0).T for p in range(n_pair)])
        ke_be = jnp.stack([jnp.concatenate([k_e[:, p * LANES:(p + 1) * LANES], beta_e[:, p * LANES:(p + 1) * LANES]],
                                           axis=0) for p in range(n_pair)])
        decay = jnp.stack([gamma_c[:, p * LANES:(p + 1) * LANES] for p in range(n_pair)])
        st = st * decay + pair_bd * _bmm(vu_t, ke_be)
        yield
    state_ref[...] = st
    y = jnp.concatenate(y_chunks, axis=0)

    inv_n = 1.0 / hd
    mean = head_sum(y) * inv_n
    yc = y - mean
    var = head_sum(yc * yc) * inv_n
    yn = yc * lax.rsqrt(var + A_GN_EPS) * gng_ref[...] + gnb_ref[...]
    bonus = head_sum(r * k * rk_ref[...]) * v
    y_ref[...] = (yn + bonus) * g


def _mla_steps(project, cq_ref, sq_ref, ck_ref, sk_ref, qg_ref, wq_ref, wqs_ref, kg_ref, wk_ref, wv_ref,
               q_ref, k_ref, vt_ref):
    x = project()
    yield
    cq = x[:, 0:B_Q_RANK]
    ckv = x[:, B_Q_RANK:B_Q_RANK + B_KV_RANK]
    kr = x[:, B_Q_RANK + B_KV_RANK:B_Q_RANK + B_KV_RANK + LANES]
    kr_sw = x[:, B_Q_RANK + B_KV_RANK + LANES:B_Q_RANK + B_KV_RANK + 2 * LANES]
    cqn = _rms(cq, qg_ref[...]).astype(BF16)
    ckn = _rms(ckv, kg_ref[...]).astype(BF16)
    q = jnp.dot(cqn, wq_ref[...], preferred_element_type=F32)
    q_sw = jnp.dot(cqn, wqs_ref[...], preferred_element_type=F32)
    yield
    k_nope = jnp.dot(ckn, wk_ref[...], preferred_element_type=F32)
    val = jnp.dot(ckn, wv_ref[...], preferred_element_type=F32)
    k_rope = kr * ck_ref[...] + kr_sw * sk_ref[...]
    cq_t, sq_t = cq_ref[...], sq_ref[...]
    ones_col = (lax.broadcasted_iota(jnp.int32, (1, LANES), 1) == B_V).astype(F32)
    yield
    for hd in range(B_HEADS):
        sl = slice(hd * LANES, (hd + 1) * LANES)
        q_ref[hd] = (q[:, sl] * cq_t + q_sw[:, sl] * sq_t).astype(BF16)
        k_ref[hd] = (k_nope[:, sl] + k_rope).astype(BF16)
        vt_ref[hd] = (val[:, sl] + ones_col).T.astype(BF16)
        if hd % 2 == 1:
            yield


def _flash_body(q_ref, qn_ref, k_ref, vt_ref, o_ref, m_ref, acc_ref, s0_ref, s1_ref, s2_ref):
    i = pl.program_id(1)
    tile = q_ref.shape[1]
    sub = tile // B_SUBTILES
    units = [(a, hh) for a in range(B_SUBTILES) for hh in range(B_HEADS_PER_STEP)]
    m_ref[...] = jnp.full_like(m_ref, -jnp.inf)
    acc_ref[...] = jnp.zeros_like(acc_ref)

    def scores(queries_ref, j, s_ref, a, hh):
        start = pl.multiple_of(j * tile, tile)
        s_ref[hh, :, a * sub:(a + 1) * sub] = lax.dot_general(
            k_ref[hh, pl.ds(start, tile), :], queries_ref[hh, a * sub:(a + 1) * sub, :], (((1,), (1,)), ((), ())),
            preferred_element_type=F32)

    def softmax_pv(j, s_ref, a, hh, on_diagonal):
        start = pl.multiple_of(j * tile, tile)
        cs = slice(a * sub, (a + 1) * sub)
        n_keys = (a + 1) * sub if on_diagonal else tile
        s = s_ref[hh, 0:n_keys, cs]
        if on_diagonal:
            keys = lax.broadcasted_iota(jnp.int32, s.shape, 0)
            queries = lax.broadcasted_iota(jnp.int32, s.shape, 1) + a * sub
            s = jnp.where(keys <= queries, s, -jnp.inf)
        m_old = m_ref[hh:hh + 1, cs]
        m_new = jnp.maximum(m_old, jnp.max(s, axis=0, keepdims=True))
        p = jnp.exp2(s - m_new).astype(BF16)
        pv = jnp.dot(vt_ref[hh, :, pl.ds(start, n_keys)], p, preferred_element_type=F32)
        acc_ref[hh, :, cs] = jnp.exp2(m_old - m_new) * acc_ref[hh, :, cs] + pv
        m_ref[hh:hh + 1, cs] = m_new

    def phase(j, cur_ref, nxt_ref):
        for a, hh in units:
            scores(q_ref, j + 1, nxt_ref, a, hh)
            softmax_pv(j, cur_ref, a, hh, False)

    def diagonal(cur_ref, overlap_next):
        for a, hh in units:
            if overlap_next:
                scores(qn_ref, 0, s2_ref, a, hh)
            softmax_pv(i, cur_ref, a, hh, True)
        for hh in range(B_HEADS_PER_STEP):
            acc = acc_ref[hh]
            o_ref[:, hh * LANES:(hh + 1) * LANES] = (acc / acc[B_V:B_V + 1, :]).T

    @pl.when(i == 0)
    def _():
        for a, hh in units:
            scores(q_ref, 0, s2_ref, a, hh)
        diagonal(s2_ref, False)
        for a, hh in units:
            scores(qn_ref, 0, s2_ref, a, hh)

    @pl.when(i > 0)
    def _():
        phase(0, s2_ref, s0_ref)

        def two_tiles(jj, carry):
            phase(2 * jj + 1, s0_ref, s1_ref)
            phase(2 * jj + 2, s1_ref, s0_ref)
            return carry

        lax.fori_loop(0, (i - 1) // 2, two_tiles, 0)

        @pl.when(i % 2 == 0)
        def _():
            phase(i - 1, s0_ref, s1_ref)
            diagonal(s1_ref, True)

        @pl.when(i % 2 == 1)
        def _():
            diagonal(s0_ref, True)


def _flash(q, k, vt, tile):
    n_h, s, _ = q.shape
    n_q = s // tile
    hs = B_HEADS_PER_STEP
    scores = pltpu.VMEM((hs, tile, tile), F32)
    resident = pl.Buffered(1)
    return pl.pallas_call(
        _flash_body, grid=(n_h // hs, n_q),
        in_specs=[pl.BlockSpec((hs, tile, LANES), lambda h, i: (h, i, 0)),
                  pl.BlockSpec((hs, tile, LANES), lambda h, i: (h, jnp.minimum(i + 1, n_q - 1), 0)),
                  pl.BlockSpec((hs, s, LANES), lambda h, i: (h, 0, 0), pipeline_mode=resident),
                  pl.BlockSpec((hs, LANES, s), lambda h, i: (h, 0, 0), pipeline_mode=resident)],
        out_specs=pl.BlockSpec((tile, hs * LANES), lambda h, i: (i, h)),
        out_shape=jax.ShapeDtypeStruct((s, n_h * LANES), F32),
        scratch_shapes=[pltpu.VMEM((hs, tile), F32), pltpu.VMEM((hs, LANES, tile), F32), scores, scores, scores],
        compiler_params=_params("arbitrary", "arbitrary"), name="mla_flash",
    )(q, q, k, vt)


def _hgrn_steps(project, lb_ref, ng_ref, o_ref, state_ref):
    c = C_CHUNK
    w = C_WIDTH
    x = project()
    yield
    rows = x.shape[0]
    n_ch = rows // c
    lower = lb_ref[...]
    fz = x[:, w:2 * w]
    f = lower + (1.0 - lower) * _sigmoid(fz)
    logf = jnp.log(jnp.maximum(f, C_MIN_FORGET))
    kf = (1.0 - lower) * _sigmoid(-fz)
    cq = x[:, 0:w]
    q = cq * _sigmoid(cq)
    val = x[:, 2 * w:3 * w]
    cg = x[:, 3 * w:4 * w]

    def heads(t):
        return jnp.stack([t[ci * c:(ci + 1) * c, hd * C_DIM:(hd + 1) * C_DIM]
                          for ci in range(n_ch) for hd in range(C_HEADS)])

    ri = lax.broadcasted_iota(jnp.int32, (rows, rows), 0)
    rj = lax.broadcasted_iota(jnp.int32, (rows, rows), 1)
    row = lax.broadcasted_iota(jnp.int32, (rows, 1), 0)
    ti = lax.broadcasted_iota(jnp.int32, (c, c), 0)
    tj = lax.broadcasted_iota(jnp.int32, (c, c), 1)
    b = _sel_dot((((ri // c) == (rj // c)) & (rj <= ri)).astype(F32), logf, 3)

    def block_row(size, offset):
        return jnp.concatenate([jnp.broadcast_to(b[r + offset:r + offset + 1, :], (size, w))
                                for r in range(0, rows, size)], axis=0)

    b_last = block_row(c, c - 1)

    yield
    n0 = C_SUB
    ref = block_row(n0, n0 // 2 - 1)
    mask = ((ti // n0) == (tj // n0)) & (tj <= ti)
    attn = jnp.where(mask, _bmm_nt(heads(q * jnp.exp(b - ref)), heads(kf * jnp.exp(ref - b))), 0.0)
    yield
    n = n0
    while n < c:
        ref = block_row(2 * n, n - 1)
        right = ((row // n) % 2) == 1
        decay = jnp.exp(jnp.where(right, b - ref, ref - b))
        mask = ((ti // (2 * n)) == (tj // (2 * n))) & (((ti // n) % 2) == 1) & (((tj // n) % 2) == 0)
        attn = jnp.where(mask, _bmm_nt(heads(q * decay), heads(kf * decay)), attn)
        n *= 2
        yield

    vals = heads(val)
    intra = _bmm(attn, vals)
    q_in = heads(q * jnp.exp(b))
    k_end = heads(kf * jnp.exp(b_last - b))
    g_end = jnp.exp(b_last)
    gates = cg * _sigmoid(cg)
    st = state_ref[...]
    out_chunks = []
    for ci in range(n_ch):
        bs = slice(ci * C_HEADS, (ci + 1) * C_HEADS)
        o = intra[bs] + _bmm_nt(q_in[bs], st)
        vt = jnp.stack([vals[ci * C_HEADS + hd].T for hd in range(C_HEADS)])
        decay = jnp.stack([g_end[(ci + 1) * c - 1:(ci + 1) * c, hd * C_DIM:(hd + 1) * C_DIM] for hd in range(C_HEADS)])
        st = st * decay + _bmm(vt, k_end[bs])
        out_chunks.append(jnp.concatenate([_rms(o[hd], ng_ref[...]) for hd in range(C_HEADS)], axis=1))
        yield
    state_ref[...] = st
    o_ref[...] = jnp.concatenate(out_chunks, axis=0) * gates


N_MLA_IN = 10


def _mixers_body(has_vres, n_rwkv_in, *refs):
    n_rwkv_out = 1 if has_vres else 2
    (h_ref, g_ref, wa_ref, mu_ref, wc_ref, wb_ref), rest = refs[:6], refs[6:]
    rwkv_in, rest = rest[:n_rwkv_in], rest[n_rwkv_in:]
    hgrn_in, rest = rest[:2], rest[2:]
    mla_in, rest = rest[:N_MLA_IN], rest[N_MLA_IN:]
    rwkv_out, hgrn_out, mla_out = rest[:n_rwkv_out], rest[n_rwkv_out], rest[n_rwkv_out + 1:n_rwkv_out + 4]
    rwkv_state, hgrn_state, carry_ref = rest[n_rwkv_out + 4:]

    @pl.when(pl.program_id(0) == 0)
    def _():
        rwkv_state[...] = jnp.zeros_like(rwkv_state)
        hgrn_state[...] = jnp.zeros_like(hgrn_state)
        carry_ref[...] = jnp.zeros_like(carry_ref)

    xn = _rms(h_ref[...], g_ref[...]).astype(BF16)

    def project_shifted():
        p = jnp.dot(xn, wa_ref[...], preferred_element_type=F32)
        rows = p.shape[0]
        prev = pltpu.roll(p, 1, axis=0)
        first = lax.broadcasted_iota(jnp.int32, p.shape, 0) == 0
        prev = jnp.where(first, carry_ref[0:1, :], prev)
        carry_ref[0:1, :] = p[rows - 1:rows, :]
        return p + (prev - p) * mu_ref[...]

    def project(w_ref):
        return lambda: jnp.dot(xn, w_ref[...], preferred_element_type=F32)

    pending = [_rwkv_steps(has_vres, project_shifted, *rwkv_in, *rwkv_out, rwkv_state),
               _hgrn_steps(project(wc_ref), *hgrn_in, hgrn_out, hgrn_state),
               _mla_steps(project(wb_ref), *mla_in, *mla_out)]
    while pending:
        for steps in list(pending):
            if next(steps, StopIteration) is StopIteration:
                pending.remove(steps)


def _mixers_front(h, pre_g, lw, p, v_first, lower, norm_g, tabs, pb):
    s = h.shape[0]
    rows = A_CHUNK * A_CHUNKS_PER_STEP
    assert rows == C_CHUNK * C_CHUNKS_PER_STEP
    has_vres = v_first is not None
    row = lambda n: pl.BlockSpec((rows, n), lambda i: (i, 0))
    const = lambda a: pl.BlockSpec(a.shape, lambda i: (0,) * a.ndim)
    names = ["w0", "w_up", "a0", "a_up", "g_up", "k_k", "k_a", "r_k", "gn_g", "gn_b"]
    if has_vres:
        names += ["v0", "vres_up"]
    consts = [p[n] for n in names]
    proj_ins = [h, pre_g, lw["w_a"], lw["mu_a"], lw["w_c"], lw["w_b"]]
    proj_specs = [row(D_MODEL)] + [const(a) for a in proj_ins[1:]]
    rwkv_ins = ([v_first] if has_vres else []) + consts
    rwkv_specs = ([row(A_WIDTH)] if has_vres else []) + [const(a) for a in consts]
    mla_consts = [pb["q_norm_g"], pb["w_q"], pb["w_q_swap"], pb["kv_norm_g"], pb["w_k"], pb["w_v"]]
    assert len(tabs) + len(mla_consts) == N_MLA_IN
    n_y = 1 if has_vres else 2
    y_shape = jax.ShapeDtypeStruct((s, A_WIDTH), F32)
    head_spec = pl.BlockSpec((B_HEADS, rows, LANES), lambda i: (0, i, 0))
    head_shape = jax.ShapeDtypeStruct((B_HEADS, s, LANES), BF16)
    head_t_spec = pl.BlockSpec((B_HEADS, LANES, rows), lambda i: (0, 0, i))
    head_t_shape = jax.ShapeDtypeStruct((B_HEADS, LANES, s), BF16)
    out = pl.pallas_call(
        functools.partial(_mixers_body, has_vres, len(rwkv_ins)),
        grid=(s // rows,),
        in_specs=(proj_specs + rwkv_specs + [const(lower), const(norm_g)]
                  + [row(LANES)] * len(tabs) + [const(a) for a in mla_consts]),
        out_specs=[row(A_WIDTH)] * n_y + [row(C_WIDTH), head_spec, head_spec, head_t_spec],
        out_shape=[y_shape] * n_y + [jax.ShapeDtypeStruct((s, C_WIDTH), F32), head_shape, head_shape, head_t_shape],
        scratch_shapes=[pltpu.VMEM((A_WIDTH // LANES, LANES, LANES), F32), pltpu.VMEM((C_HEADS, C_DIM, C_DIM), F32),
                        pltpu.VMEM((8, A_IN), F32)],
        compiler_params=_params("arbitrary"), name="mixers_front",
    )(*proj_ins, *rwkv_ins, lower, norm_g, *tabs, *mla_consts)
    if has_vres:
        return (out[0], v_first) + tuple(out[1:])
    return tuple(out)


def _merge_body(h_ref, ya_ref, yb_ref, yc_ref, g_ref, wg_ref, wa_ref, wb_ref, wc_ref, wo_ref, pg_ref, o_ref):
    d = D_MODEL
    h = h_ref[...]
    u = _rms(h, g_ref[...]).astype(BF16)
    merged = None
    for n, (y_ref, w_ref) in enumerate(((ya_ref, wa_ref), (yb_ref, wb_ref), (yc_ref, wc_ref))):
        gate = _sigmoid(jnp.dot(u, wg_ref[:, n * d:(n + 1) * d], preferred_element_type=F32))
        term = gate * _bdot(y_ref[...], w_ref[...])
        merged = term if merged is None else merged + term
    o_ref[...] = h + _rms(_bdot(merged, wo_ref[...]), pg_ref[...])


def _merge(h, ya, yb, yc, pre_g, wg, wa, wb, wc, wo, post_g):
    s = h.shape[0]
    tm = ROW_TILE
    row = lambda n: pl.BlockSpec((tm, n), lambda i: (i, 0))
    const = lambda a: pl.BlockSpec(a.shape, lambda i: (0,) * a.ndim)
    consts = [pre_g, wg, wa, wb, wc, wo, post_g]
    return pl.pallas_call(
        _merge_body, grid=(s // tm,),
        in_specs=[row(D_MODEL), row(A_WIDTH), row(B_HEADS * LANES), row(C_WIDTH)] + [const(a) for a in consts],
        out_specs=row(D_MODEL), out_shape=jax.ShapeDtypeStruct((s, D_MODEL), F32),
        compiler_params=_params("arbitrary"), name="merge",
    )(h, ya, yb, yc, *consts)


def _pad_cols(a, n):
    return jnp.pad(a, ((0, 0), (0, n - a.shape[1])))


def _pad_rows(a, n):
    return jnp.pad(a, ((0, n - a.shape[0]), (0, 0)))


def _rope_half_swap(a):
    half = B_ROPE // 2
    return jnp.concatenate([a[:, half:], a[:, :half]], axis=1)


def _layer_weights(l, w_in, rwkv_mu, vres_down, vres_mu, w_uq, w_ukv, mla_out):
    wi = w_in[l]
    d = D_MODEL
    o = 3 * A_WIDTH
    lora = [wi[:, o:o + 64], wi[:, o + 64:o + 128], wi[:, o + 128:o + 256]]
    mus = [rwkv_mu[l][o:o + 64], rwkv_mu[l][o + 64:o + 128], rwkv_mu[l][o + 128:o + 256]]
    if l > 0:
        lora.append(vres_down[l - 1])
        mus.append(vres_mu[l - 1])
    else:
        lora.append(jnp.zeros((d, 0), F32))
        mus.append(jnp.zeros((0,), F32))
    w_a = jnp.concatenate([wi[:, :o]] + [_pad_cols(t, LANES) for t in lora], axis=1)
    mu_a = jnp.concatenate([rwkv_mu[l][:o]] + [jnp.pad(t, (0, LANES - t.shape[0])) for t in mus])[None, :]

    o = A_COLS
    w_kr = wi[:, o + B_Q_RANK + B_KV_RANK:o + B_Q_RANK + B_KV_RANK + B_ROPE]
    place = lambda t: jnp.pad(t, ((0, 0), (B_NOPE, LANES - B_NOPE - B_ROPE)))
    w_b = jnp.concatenate([wi[:, o:o + B_Q_RANK + B_KV_RANK], place(w_kr), place(_rope_half_swap(w_kr))], axis=1)
    o += B_Q_RANK + B_KV_RANK + B_ROPE
    w_c = wi[:, o:o + C_IN]
    w_g = wi[:, o + C_IN:o + C_IN + N_BRANCH * d]

    scale = (B_NOPE + B_ROPE) ** -0.5 * np.log2(np.e)
    uq = (w_uq[l] * scale).reshape(B_Q_RANK, B_HEADS, B_NOPE + B_ROPE)
    uq_sw = jnp.concatenate([jnp.zeros_like(uq[..., :B_NOPE]), uq[..., B_NOPE + B_ROPE // 2:],
                             uq[..., B_NOPE:B_NOPE + B_ROPE // 2]], axis=-1)
    pad_head = lambda t: jnp.pad(t, ((0, 0), (0, 0), (0, LANES - t.shape[-1]))).reshape(t.shape[0], B_HEADS * LANES)
    ukv = w_ukv[l].reshape(B_KV_RANK, B_HEADS, B_NOPE + B_V)
    mo = jnp.pad(mla_out[l].reshape(B_HEADS, B_V, d), ((0, 0), (0, LANES - B_V), (0, 0))).reshape(B_HEADS * LANES, d)
    bf = lambda t: t.astype(BF16)
    return dict(w_a=bf(w_a), mu_a=mu_a, w_b=bf(w_b), w_c=bf(w_c), w_g=bf(w_g),
                w_q=bf(pad_head(uq)), w_q_swap=bf(pad_head(uq_sw)),
                w_k=bf(pad_head(ukv[..., :B_NOPE])), w_v=bf(pad_head(ukv[..., B_NOPE:])), mla_out=bf(mo))


def _rope_tables(positions):
    inv_freq = ROPE_THETA ** (-jnp.arange(0, B_ROPE, 2, dtype=F32) / B_ROPE)
    ang = positions.astype(F32)[:, None] * inv_freq
    cos, sin = jnp.cos(ang), jnp.sin(ang)
    s = positions.shape[0]
    pad = jnp.zeros((s, LANES - B_NOPE - B_ROPE), F32)
    cos_q = jnp.concatenate([jnp.ones((s, B_NOPE), F32), cos, cos, pad], axis=1)
    sin_t = jnp.concatenate([jnp.zeros((s, B_NOPE), F32), -sin, sin, pad], axis=1)
    cos_k = jnp.concatenate([jnp.zeros((s, B_NOPE), F32), cos, cos, pad], axis=1)
    return cos_q, sin_t, cos_k, sin_t


def kernel(x, positions, ffn1_pre_g, ffn1_post_g, ffn1_w_gate, ffn1_w_up, ffn1_w_down, mix_pre_g, mix_post_g, w_in, rwkv_mu, rwkv_w0, rwkv_w_up, rwkv_a0, rwkv_a_up, rwkv_g_up, rwkv_k_k, rwkv_k_a, rwkv_r_k, rwkv_gn_g, rwkv_gn_b, rwkv_vres_down, rwkv_vres_mu, rwkv_vres_up, rwkv_v0, rwkv_out, mla_q_norm_g, mla_w_uq, mla_kv_norm_g, mla_w_ukv, mla_out, hgrn_lower_bounds, hgrn_norm_g, hgrn_out, w_o, ffn2_pre_g, ffn2_post_g, ffn2_w_gate, ffn2_w_up, ffn2_w_down):
    bsz, seq, d = x.shape
    assert bsz == 1 and d == D_MODEL and seq % max(ROW_TILE, C_CHUNK * C_CHUNKS_PER_STEP, A_CHUNK * A_CHUNKS_PER_STEP) == 0
    depth = w_in.shape[0]
    flash_tile = min(1024, seq)
    tabs = _rope_tables(positions[0])
    lb_p = jax.nn.softmax(hgrn_lower_bounds.astype(F32), axis=0)
    lower_bounds = jnp.cumsum(lb_p, axis=0) - lb_p[0]
    row = lambda t: t[None, :]
    bf = lambda t: t.astype(BF16)

    h = x[0]
    v_first = None
    for l in range(depth):
        h = _ffn(h, row(ffn1_pre_g[l]), bf(ffn1_w_gate[l]), bf(ffn1_w_up[l]), bf(ffn1_w_down[l]),
                 row(ffn1_post_g[l]))

        lw = _layer_weights(l, w_in, rwkv_mu, rwkv_vres_down, rwkv_vres_mu, mla_w_uq, mla_w_ukv, mla_out)
        pre_g = row(mix_pre_g[l])
        pa = dict(w0=row(rwkv_w0[l]), w_up=bf(_pad_rows(rwkv_w_up[l], LANES)), a0=row(rwkv_a0[l]),
                  a_up=bf(_pad_rows(rwkv_a_up[l], LANES)), g_up=bf(rwkv_g_up[l]), k_k=row(rwkv_k_k[l]),
                  k_a=row(rwkv_k_a[l]), r_k=row(rwkv_r_k[l].reshape(-1)), gn_g=row(rwkv_gn_g[l]),
                  gn_b=row(rwkv_gn_b[l]))
        if l > 0:
            pa.update(v0=row(rwkv_v0[l - 1]), vres_up=bf(_pad_rows(rwkv_vres_up[l - 1], LANES)))
        pb = dict(q_norm_g=row(mla_q_norm_g[l]), kv_norm_g=row(mla_kv_norm_g[l]), w_q=lw["w_q"],
                  w_q_swap=lw["w_q_swap"], w_k=lw["w_k"], w_v=lw["w_v"])
        y_a, v_first, y_c, q_h, k_h, vt_h = _mixers_front(h, pre_g, lw, pa, v_first, row(lower_bounds[l]),
                                                          row(hgrn_norm_g[l]), tabs, pb)
        y_b = _flash(q_h, k_h, vt_h, flash_tile)

        h = _merge(h, y_a, y_b, y_c, pre_g, lw["w_g"], bf(rwkv_out[l]), lw["mla_out"], bf(hgrn_out[l]),
                   bf(w_o[l]), row(mix_post_g[l]))

        h = _ffn(h, row(ffn2_pre_g[l]), bf(ffn2_w_gate[l]), bf(ffn2_w_up[l]), bf(ffn2_w_down[l]),
                 row(ffn2_post_g[l]))
    return h[None]
```

```python
import functools

import jax
import jax.numpy as jnp
import numpy as np
from jax import lax
from jax.experimental import pallas as pl
from jax.experimental.pallas import tpu as pltpu

F32 = jnp.float32
BF16 = jnp.bfloat16

D_MODEL = 1024
D_FF = 2816
NORM_EPS = 1e-6
MACARON_WEIGHT = 0.5

A_HEADS = 8
A_HEAD_DIM = 64
A_WIDTH = 512
A_GN_EPS = 64e-5
A_CHUNK = 64
A_CHUNKS_PER_STEP = 4
A_COLS = 1792
A_IN = 2048

B_HEADS = 8
B_NOPE = 64
B_ROPE = 32
B_V = 64
B_Q_RANK = 384
B_KV_RANK = 256
B_IN = 896
ROPE_THETA = 10000.0
B_SUBTILES = 4
B_HEADS_PER_STEP = 2
LANES = 128

C_HEADS = 4
C_DIM = 128
C_WIDTH = 512
C_IN = 2048
C_CHUNK = 128
C_CHUNKS_PER_STEP = 2
C_SUB = 8
C_MIN_FORGET = 1e-6

N_BRANCH = 3
MERGE_ROW_TILE = 512
FFN_ROW_TILE = 1024
VMEM_LIMIT = 56 * 1024 * 1024

assert C_SUB / 2 * -np.log(C_MIN_FORGET) < 80.0


def _bdot(a, b):
    return jnp.dot(a.astype(BF16), b.astype(BF16), preferred_element_type=F32)


def _bdot_nt(a, b):
    return lax.dot_general(a.astype(BF16), b.astype(BF16), (((1,), (1,)), ((), ())),
                           preferred_element_type=F32)


def _bmm(a, b):
    return lax.dot_general(a.astype(BF16), b.astype(BF16), (((2,), (1,)), ((0,), (0,))),
                           preferred_element_type=F32)


def _bmm_nt(a, b):
    return lax.dot_general(a.astype(BF16), b.astype(BF16), (((2,), (2,)), ((0,), (0,))),
                           preferred_element_type=F32)


def _split_terms(data, parts):
    terms, rem = [], data
    for _ in range(parts):
        piece = rem.astype(BF16)
        terms.append(piece)
        rem = rem - piece.astype(F32)
    return terms


def _sel_dot(sel, data, parts):
    sel = sel.astype(BF16)
    return sum(jnp.dot(sel, t, preferred_element_type=F32) for t in _split_terms(data, parts))


def _dot_sel(data, sel, parts):
    sel = sel.astype(BF16)
    return sum(jnp.dot(t, sel, preferred_element_type=F32) for t in _split_terms(data, parts))


def _rms(x, g):
    return x * lax.rsqrt(jnp.mean(x * x, axis=-1, keepdims=True) + NORM_EPS) * g


def _sigmoid(x):
    return jax.nn.sigmoid(x)


def _params(*sem):
    return pltpu.CompilerParams(dimension_semantics=sem, vmem_limit_bytes=VMEM_LIMIT)


def _ffn_math(x, pre_g, wg, wu, wd, post_g):
    xn = _rms(x, pre_g).astype(BF16)
    gate = jnp.dot(xn, wg, preferred_element_type=F32)
    up = jnp.dot(xn, wu, preferred_element_type=F32)
    mid = (gate * _sigmoid(gate) * up).astype(BF16)
    y = jnp.dot(mid, wd, preferred_element_type=F32)
    return x + MACARON_WEIGHT * _rms(y, post_g)


def _ffn_body(x_ref, pre_g_ref, wg_ref, wu_ref, wd_ref, post_g_ref, o_ref):
    o_ref[...] = _ffn_math(x_ref[...], pre_g_ref[...], wg_ref[...], wu_ref[...], wd_ref[...], post_g_ref[...])


def _ffn(h, pre_g, wg, wu, wd, post_g):
    s = h.shape[0]
    tm = FFN_ROW_TILE
    const = lambda a: pl.BlockSpec(a.shape, lambda i: (0,) * a.ndim)
    return pl.pallas_call(
        _ffn_body,
        grid=(s // tm,),
        in_specs=[pl.BlockSpec((tm, D_MODEL), lambda i: (i, 0)), const(pre_g), const(wg), const(wu), const(wd),
                  const(post_g)],
        out_specs=pl.BlockSpec((tm, D_MODEL), lambda i: (i, 0)),
        out_shape=jax.ShapeDtypeStruct((s, D_MODEL), F32),
        compiler_params=_params("arbitrary"),
        name="ffn",
    )(h, pre_g, wg, wu, wd, post_g)


def _tri_inverse(a_strict, eye, blk16, lvl1, lvl2):
    d = a_strict * blk16
    d2 = _bmm(d, d)
    d4 = _bmm(d2, d2)
    d8 = _bmm(d4, d4)
    t = eye + d
    t = t + _bmm(t, d2)
    t = t + _bmm(t, d4)
    t = t + _bmm(t, d8)
    t = t + _bmm(_bmm(t, a_strict * lvl1), t)
    t = t + _bmm(_bmm(t, a_strict * lvl2), t)
    return t


def _rwkv_steps(has_vres, project, *refs):
    if has_vres:
        (vf_ref, w0_ref, wup_ref, a0_ref, aup_ref, gup_ref, kk_ref, ka_ref, rk_ref, gng_ref, gnb_ref,
         v0_ref, vup_ref, y_ref, state_ref) = refs
    else:
        (w0_ref, wup_ref, a0_ref, aup_ref, gup_ref, kk_ref, ka_ref, rk_ref, gng_ref, gnb_ref,
         y_ref, vf_out_ref, state_ref) = refs

    c = A_CHUNK
    w = A_WIDTH
    hd = A_HEAD_DIM
    x = project()
    yield
    rows = x.shape[0]
    r = x[:, 0:w]
    k = x[:, w:2 * w]
    v = x[:, 2 * w:3 * w]
    wl = x[:, 3 * w:3 * w + LANES]
    al = x[:, 3 * w + LANES:3 * w + 2 * LANES]
    gl = x[:, 3 * w + 2 * LANES:3 * w + 3 * LANES]

    z = w0_ref[...] + _bdot(jnp.tanh(wl), wup_ref[...])
    softplus_neg = jnp.maximum(-z, 0.0) + jnp.log(1.0 + jnp.exp(-jnp.abs(z)))
    logw = -jnp.exp(-softplus_neg - 0.5)
    a = _sigmoid(a0_ref[...] + _bdot(al, aup_ref[...]))
    g = _bdot(_sigmoid(gl), gup_ref[...])
    if has_vres:
        vl = x[:, 3 * w + 3 * LANES:3 * w + 4 * LANES]
        v = v + (vf_ref[...] - v) * _sigmoid(v0_ref[...] + _bdot(vl, vup_ref[...]))
    else:
        vf_out_ref[...] = v

    li = lax.broadcasted_iota(jnp.int32, (LANES, LANES), 0)
    lj = lax.broadcasted_iota(jnp.int32, (LANES, LANES), 1)
    pair_bd = ((li < hd) == (lj < hd)).astype(F32)
    lane = lax.broadcasted_iota(jnp.int32, (1, LANES), 1)
    m0 = (lane < hd).astype(F32)
    m1 = (lane >= hd).astype(F32)

    def head_sum(t):
        return jnp.concatenate([_dot_sel(t[:, p * LANES:(p + 1) * LANES], pair_bd, 1) for p in range(w // LANES)],
                               axis=1)

    def by_head(t):
        return jnp.concatenate([t * m0, t * m1], axis=0)

    yield
    kkr = k * kk_ref[...]
    kk = kkr / jnp.maximum(jnp.sqrt(head_sum(kkr * kkr)), 1e-12)
    k = k * (1.0 + (a - 1.0) * ka_ref[...])

    ti = lax.broadcasted_iota(jnp.int32, (2 * c, 2 * c), 0)
    tj = lax.broadcasted_iota(jnp.int32, (2 * c, 2 * c), 1)
    same_head = (ti // c) == (tj // c)
    incl = (same_head & (tj <= ti)).astype(F32)
    strict = (same_head & (tj < ti)).astype(F32)
    eye = (tj == ti).astype(F32)
    blk16 = ((ti // 16) == (tj // 16)).astype(F32)
    lvl1 = (((ti // 32) == (tj // 32)) & ((ti // 16) != (tj // 16))).astype(F32)
    lvl2 = (same_head & ((ti // 32) != (tj // 32))).astype(F32)

    ri = lax.broadcasted_iota(jnp.int32, (rows, rows), 0)
    rj = lax.broadcasted_iota(jnp.int32, (rows, rows), 1)
    chunk_incl = (((ri // c) == (rj // c)) & (rj <= ri)).astype(F32)
    lb = _sel_dot(chunk_incl, logw, 3)
    e_neg = jnp.exp(-lb)
    alpha_t = -kk * jnp.exp(lb - logw)
    beta = kk * a
    beta_h = beta * e_neg
    k_h = k * e_neg
    r_t = r * jnp.exp(lb)

    n_ch = rows // c
    n_pair = w // LANES

    def slabs(t):
        return [t[ci * c:(ci + 1) * c, p * LANES:(p + 1) * LANES] for ci in range(n_ch) for p in range(n_pair)]

    def stack_by_head(t):
        return jnp.stack([by_head(s) for s in slabs(t)])

    yield
    xa2, v2 = stack_by_head(alpha_t), stack_by_head(v)
    gram = _bmm_nt(jnp.concatenate([xa2, stack_by_head(r_t)], axis=1),
                   jnp.concatenate([stack_by_head(beta_h), stack_by_head(k_h)], axis=1))
    a_ab = gram[:, 0:2 * c, 0:2 * c] * strict
    a_ak = gram[:, 0:2 * c, 2 * c:4 * c] * strict
    a_r = jnp.concatenate([gram[:, 2 * c:4 * c, 2 * c:4 * c] * incl, gram[:, 2 * c:4 * c, 0:2 * c] * incl], axis=2)
    yield
    t = _tri_inverse(a_ab, eye, blk16, lvl1, lvl2)
    yield
    wu = _bmm(t, jnp.concatenate([xa2, _bmm(a_ak, v2)], axis=2))
    wt = wu[:, 0:c, 0:LANES] + wu[:, c:2 * c, 0:LANES]
    ut = wu[:, 0:c, LANES:2 * LANES] + wu[:, c:2 * c, LANES:2 * LANES]
    xr = jnp.stack(slabs(r_t))
    vp = jnp.stack(slabs(v))

    yield
    st = state_ref[...]
    y_chunks = []
    for ci in range(n_ch):
        bs = slice(ci * n_pair, (ci + 1) * n_pair)
        lb_last = lb[(ci + 1) * c - 1:(ci + 1) * c, :]
        e_end = jnp.exp(lb_last - lb[ci * c:(ci + 1) * c])
        k_e = k[ci * c:(ci + 1) * c] * e_end
        beta_e = beta[ci * c:(ci + 1) * c] * e_end
        gamma_c = jnp.exp(lb_last)
        u = _bmm_nt(wt[bs], st) + ut[bs]
        u2 = jnp.concatenate([u * m0, u * m1], axis=1)
        y2 = _bmm(a_r[bs], jnp.concatenate([v2[bs], u2], axis=1))
        y = _bmm_nt(xr[bs], st) + y2[:, 0:c] + y2[:, c:2 * c]
        y_chunks.append(jnp.concatenate([y[p] for p in range(n_pair)], axis=1))
        vu_t = jnp.stack([jnp.concatenate([vp[ci * n_pair + p], u[p]], axis=0).T for p in range(n_pair)])
        ke_be = jnp.stack([jnp.concatenate([k_e[:, p * LANES:(p + 1) * LANES], beta_e[:, p * LANES:(p + 1) * LANES]],
                                           axis=0) for p in range(n_pair)])
        decay = jnp.stack([gamma_c[:, p * LANES:(p + 1) * LANES] for p in range(n_pair)])
        st = st * decay + pair_bd * _bmm(vu_t, ke_be)
        yield
    state_ref[...] = st
    y = jnp.concatenate(y_chunks, axis=0)

    inv_n = 1.0 / hd
    mean = head_sum(y) * inv_n
    yc = y - mean
    var = head_sum(yc * yc) * inv_n
    yn = yc * lax.rsqrt(var + A_GN_EPS) * gng_ref[...] + gnb_ref[...]
    bonus = head_sum(r * k * rk_ref[...]) * v
    y_ref[...] = (yn + bonus) * g


def _mla_steps(project, cq_ref, sq_ref, ck_ref, sk_ref, qg_ref, wq_ref, wqs_ref, kg_ref, wk_ref, wv_ref,
               q_ref, k_ref, vt_ref):
    x = project()
    yield
    cq = x[:, 0:B_Q_RANK]
    ckv = x[:, B_Q_RANK:B_Q_RANK + B_KV_RANK]
    kr = x[:, B_Q_RANK + B_KV_RANK:B_Q_RANK + B_KV_RANK + LANES]
    kr_sw = x[:, B_Q_RANK + B_KV_RANK + LANES:B_Q_RANK + B_KV_RANK + 2 * LANES]
    cqn = _rms(cq, qg_ref[...]).astype(BF16)
    ckn = _rms(ckv, kg_ref[...]).astype(BF16)
    q = jnp.dot(cqn, wq_ref[...], preferred_element_type=F32)
    q_sw = jnp.dot(cqn, wqs_ref[...], preferred_element_type=F32)
    yield
    k_nope = jnp.dot(ckn, wk_ref[...], preferred_element_type=F32)
    val = jnp.dot(ckn, wv_ref[...], preferred_element_type=F32)
    k_rope = kr * ck_ref[...] + kr_sw * sk_ref[...]
    cq_t, sq_t = cq_ref[...], sq_ref[...]
    ones_col = (lax.broadcasted_iota(jnp.int32, (1, LANES), 1) == B_V).astype(F32)
    yield
    for hd in range(B_HEADS):
        sl = slice(hd * LANES, (hd + 1) * LANES)
        q_ref[hd] = (q[:, sl] * cq_t + q_sw[:, sl] * sq_t).astype(BF16)
        k_ref[hd] = (k_nope[:, sl] + k_rope).astype(BF16)
        vt_ref[hd] = (val[:, sl] + ones_col).T.astype(BF16)
        if hd % 2 == 1:
            yield


def _flash_body(q_ref, qn_ref, k_ref, vt_ref, o_ref, m_ref, acc_ref, s0_ref, s1_ref, s2_ref):
    i = pl.program_id(1)
    tile = q_ref.shape[1]
    sub = tile // B_SUBTILES
    units = [(a, hh) for a in range(B_SUBTILES) for hh in range(B_HEADS_PER_STEP)]
    m_ref[...] = jnp.full_like(m_ref, -jnp.inf)
    acc_ref[...] = jnp.zeros_like(acc_ref)

    def scores(queries_ref, j, s_ref, a, hh):
        start = pl.multiple_of(j * tile, tile)
        s_ref[hh, :, a * sub:(a + 1) * sub] = lax.dot_general(
            k_ref[hh, pl.ds(start, tile), :], queries_ref[hh, a * sub:(a + 1) * sub, :], (((1,), (1,)), ((), ())),
            preferred_element_type=F32)

    def softmax_pv(j, s_ref, a, hh, on_diagonal):
        start = pl.multiple_of(j * tile, tile)
        cs = slice(a * sub, (a + 1) * sub)
        n_keys = (a + 1) * sub if on_diagonal else tile
        s = s_ref[hh, 0:n_keys, cs]
        if on_diagonal:
            keys = lax.broadcasted_iota(jnp.int32, s.shape, 0)
            queries = lax.broadcasted_iota(jnp.int32, s.shape, 1) + a * sub
            s = jnp.where(keys <= queries, s, -jnp.inf)
        m_old = m_ref[hh:hh + 1, cs]
        m_new = jnp.maximum(m_old, jnp.max(s, axis=0, keepdims=True))
        p = jnp.exp2(s - m_new).astype(BF16)
        pv = jnp.dot(vt_ref[hh, :, pl.ds(start, n_keys)], p, preferred_element_type=F32)
        acc_ref[hh, :, cs] = jnp.exp2(m_old - m_new) * acc_ref[hh, :, cs] + pv
        m_ref[hh:hh + 1, cs] = m_new

    def phase(j, cur_ref, nxt_ref):
        for a, hh in units:
            scores(q_ref, j + 1, nxt_ref, a, hh)
            softmax_pv(j, cur_ref, a, hh, False)

    def diagonal(cur_ref, overlap_next):
        for a, hh in units:
            if overlap_next:
                scores(qn_ref, 0, s2_ref, a, hh)
            softmax_pv(i, cur_ref, a, hh, True)
        for hh in range(B_HEADS_PER_STEP):
            acc = acc_ref[hh]
            o_ref[:, hh * LANES:(hh + 1) * LANES] = (acc / acc[B_V:B_V + 1, :]).T

    @pl.when(i == 0)
    def _():
        for a, hh in units:
            scores(q_ref, 0, s2_ref, a, hh)
        diagonal(s2_ref, False)
        for a, hh in units:
            scores(qn_ref, 0, s2_ref, a, hh)

    @pl.when(i > 0)
    def _():
        phase(0, s2_ref, s0_ref)

        def two_tiles(jj, carry):
            phase(2 * jj + 1, s0_ref, s1_ref)
            phase(2 * jj + 2, s1_ref, s0_ref)
            return carry

        lax.fori_loop(0, (i - 1) // 2, two_tiles, 0)

        @pl.when(i % 2 == 0)
        def _():
            phase(i - 1, s0_ref, s1_ref)
            diagonal(s1_ref, True)

        @pl.when(i % 2 == 1)
        def _():
            diagonal(s0_ref, True)


def _flash(q, k, vt, tile):
    n_h, s, _ = q.shape
    n_q = s // tile
    hs = B_HEADS_PER_STEP
    scores = pltpu.VMEM((hs, tile, tile), F32)
    resident = pl.Buffered(1)
    return pl.pallas_call(
        _flash_body, grid=(n_h // hs, n_q),
        in_specs=[pl.BlockSpec((hs, tile, LANES), lambda h, i: (h, i, 0)),
                  pl.BlockSpec((hs, tile, LANES), lambda h, i: (h, jnp.minimum(i + 1, n_q - 1), 0)),
                  pl.BlockSpec((hs, s, LANES), lambda h, i: (h, 0, 0), pipeline_mode=resident),
                  pl.BlockSpec((hs, LANES, s), lambda h, i: (h, 0, 0), pipeline_mode=resident)],
        out_specs=pl.BlockSpec((tile, hs * LANES), lambda h, i: (i, h)),
        out_shape=jax.ShapeDtypeStruct((s, n_h * LANES), F32),
        scratch_shapes=[pltpu.VMEM((hs, tile), F32), pltpu.VMEM((hs, LANES, tile), F32), scores, scores, scores],
        compiler_params=_params("arbitrary", "arbitrary"), name="mla_flash",
    )(q, q, k, vt)


def _hgrn_steps(project, lb_ref, ng_ref, o_ref, state_ref):
    c = C_CHUNK
    w = C_WIDTH
    x = project()
    yield
    rows = x.shape[0]
    n_ch = rows // c
    lower = lb_ref[...]
    fz = x[:, w:2 * w]
    f = lower + (1.0 - lower) * _sigmoid(fz)
    logf = jnp.log(jnp.maximum(f, C_MIN_FORGET))
    kf = (1.0 - lower) * _sigmoid(-fz)
    cq = x[:, 0:w]
    q = cq * _sigmoid(cq)
    val = x[:, 2 * w:3 * w]
    cg = x[:, 3 * w:4 * w]

    def heads(t):
        return jnp.stack([t[ci * c:(ci + 1) * c, hd * C_DIM:(hd + 1) * C_DIM]
                          for ci in range(n_ch) for hd in range(C_HEADS)])

    ri = lax.broadcasted_iota(jnp.int32, (rows, rows), 0)
    rj = lax.broadcasted_iota(jnp.int32, (rows, rows), 1)
    row = lax.broadcasted_iota(jnp.int32, (rows, 1), 0)
    ti = lax.broadcasted_iota(jnp.int32, (c, c), 0)
    tj = lax.broadcasted_iota(jnp.int32, (c, c), 1)
    b = _sel_dot((((ri // c) == (rj // c)) & (rj <= ri)).astype(F32), logf, 3)

    def block_row(size, offset):
        return jnp.concatenate([jnp.broadcast_to(b[r + offset:r + offset + 1, :], (size, w))
                                for r in range(0, rows, size)], axis=0)

    b_last = block_row(c, c - 1)

    yield
    n0 = C_SUB
    ref = block_row(n0, n0 // 2 - 1)
    mask = ((ti // n0) == (tj // n0)) & (tj <= ti)
    attn = jnp.where(mask, _bmm_nt(heads(q * jnp.exp(b - ref)), heads(kf * jnp.exp(ref - b))), 0.0)
    yield
    n = n0
    while n < c:
        ref = block_row(2 * n, n - 1)
        right = ((row // n) % 2) == 1
        decay = jnp.exp(jnp.where(right, b - ref, ref - b))
        mask = ((ti // (2 * n)) == (tj // (2 * n))) & (((ti // n) % 2) == 1) & (((tj // n) % 2) == 0)
        attn = jnp.where(mask, _bmm_nt(heads(q * decay), heads(kf * decay)), attn)
        n *= 2
        yield

    vals = heads(val)
    intra = _bmm(attn, vals)
    q_in = heads(q * jnp.exp(b))
    k_end = heads(kf * jnp.exp(b_last - b))
    g_end = jnp.exp(b_last)
    gates = cg * _sigmoid(cg)
    st = state_ref[...]
    out_chunks = []
    for ci in range(n_ch):
        bs = slice(ci * C_HEADS, (ci + 1) * C_HEADS)
        o = intra[bs] + _bmm_nt(q_in[bs], st)
        vt = jnp.stack([vals[ci * C_HEADS + hd].T for hd in range(C_HEADS)])
        decay = jnp.stack([g_end[(ci + 1) * c - 1:(ci + 1) * c, hd * C_DIM:(hd + 1) * C_DIM] for hd in range(C_HEADS)])
        st = st * decay + _bmm(vt, k_end[bs])
        out_chunks.append(jnp.concatenate([_rms(o[hd], ng_ref[...]) for hd in range(C_HEADS)], axis=1))
        yield
    state_ref[...] = st
    o_ref[...] = jnp.concatenate(out_chunks, axis=0) * gates


N_MLA_IN = 10


def _mixers_body(has_vres, n_rwkv_in, *refs):
    n_rwkv_out = 1 if has_vres else 2
    (h_ref, g_ref, wa_ref, mu_ref, wc_ref, wb_ref), rest = refs[:6], refs[6:]
    rwkv_in, rest = rest[:n_rwkv_in], rest[n_rwkv_in:]
    hgrn_in, rest = rest[:2], rest[2:]
    mla_in, rest = rest[:N_MLA_IN], rest[N_MLA_IN:]
    rwkv_out, hgrn_out, mla_out = rest[:n_rwkv_out], rest[n_rwkv_out], rest[n_rwkv_out + 1:n_rwkv_out + 4]
    rwkv_state, hgrn_state, carry_ref = rest[n_rwkv_out + 4:]

    @pl.when(pl.program_id(0) == 0)
    def _():
        rwkv_state[...] = jnp.zeros_like(rwkv_state)
        hgrn_state[...] = jnp.zeros_like(hgrn_state)
        carry_ref[...] = jnp.zeros_like(carry_ref)

    xn = _rms(h_ref[...], g_ref[...]).astype(BF16)

    def project_shifted():
        p = jnp.dot(xn, wa_ref[...], preferred_element_type=F32)
        rows = p.shape[0]
        prev = pltpu.roll(p, 1, axis=0)
        first = lax.broadcasted_iota(jnp.int32, p.shape, 0) == 0
        prev = jnp.where(first, carry_ref[0:1, :], prev)
        carry_ref[0:1, :] = p[rows - 1:rows, :]
        return p + (prev - p) * mu_ref[...]

    def project(w_ref):
        return lambda: jnp.dot(xn, w_ref[...], preferred_element_type=F32)

    pending = [_rwkv_steps(has_vres, project_shifted, *rwkv_in, *rwkv_out, rwkv_state),
               _hgrn_steps(project(wc_ref), *hgrn_in, hgrn_out, hgrn_state),
               _mla_steps(project(wb_ref), *mla_in, *mla_out)]
    while pending:
        for steps in list(pending):
            if next(steps, StopIteration) is StopIteration:
                pending.remove(steps)


def _mixers_front(h, pre_g, lw, p, v_first, lower, norm_g, tabs, pb):
    s = h.shape[0]
    rows = A_CHUNK * A_CHUNKS_PER_STEP
    assert rows == C_CHUNK * C_CHUNKS_PER_STEP
    has_vres = v_first is not None
    row = lambda n: pl.BlockSpec((rows, n), lambda i: (i, 0))
    const = lambda a: pl.BlockSpec(a.shape, lambda i: (0,) * a.ndim)
    names = ["w0", "w_up", "a0", "a_up", "g_up", "k_k", "k_a", "r_k", "gn_g", "gn_b"]
    if has_vres:
        names += ["v0", "vres_up"]
    consts = [p[n] for n in names]
    proj_ins = [h, pre_g, lw["w_a"], lw["mu_a"], lw["w_c"], lw["w_b"]]
    proj_specs = [row(D_MODEL)] + [const(a) for a in proj_ins[1:]]
    rwkv_ins = ([v_first] if has_vres else []) + consts
    rwkv_specs = ([row(A_WIDTH)] if has_vres else []) + [const(a) for a in consts]
    mla_consts = [pb["q_norm_g"], pb["w_q"], pb["w_q_swap"], pb["kv_norm_g"], pb["w_k"], pb["w_v"]]
    assert len(tabs) + len(mla_consts) == N_MLA_IN
    n_y = 1 if has_vres else 2
    y_shape = jax.ShapeDtypeStruct((s, A_WIDTH), F32)
    head_spec = pl.BlockSpec((B_HEADS, rows, LANES), lambda i: (0, i, 0))
    head_shape = jax.ShapeDtypeStruct((B_HEADS, s, LANES), BF16)
    head_t_spec = pl.BlockSpec((B_HEADS, LANES, rows), lambda i: (0, 0, i))
    head_t_shape = jax.ShapeDtypeStruct((B_HEADS, LANES, s), BF16)
    out = pl.pallas_call(
        functools.partial(_mixers_body, has_vres, len(rwkv_ins)),
        grid=(s // rows,),
        in_specs=(proj_specs + rwkv_specs + [const(lower), const(norm_g)]
                  + [row(LANES)] * len(tabs) + [const(a) for a in mla_consts]),
        out_specs=[row(A_WIDTH)] * n_y + [row(C_WIDTH), head_spec, head_spec, head_t_spec],
        out_shape=[y_shape] * n_y + [jax.ShapeDtypeStruct((s, C_WIDTH), F32), head_shape, head_shape, head_t_shape],
        scratch_shapes=[pltpu.VMEM((A_WIDTH // LANES, LANES, LANES), F32), pltpu.VMEM((C_HEADS, C_DIM, C_DIM), F32),
                        pltpu.VMEM((8, A_IN), F32)],
        compiler_params=_params("arbitrary"), name="mixers_front",
    )(*proj_ins, *rwkv_ins, lower, norm_g, *tabs, *mla_consts)
    if has_vres:
        return (out[0], v_first) + tuple(out[1:])
    return tuple(out)


def _merge_ffn_body(h_ref, ya_ref, yb_ref, yc_ref, g_ref, wg_ref, wa_ref, wb_ref, wc_ref, wo_ref, pg_ref,
                    f_pre_ref, f_wg_ref, f_wu_ref, f_wd_ref, f_post_ref, o_ref):
    d = D_MODEL
    h = h_ref[...]
    u = _rms(h, g_ref[...]).astype(BF16)
    merged = None
    for n, (y_ref, w_ref) in enumerate(((ya_ref, wa_ref), (yb_ref, wb_ref), (yc_ref, wc_ref))):
        gate = _sigmoid(jnp.dot(u, wg_ref[:, n * d:(n + 1) * d], preferred_element_type=F32))
        term = gate * _bdot(y_ref[...], w_ref[...])
        merged = term if merged is None else merged + term
    h = h + _rms(_bdot(merged, wo_ref[...]), pg_ref[...])
    o_ref[...] = _ffn_math(h, f_pre_ref[...], f_wg_ref[...], f_wu_ref[...], f_wd_ref[...], f_post_ref[...])


def _merge_ffn(h, ya, yb, yc, merge_consts, ffn_consts):
    s = h.shape[0]
    tm = MERGE_ROW_TILE
    row = lambda n: pl.BlockSpec((tm, n), lambda i: (i, 0))
    const = lambda a: pl.BlockSpec(a.shape, lambda i: (0,) * a.ndim)
    consts = list(merge_consts) + list(ffn_consts)
    return pl.pallas_call(
        _merge_ffn_body, grid=(s // tm,),
        in_specs=[row(D_MODEL), row(A_WIDTH), row(B_HEADS * LANES), row(C_WIDTH)] + [const(a) for a in consts],
        out_specs=row(D_MODEL), out_shape=jax.ShapeDtypeStruct((s, D_MODEL), F32),
        compiler_params=_params("arbitrary"), name="merge_ffn",
    )(h, ya, yb, yc, *consts)


def _pad_cols(a, n):
    return jnp.pad(a, ((0, 0), (0, n - a.shape[1])))


def _pad_rows(a, n):
    return jnp.pad(a, ((0, n - a.shape[0]), (0, 0)))


def _rope_half_swap(a):
    half = B_ROPE // 2
    return jnp.concatenate([a[:, half:], a[:, :half]], axis=1)


def _layer_weights(l, w_in, rwkv_mu, vres_down, vres_mu, w_uq, w_ukv, mla_out):
    wi = w_in[l]
    d = D_MODEL
    o = 3 * A_WIDTH
    lora = [wi[:, o:o + 64], wi[:, o + 64:o + 128], wi[:, o + 128:o + 256]]
    mus = [rwkv_mu[l][o:o + 64], rwkv_mu[l][o + 64:o + 128], rwkv_mu[l][o + 128:o + 256]]
    if l > 0:
        lora.append(vres_down[l - 1])
        mus.append(vres_mu[l - 1])
    else:
        lora.append(jnp.zeros((d, 0), F32))
        mus.append(jnp.zeros((0,), F32))
    w_a = jnp.concatenate([wi[:, :o]] + [_pad_cols(t, LANES) for t in lora], axis=1)
    mu_a = jnp.concatenate([rwkv_mu[l][:o]] + [jnp.pad(t, (0, LANES - t.shape[0])) for t in mus])[None, :]

    o = A_COLS
    w_kr = wi[:, o + B_Q_RANK + B_KV_RANK:o + B_Q_RANK + B_KV_RANK + B_ROPE]
    place = lambda t: jnp.pad(t, ((0, 0), (B_NOPE, LANES - B_NOPE - B_ROPE)))
    w_b = jnp.concatenate([wi[:, o:o + B_Q_RANK + B_KV_RANK], place(w_kr), place(_rope_half_swap(w_kr))], axis=1)
    o += B_Q_RANK + B_KV_RANK + B_ROPE
    w_c = wi[:, o:o + C_IN]
    w_g = wi[:, o + C_IN:o + C_IN + N_BRANCH * d]

    scale = (B_NOPE + B_ROPE) ** -0.5 * np.log2(np.e)
    uq = (w_uq[l] * scale).reshape(B_Q_RANK, B_HEADS, B_NOPE + B_ROPE)
    uq_sw = jnp.concatenate([jnp.zeros_like(uq[..., :B_NOPE]), uq[..., B_NOPE + B_ROPE // 2:],
                             uq[..., B_NOPE:B_NOPE + B_ROPE // 2]], axis=-1)
    pad_head = lambda t: jnp.pad(t, ((0, 0), (0, 0), (0, LANES - t.shape[-1]))).reshape(t.shape[0], B_HEADS * LANES)
    ukv = w_ukv[l].reshape(B_KV_RANK, B_HEADS, B_NOPE + B_V)
    mo = jnp.pad(mla_out[l].reshape(B_HEADS, B_V, d), ((0, 0), (0, LANES - B_V), (0, 0))).reshape(B_HEADS * LANES, d)
    bf = lambda t: t.astype(BF16)
    return dict(w_a=bf(w_a), mu_a=mu_a, w_b=bf(w_b), w_c=bf(w_c), w_g=bf(w_g),
                w_q=bf(pad_head(uq)), w_q_swap=bf(pad_head(uq_sw)),
                w_k=bf(pad_head(ukv[..., :B_NOPE])), w_v=bf(pad_head(ukv[..., B_NOPE:])), mla_out=bf(mo))


def _rope_tables(positions):
    inv_freq = ROPE_THETA ** (-jnp.arange(0, B_ROPE, 2, dtype=F32) / B_ROPE)
    ang = positions.astype(F32)[:, None] * inv_freq
    cos, sin = jnp.cos(ang), jnp.sin(ang)
    s = positions.shape[0]
    pad = jnp.zeros((s, LANES - B_NOPE - B_ROPE), F32)
    cos_q = jnp.concatenate([jnp.ones((s, B_NOPE), F32), cos, cos, pad], axis=1)
    sin_t = jnp.concatenate([jnp.zeros((s, B_NOPE), F32), -sin, sin, pad], axis=1)
    cos_k = jnp.concatenate([jnp.zeros((s, B_NOPE), F32), cos, cos, pad], axis=1)
    return cos_q, sin_t, cos_k, sin_t


def kernel(x, positions, ffn1_pre_g, ffn1_post_g, ffn1_w_gate, ffn1_w_up, ffn1_w_down, mix_pre_g, mix_post_g, w_in, rwkv_mu, rwkv_w0, rwkv_w_up, rwkv_a0, rwkv_a_up, rwkv_g_up, rwkv_k_k, rwkv_k_a, rwkv_r_k, rwkv_gn_g, rwkv_gn_b, rwkv_vres_down, rwkv_vres_mu, rwkv_vres_up, rwkv_v0, rwkv_out, mla_q_norm_g, mla_w_uq, mla_kv_norm_g, mla_w_ukv, mla_out, hgrn_lower_bounds, hgrn_norm_g, hgrn_out, w_o, ffn2_pre_g, ffn2_post_g, ffn2_w_gate, ffn2_w_up, ffn2_w_down):
    bsz, seq, d = x.shape
    assert bsz == 1 and d == D_MODEL and seq % max(FFN_ROW_TILE, MERGE_ROW_TILE, C_CHUNK * C_CHUNKS_PER_STEP, A_CHUNK * A_CHUNKS_PER_STEP) == 0
    depth = w_in.shape[0]
    flash_tile = min(1024, seq)
    tabs = _rope_tables(positions[0])
    lb_p = jax.nn.softmax(hgrn_lower_bounds.astype(F32), axis=0)
    lower_bounds = jnp.cumsum(lb_p, axis=0) - lb_p[0]
    row = lambda t: t[None, :]
    bf = lambda t: t.astype(BF16)

    h = x[0]
    v_first = None
    for l in range(depth):
        h = _ffn(h, row(ffn1_pre_g[l]), bf(ffn1_w_gate[l]), bf(ffn1_w_up[l]), bf(ffn1_w_down[l]),
                 row(ffn1_post_g[l]))

        lw = _layer_weights(l, w_in, rwkv_mu, rwkv_vres_down, rwkv_vres_mu, mla_w_uq, mla_w_ukv, mla_out)
        pre_g = row(mix_pre_g[l])
        pa = dict(w0=row(rwkv_w0[l]), w_up=bf(_pad_rows(rwkv_w_up[l], LANES)), a0=row(rwkv_a0[l]),
                  a_up=bf(_pad_rows(rwkv_a_up[l], LANES)), g_up=bf(rwkv_g_up[l]), k_k=row(rwkv_k_k[l]),
                  k_a=row(rwkv_k_a[l]), r_k=row(rwkv_r_k[l].reshape(-1)), gn_g=row(rwkv_gn_g[l]),
                  gn_b=row(rwkv_gn_b[l]))
        if l > 0:
            pa.update(v0=row(rwkv_v0[l - 1]), vres_up=bf(_pad_rows(rwkv_vres_up[l - 1], LANES)))
        pb = dict(q_norm_g=row(mla_q_norm_g[l]), kv_norm_g=row(mla_kv_norm_g[l]), w_q=lw["w_q"],
                  w_q_swap=lw["w_q_swap"], w_k=lw["w_k"], w_v=lw["w_v"])
        y_a, v_first, y_c, q_h, k_h, vt_h = _mixers_front(h, pre_g, lw, pa, v_first, row(lower_bounds[l]),
                                                          row(hgrn_norm_g[l]), tabs, pb)
        y_b = _flash(q_h, k_h, vt_h, flash_tile)

        h = _merge_ffn(h, y_a, y_b, y_c,
                       [pre_g, lw["w_g"], bf(rwkv_out[l]), lw["mla_out"], bf(hgrn_out[l]), bf(w_o[l]),
                        row(mix_post_g[l])],
                       [row(ffn2_pre_g[l]), bf(ffn2_w_gate[l]), bf(ffn2_w_up[l]), bf(ffn2_w_down[l]),
                        row(ffn2_post_g[l])])
    return h[None]
```

```python
import functools

import jax
import jax.numpy as jnp
import numpy as np
from jax import lax
from jax.experimental import pallas as pl
from jax.experimental.pallas import tpu as pltpu

F32 = jnp.float32
BF16 = jnp.bfloat16

D_MODEL = 1024
D_FF = 2816
NORM_EPS = 1e-6
MACARON_WEIGHT = 0.5

A_HEADS = 8
A_HEAD_DIM = 64
A_WIDTH = 512
A_GN_EPS = 64e-5
A_CHUNK = 64
A_CHUNKS_PER_STEP = 4
A_COLS = 1792
A_IN = 2048

B_HEADS = 8
B_NOPE = 64
B_ROPE = 32
B_V = 64
B_Q_RANK = 384
B_KV_RANK = 256
B_IN = 896
ROPE_THETA = 10000.0
B_SUBTILES = 4
B_HEADS_PER_STEP = 2
LANES = 128

C_HEADS = 4
C_DIM = 128
C_WIDTH = 512
C_IN = 2048
C_CHUNK = 128
C_CHUNKS_PER_STEP = 2
C_SUB = 8
C_MIN_FORGET = 1e-6

N_BRANCH = 3
ROW_TILE = 512
FFN_ROW_TILE = 1024
VMEM_LIMIT = 56 * 1024 * 1024

assert C_SUB / 2 * -np.log(C_MIN_FORGET) < 80.0


def _bdot(a, b):
    return jnp.dot(a.astype(BF16), b.astype(BF16), preferred_element_type=F32)


def _bdot_nt(a, b):
    return lax.dot_general(a.astype(BF16), b.astype(BF16), (((1,), (1,)), ((), ())),
                           preferred_element_type=F32)


def _bmm(a, b):
    return lax.dot_general(a.astype(BF16), b.astype(BF16), (((2,), (1,)), ((0,), (0,))),
                           preferred_element_type=F32)


def _bmm_nt(a, b):
    return lax.dot_general(a.astype(BF16), b.astype(BF16), (((2,), (2,)), ((0,), (0,))),
                           preferred_element_type=F32)


def _split_terms(data, parts):
    terms, rem = [], data
    for _ in range(parts):
        piece = rem.astype(BF16)
        terms.append(piece)
        rem = rem - piece.astype(F32)
    return terms


def _sel_dot(sel, data, parts):
    sel = sel.astype(BF16)
    return sum(jnp.dot(sel, t, preferred_element_type=F32) for t in _split_terms(data, parts))


def _dot_sel(data, sel, parts):
    sel = sel.astype(BF16)
    return sum(jnp.dot(t, sel, preferred_element_type=F32) for t in _split_terms(data, parts))


def _rms(x, g):
    return x * lax.rsqrt(jnp.mean(x * x, axis=-1, keepdims=True) + NORM_EPS) * g


def _sigmoid(x):
    return jax.nn.sigmoid(x)


def _params(*sem):
    return pltpu.CompilerParams(dimension_semantics=sem, vmem_limit_bytes=VMEM_LIMIT)


def _ffn_body(x_ref, pre_g_ref, wg_ref, wu_ref, wd_ref, post_g_ref, o_ref):
    x = x_ref[...]
    xn = _rms(x, pre_g_ref[...]).astype(BF16)
    gate = jnp.dot(xn, wg_ref[...], preferred_element_type=F32)
    up = jnp.dot(xn, wu_ref[...], preferred_element_type=F32)
    mid = (gate * _sigmoid(gate) * up).astype(BF16)
    y = jnp.dot(mid, wd_ref[...], preferred_element_type=F32)
    o_ref[...] = x + MACARON_WEIGHT * _rms(y, post_g_ref[...])


def _ffn(h, pre_g, wg, wu, wd, post_g):
    s = h.shape[0]
    tm = FFN_ROW_TILE
    const = lambda a: pl.BlockSpec(a.shape, lambda i: (0,) * a.ndim)
    return pl.pallas_call(
        _ffn_body,
        grid=(s // tm,),
        in_specs=[pl.BlockSpec((tm, D_MODEL), lambda i: (i, 0)), const(pre_g), const(wg), const(wu), const(wd),
                  const(post_g)],
        out_specs=pl.BlockSpec((tm, D_MODEL), lambda i: (i, 0)),
        out_shape=jax.ShapeDtypeStruct((s, D_MODEL), F32),
        compiler_params=_params("arbitrary"),
        name="ffn",
    )(h, pre_g, wg, wu, wd, post_g)


def _tri_inverse(a_strict, eye, blk16, lvl1, lvl2):
    d = a_strict * blk16
    d2 = _bmm(d, d)
    d4 = _bmm(d2, d2)
    d8 = _bmm(d4, d4)
    t = eye + d
    t = t + _bmm(t, d2)
    t = t + _bmm(t, d4)
    t = t + _bmm(t, d8)
    t = t + _bmm(_bmm(t, a_strict * lvl1), t)
    t = t + _bmm(_bmm(t, a_strict * lvl2), t)
    return t


def _rwkv_steps(has_vres, project, *refs):
    if has_vres:
        (vf_ref, w0_ref, wup_ref, a0_ref, aup_ref, gup_ref, kk_ref, ka_ref, rk_ref, gng_ref, gnb_ref,
         v0_ref, vup_ref, y_ref, state_ref) = refs
    else:
        (w0_ref, wup_ref, a0_ref, aup_ref, gup_ref, kk_ref, ka_ref, rk_ref, gng_ref, gnb_ref,
         y_ref, vf_out_ref, state_ref) = refs

    c = A_CHUNK
    w = A_WIDTH
    hd = A_HEAD_DIM
    x = project()
    yield
    rows = x.shape[0]
    r = x[:, 0:w]
    k = x[:, w:2 * w]
    v = x[:, 2 * w:3 * w]
    wl = x[:, 3 * w:3 * w + LANES]
    al = x[:, 3 * w + LANES:3 * w + 2 * LANES]
    gl = x[:, 3 * w + 2 * LANES:3 * w + 3 * LANES]

    z = w0_ref[...] + _bdot(jnp.tanh(wl), wup_ref[...])
    softplus_neg = jnp.maximum(-z, 0.0) + jnp.log(1.0 + jnp.exp(-jnp.abs(z)))
    logw = -jnp.exp(-softplus_neg - 0.5)
    a = _sigmoid(a0_ref[...] + _bdot(al, aup_ref[...]))
    g = _bdot(_sigmoid(gl), gup_ref[...])
    if has_vres:
        vl = x[:, 3 * w + 3 * LANES:3 * w + 4 * LANES]
        v = v + (vf_ref[...] - v) * _sigmoid(v0_ref[...] + _bdot(vl, vup_ref[...]))
    else:
        vf_out_ref[...] = v

    li = lax.broadcasted_iota(jnp.int32, (LANES, LANES), 0)
    lj = lax.broadcasted_iota(jnp.int32, (LANES, LANES), 1)
    pair_bd = ((li < hd) == (lj < hd)).astype(F32)
    lane = lax.broadcasted_iota(jnp.int32, (1, LANES), 1)
    m0 = (lane < hd).astype(F32)
    m1 = (lane >= hd).astype(F32)

    def head_sum(t):
        return jnp.concatenate([_dot_sel(t[:, p * LANES:(p + 1) * LANES], pair_bd, 1) for p in range(w // LANES)],
                               axis=1)

    def by_head(t):
        return jnp.concatenate([t * m0, t * m1], axis=0)

    yield
    kkr = k * kk_ref[...]
    kk = kkr / jnp.maximum(jnp.sqrt(head_sum(kkr * kkr)), 1e-12)
    k = k * (1.0 + (a - 1.0) * ka_ref[...])

    ti = lax.broadcasted_iota(jnp.int32, (2 * c, 2 * c), 0)
    tj = lax.broadcasted_iota(jnp.int32, (2 * c, 2 * c), 1)
    same_head = (ti // c) == (tj // c)
    incl = (same_head & (tj <= ti)).astype(F32)
    strict = (same_head & (tj < ti)).astype(F32)
    eye = (tj == ti).astype(F32)
    blk16 = ((ti // 16) == (tj // 16)).astype(F32)
    lvl1 = (((ti // 32) == (tj // 32)) & ((ti // 16) != (tj // 16))).astype(F32)
    lvl2 = (same_head & ((ti // 32) != (tj // 32))).astype(F32)

    ri = lax.broadcasted_iota(jnp.int32, (rows, rows), 0)
    rj = lax.broadcasted_iota(jnp.int32, (rows, rows), 1)
    chunk_incl = (((ri // c) == (rj // c)) & (rj <= ri)).astype(F32)
    lb = _sel_dot(chunk_incl, logw, 3)
    e_neg = jnp.exp(-lb)
    alpha_t = -kk * jnp.exp(lb - logw)
    beta = kk * a
    beta_h = beta * e_neg
    k_h = k * e_neg
    r_t = r * jnp.exp(lb)

    n_ch = rows // c
    n_pair = w // LANES

    def slabs(t):
        return [t[ci * c:(ci + 1) * c, p * LANES:(p + 1) * LANES] for ci in range(n_ch) for p in range(n_pair)]

    def stack_by_head(t):
        return jnp.stack([by_head(s) for s in slabs(t)])

    yield
    xa2, v2 = stack_by_head(alpha_t), stack_by_head(v)
    gram = _bmm_nt(jnp.concatenate([xa2, stack_by_head(r_t)], axis=1),
                   jnp.concatenate([stack_by_head(beta_h), stack_by_head(k_h)], axis=1))
    a_ab = gram[:, 0:2 * c, 0:2 * c] * strict
    a_ak = gram[:, 0:2 * c, 2 * c:4 * c] * strict
    a_r = jnp.concatenate([gram[:, 2 * c:4 * c, 2 * c:4 * c] * incl, gram[:, 2 * c:4 * c, 0:2 * c] * incl], axis=2)
    yield
    t = _tri_inverse(a_ab, eye, blk16, lvl1, lvl2)
    yield
    wu = _bmm(t, jnp.concatenate([xa2, _bmm(a_ak, v2)], axis=2))
    wt = wu[:, 0:c, 0:LANES] + wu[:, c:2 * c, 0:LANES]
    ut = wu[:, 0:c, LANES:2 * LANES] + wu[:, c:2 * c, LANES:2 * LANES]
    xr = jnp.stack(slabs(r_t))
    vp = jnp.stack(slabs(v))

    yield
    st = state_ref[...]
    y_chunks = []
    for ci in range(n_ch):
        bs = slice(ci * n_pair, (ci + 1) * n_pair)
        lb_last = lb[(ci + 1) * c - 1:(ci + 1) * c, :]
        e_end = jnp.exp(lb_last - lb[ci * c:(ci + 1) * c])
        k_e = k[ci * c:(ci + 1) * c] * e_end
        beta_e = beta[ci * c:(ci + 1) * c] * e_end
        gamma_c = jnp.exp(lb_last)
        u = _bmm_nt(wt[bs], st) + ut[bs]
        u2 = jnp.concatenate([u * m0, u * m1], axis=1)
        y2 = _bmm(a_r[bs], jnp.concatenate([v2[bs], u2], axis=1))
        y = _bmm_nt(xr[bs], st) + y2[:, 0:c] + y2[:, c:2 * c]
        y_chunks.append(jnp.concatenate([y[p] for p in range(n_pair)], axis=1))
        vu_t = jnp.stack([jnp.concatenate([vp[ci * n_pair + p], u[p]], axis=0).T for p in range(n_pair)])
        ke_be = jnp.stack([jnp.concatenate([k_e[:, p * LANES:(p + 1) * LANES], beta_e[:, p * LANES:(p + 1) * LANES]],
                                           axis=0) for p in range(n_pair)])
        decay = jnp.stack([gamma_c[:, p * LANES:(p + 1) * LANES] for p in range(n_pair)])
        st = st * decay + pair_bd * _bmm(vu_t, ke_be)
        yield
    state_ref[...] = st
    y = jnp.concatenate(y_chunks, axis=0)

    inv_n = 1.0 / hd
    mean = head_sum(y) * inv_n
    yc = y - mean
    var = head_sum(yc * yc) * inv_n
    yn = yc * lax.rsqrt(var + A_GN_EPS) * gng_ref[...] + gnb_ref[...]
    bonus = head_sum(r * k * rk_ref[...]) * v
    y_ref[...] = (yn + bonus) * g


def _mla_steps(project, cq_ref, sq_ref, ck_ref, sk_ref, qg_ref, wq_ref, wqs_ref, kg_ref, wk_ref, wv_ref,
               q_ref, k_ref, vt_ref):
    x = project()
    yield
    cq = x[:, 0:B_Q_RANK]
    ckv = x[:, B_Q_RANK:B_Q_RANK + B_KV_RANK]
    kr = x[:, B_Q_RANK + B_KV_RANK:B_Q_RANK + B_KV_RANK + LANES]
    kr_sw = x[:, B_Q_RANK + B_KV_RANK + LANES:B_Q_RANK + B_KV_RANK + 2 * LANES]
    cqn = _rms(cq, qg_ref[...]).astype(BF16)
    ckn = _rms(ckv, kg_ref[...]).astype(BF16)
    q = jnp.dot(cqn, wq_ref[...], preferred_element_type=F32)
    q_sw = jnp.dot(cqn, wqs_ref[...], preferred_element_type=F32)
    yield
    k_nope = jnp.dot(ckn, wk_ref[...], preferred_element_type=F32)
    val = jnp.dot(ckn, wv_ref[...], preferred_element_type=F32)
    k_rope = kr * ck_ref[...] + kr_sw * sk_ref[...]
    cq_t, sq_t = cq_ref[...], sq_ref[...]
    ones_col = (lax.broadcasted_iota(jnp.int32, (1, LANES), 1) == B_V).astype(F32)
    yield
    for hd in range(B_HEADS):
        sl = slice(hd * LANES, (hd + 1) * LANES)
        q_ref[hd] = (q[:, sl] * cq_t + q_sw[:, sl] * sq_t).astype(BF16)
        k_ref[hd] = (k_nope[:, sl] + k_rope).astype(BF16)
        vt_ref[hd] = (val[:, sl] + ones_col).T.astype(BF16)
        if hd % 2 == 1:
            yield


def _flash_body(q_ref, qn_ref, k_ref, vt_ref, o_ref, m_ref, acc_ref, s0_ref, s1_ref, s2_ref):
    i = pl.program_id(1)
    tile = q_ref.shape[1]
    sub = tile // B_SUBTILES
    units = [(a, hh) for a in range(B_SUBTILES) for hh in range(B_HEADS_PER_STEP)]
    m_ref[...] = jnp.full_like(m_ref, -jnp.inf)
    acc_ref[...] = jnp.zeros_like(acc_ref)

    def scores(queries_ref, j, s_ref, a, hh, on_diagonal=False):
        start = pl.multiple_of(j * tile, tile)
        n_keys = (a + 1) * sub if on_diagonal else tile
        s_ref[hh, 0:n_keys, a * sub:(a + 1) * sub] = lax.dot_general(
            k_ref[hh, pl.ds(start, n_keys), :], queries_ref[hh, a * sub:(a + 1) * sub, :], (((1,), (1,)), ((), ())),
            preferred_element_type=F32)

    def softmax_pv(j, s_ref, a, hh, on_diagonal):
        start = pl.multiple_of(j * tile, tile)
        cs = slice(a * sub, (a + 1) * sub)
        n_keys = (a + 1) * sub if on_diagonal else tile
        s = s_ref[hh, 0:n_keys, cs]
        if on_diagonal:
            keys = lax.broadcasted_iota(jnp.int32, s.shape, 0)
            queries = lax.broadcasted_iota(jnp.int32, s.shape, 1) + a * sub
            s = jnp.where(keys <= queries, s, -jnp.inf)
        m_old = m_ref[hh:hh + 1, cs]
        m_new = jnp.maximum(m_old, jnp.max(s, axis=0, keepdims=True))
        p = jnp.exp2(s - m_new).astype(BF16)
        pv = jnp.dot(vt_ref[hh, :, pl.ds(start, n_keys)], p, preferred_element_type=F32)
        acc_ref[hh, :, cs] = jnp.exp2(m_old - m_new) * acc_ref[hh, :, cs] + pv
        m_ref[hh:hh + 1, cs] = m_new

    def phase(j, cur_ref, nxt_ref, next_on_diagonal=False):
        for a, hh in units:
            scores(q_ref, j + 1, nxt_ref, a, hh, next_on_diagonal)
            softmax_pv(j, cur_ref, a, hh, False)

    def diagonal(cur_ref, overlap_next):
        for a, hh in units:
            if overlap_next:
                scores(qn_ref, 0, s2_ref, a, hh)
            softmax_pv(i, cur_ref, a, hh, True)
        for hh in range(B_HEADS_PER_STEP):
            acc = acc_ref[hh]
            o_ref[:, hh * LANES:(hh + 1) * LANES] = (acc / acc[B_V:B_V + 1, :]).T

    @pl.when(i == 0)
    def _():
        for a, hh in units:
            scores(q_ref, 0, s2_ref, a, hh, True)
        diagonal(s2_ref, False)
        for a, hh in units:
            scores(qn_ref, 0, s2_ref, a, hh)

    @pl.when(i > 0)
    def _():
        phase(0, s2_ref, s0_ref)

        def two_tiles(jj, carry):
            phase(2 * jj + 1, s0_ref, s1_ref)
            phase(2 * jj + 2, s1_ref, s0_ref)
            return carry

        lax.fori_loop(0, (i - 1) // 2, two_tiles, 0)

        @pl.when(i % 2 == 0)
        def _():
            phase(i - 1, s0_ref, s1_ref, True)
            diagonal(s1_ref, True)

        @pl.when(i % 2 == 1)
        def _():
            diagonal(s0_ref, True)


def _flash(q, k, vt, tile):
    n_h, s, _ = q.shape
    n_q = s // tile
    hs = B_HEADS_PER_STEP
    scores = pltpu.VMEM((hs, tile, tile), F32)
    resident = pl.Buffered(1)
    return pl.pallas_call(
        _flash_body, grid=(n_h // hs, n_q),
        in_specs=[pl.BlockSpec((hs, tile, LANES), lambda h, i: (h, i, 0)),
                  pl.BlockSpec((hs, tile, LANES), lambda h, i: (h, jnp.minimum(i + 1, n_q - 1), 0)),
                  pl.BlockSpec((hs, s, LANES), lambda h, i: (h, 0, 0), pipeline_mode=resident),
                  pl.BlockSpec((hs, LANES, s), lambda h, i: (h, 0, 0), pipeline_mode=resident)],
        out_specs=pl.BlockSpec((tile, hs * LANES), lambda h, i: (i, h)),
        out_shape=jax.ShapeDtypeStruct((s, n_h * LANES), F32),
        scratch_shapes=[pltpu.VMEM((hs, tile), F32), pltpu.VMEM((hs, LANES, tile), F32), scores, scores, scores],
        compiler_params=_params("arbitrary", "arbitrary"), name="mla_flash",
    )(q, q, k, vt)


def _hgrn_steps(project, lb_ref, ng_ref, o_ref, state_ref):
    c = C_CHUNK
    w = C_WIDTH
    x = project()
    yield
    rows = x.shape[0]
    n_ch = rows // c
    lower = lb_ref[...]
    fz = x[:, w:2 * w]
    f = lower + (1.0 - lower) * _sigmoid(fz)
    logf = jnp.log(jnp.maximum(f, C_MIN_FORGET))
    kf = (1.0 - lower) * _sigmoid(-fz)
    cq = x[:, 0:w]
    q = cq * _sigmoid(cq)
    val = x[:, 2 * w:3 * w]
    cg = x[:, 3 * w:4 * w]

    def heads(t):
        return jnp.stack([t[ci * c:(ci + 1) * c, hd * C_DIM:(hd + 1) * C_DIM]
                          for ci in range(n_ch) for hd in range(C_HEADS)])

    n_pair = C_HEADS // 2
    first = lax.broadcasted_iota(jnp.int32, (1, 2 * C_DIM), 1) < C_DIM

    def pairs(t):
        return jnp.stack([t[ci * c:(ci + 1) * c, p * 2 * C_DIM:(p + 1) * 2 * C_DIM]
                          for ci in range(n_ch) for p in range(n_pair)])

    def pairs_by_rows(t):
        t = pairs(t)
        return jnp.concatenate([jnp.where(first, t, 0.0), jnp.where(first, 0.0, t)], axis=1)

    ri = lax.broadcasted_iota(jnp.int32, (rows, rows), 0)
    rj = lax.broadcasted_iota(jnp.int32, (rows, rows), 1)
    row = lax.broadcasted_iota(jnp.int32, (rows, 1), 0)
    ti = lax.broadcasted_iota(jnp.int32, (c, c), 0)
    tj = lax.broadcasted_iota(jnp.int32, (c, c), 1)
    b = _sel_dot((((ri // c) == (rj // c)) & (rj <= ri)).astype(F32), logf, 3)

    def block_row(size, offset):
        return jnp.concatenate([jnp.broadcast_to(b[r + offset:r + offset + 1, :], (size, w))
                                for r in range(0, rows, size)], axis=0)

    b_last = block_row(c, c - 1)

    yield
    n0 = C_SUB
    ref = block_row(n0, n0 // 2 - 1)
    mask = ((ti // n0) == (tj // n0)) & (tj <= ti)
    mask = jnp.concatenate([mask, mask], axis=1)
    attn = jnp.where(mask, _bmm_nt(pairs(q * jnp.exp(b - ref)), pairs_by_rows(kf * jnp.exp(ref - b))), 0.0)
    yield
    n = n0
    while n < c:
        ref = block_row(2 * n, n - 1)
        right = ((row // n) % 2) == 1
        decay = jnp.exp(jnp.where(right, b - ref, ref - b))
        mask = ((ti // (2 * n)) == (tj // (2 * n))) & (((ti // n) % 2) == 1) & (((tj // n) % 2) == 0)
        mask = jnp.concatenate([mask, mask], axis=1)
        attn = jnp.where(mask, _bmm_nt(pairs(q * decay), pairs_by_rows(kf * decay)), attn)
        n *= 2
        yield

    vals = heads(val)
    intra = _bmm(attn, pairs_by_rows(val))
    q_in = heads(q * jnp.exp(b))
    k_end = heads(kf * jnp.exp(b_last - b))
    g_end = jnp.exp(b_last)
    gates = cg * _sigmoid(cg)
    st = state_ref[...]
    out_chunks = []
    for ci in range(n_ch):
        bs = slice(ci * C_HEADS, (ci + 1) * C_HEADS)
        o = jnp.stack([intra[ci * n_pair + hd // 2][:, (hd % 2) * C_DIM:(hd % 2 + 1) * C_DIM]
                       for hd in range(C_HEADS)]) + _bmm_nt(q_in[bs], st)
        vt = jnp.stack([vals[ci * C_HEADS + hd].T for hd in range(C_HEADS)])
        decay = jnp.stack([g_end[(ci + 1) * c - 1:(ci + 1) * c, hd * C_DIM:(hd + 1) * C_DIM] for hd in range(C_HEADS)])
        st = st * decay + _bmm(vt, k_end[bs])
        out_chunks.append(jnp.concatenate([_rms(o[hd], ng_ref[...]) for hd in range(C_HEADS)], axis=1))
        yield
    state_ref[...] = st
    o_ref[...] = jnp.concatenate(out_chunks, axis=0) * gates


N_MLA_IN = 10


def _mixers_body(has_vres, n_rwkv_in, *refs):
    n_rwkv_out = 1 if has_vres else 2
    (h_ref, g_ref, wa_ref, mu_ref, wc_ref, wb_ref), rest = refs[:6], refs[6:]
    rwkv_in, rest = rest[:n_rwkv_in], rest[n_rwkv_in:]
    hgrn_in, rest = rest[:2], rest[2:]
    mla_in, rest = rest[:N_MLA_IN], rest[N_MLA_IN:]
    rwkv_out, hgrn_out, mla_out = rest[:n_rwkv_out], rest[n_rwkv_out], rest[n_rwkv_out + 1:n_rwkv_out + 4]
    rwkv_state, hgrn_state, carry_ref = rest[n_rwkv_out + 4:]

    @pl.when(pl.program_id(0) == 0)
    def _():
        rwkv_state[...] = jnp.zeros_like(rwkv_state)
        hgrn_state[...] = jnp.zeros_like(hgrn_state)
        carry_ref[...] = jnp.zeros_like(carry_ref)

    xn = _rms(h_ref[...], g_ref[...]).astype(BF16)

    def project_shifted():
        p = jnp.dot(xn, wa_ref[...], preferred_element_type=F32)
        rows = p.shape[0]
        prev = pltpu.roll(p, 1, axis=0)
        first = lax.broadcasted_iota(jnp.int32, p.shape, 0) == 0
        prev = jnp.where(first, carry_ref[0:1, :], prev)
        carry_ref[0:1, :] = p[rows - 1:rows, :]
        return p + (prev - p) * mu_ref[...]

    def project(w_ref):
        return lambda: jnp.dot(xn, w_ref[...], preferred_element_type=F32)

    pending = [_rwkv_steps(has_vres, project_shifted, *rwkv_in, *rwkv_out, rwkv_state),
               _hgrn_steps(project(wc_ref), *hgrn_in, hgrn_out, hgrn_state),
               _mla_steps(project(wb_ref), *mla_in, *mla_out)]
    while pending:
        for steps in list(pending):
            if next(steps, StopIteration) is StopIteration:
                pending.remove(steps)


def _mixers_front(h, pre_g, lw, p, v_first, lower, norm_g, tabs, pb):
    s = h.shape[0]
    rows = A_CHUNK * A_CHUNKS_PER_STEP
    assert rows == C_CHUNK * C_CHUNKS_PER_STEP
    has_vres = v_first is not None
    row = lambda n: pl.BlockSpec((rows, n), lambda i: (i, 0))
    const = lambda a: pl.BlockSpec(a.shape, lambda i: (0,) * a.ndim)
    names = ["w0", "w_up", "a0", "a_up", "g_up", "k_k", "k_a", "r_k", "gn_g", "gn_b"]
    if has_vres:
        names += ["v0", "vres_up"]
    consts = [p[n] for n in names]
    proj_ins = [h, pre_g, lw["w_a"], lw["mu_a"], lw["w_c"], lw["w_b"]]
    proj_specs = [row(D_MODEL)] + [const(a) for a in proj_ins[1:]]
    rwkv_ins = ([v_first] if has_vres else []) + consts
    rwkv_specs = ([row(A_WIDTH)] if has_vres else []) + [const(a) for a in consts]
    mla_consts = [pb["q_norm_g"], pb["w_q"], pb["w_q_swap"], pb["kv_norm_g"], pb["w_k"], pb["w_v"]]
    assert len(tabs) + len(mla_consts) == N_MLA_IN
    n_y = 1 if has_vres else 2
    y_shape = jax.ShapeDtypeStruct((s, A_WIDTH), F32)
    head_spec = pl.BlockSpec((B_HEADS, rows, LANES), lambda i: (0, i, 0))
    head_shape = jax.ShapeDtypeStruct((B_HEADS, s, LANES), BF16)
    head_t_spec = pl.BlockSpec((B_HEADS, LANES, rows), lambda i: (0, 0, i))
    head_t_shape = jax.ShapeDtypeStruct((B_HEADS, LANES, s), BF16)
    out = pl.pallas_call(
        functools.partial(_mixers_body, has_vres, len(rwkv_ins)),
        grid=(s // rows,),
        in_specs=(proj_specs + rwkv_specs + [const(lower), const(norm_g)]
                  + [row(LANES)] * len(tabs) + [const(a) for a in mla_consts]),
        out_specs=[row(A_WIDTH)] * n_y + [row(C_WIDTH), head_spec, head_spec, head_t_spec],
        out_shape=[y_shape] * n_y + [jax.ShapeDtypeStruct((s, C_WIDTH), F32), head_shape, head_shape, head_t_shape],
        scratch_shapes=[pltpu.VMEM((A_WIDTH // LANES, LANES, LANES), F32), pltpu.VMEM((C_HEADS, C_DIM, C_DIM), F32),
                        pltpu.VMEM((8, A_IN), F32)],
        compiler_params=_params("arbitrary"), name="mixers_front",
    )(*proj_ins, *rwkv_ins, lower, norm_g, *tabs, *mla_consts)
    if has_vres:
        return (out[0], v_first) + tuple(out[1:])
    return tuple(out)


def _merge_body(h_ref, ya_ref, yb_ref, yc_ref, g_ref, wg_ref, wa_ref, wb_ref, wc_ref, wo_ref, pg_ref, o_ref):
    d = D_MODEL
    h = h_ref[...]
    u = _rms(h, g_ref[...]).astype(BF16)
    merged = None
    for n, (y_ref, w_ref) in enumerate(((ya_ref, wa_ref), (yb_ref, wb_ref), (yc_ref, wc_ref))):
        gate = _sigmoid(jnp.dot(u, wg_ref[:, n * d:(n + 1) * d], preferred_element_type=F32))
        term = gate * _bdot(y_ref[...], w_ref[...])
        merged = term if merged is None else merged + term
    o_ref[...] = h + _rms(_bdot(merged, wo_ref[...]), pg_ref[...])


def _merge(h, ya, yb, yc, pre_g, wg, wa, wb, wc, wo, post_g):
    s = h.shape[0]
    tm = ROW_TILE
    row = lambda n: pl.BlockSpec((tm, n), lambda i: (i, 0))
    const = lambda a: pl.BlockSpec(a.shape, lambda i: (0,) * a.ndim)
    consts = [pre_g, wg, wa, wb, wc, wo, post_g]
    return pl.pallas_call(
        _merge_body, grid=(s // tm,),
        in_specs=[row(D_MODEL), row(A_WIDTH), row(B_HEADS * LANES), row(C_WIDTH)] + [const(a) for a in consts],
        out_specs=row(D_MODEL), out_shape=jax.ShapeDtypeStruct((s, D_MODEL), F32),
        compiler_params=_params("arbitrary"), name="merge",
    )(h, ya, yb, yc, *consts)


def _pad_cols(a, n):
    return jnp.pad(a, ((0, 0), (0, n - a.shape[1])))


def _pad_rows(a, n):
    return jnp.pad(a, ((0, n - a.shape[0]), (0, 0)))


def _rope_half_swap(a):
    half = B_ROPE // 2
    return jnp.concatenate([a[:, half:], a[:, :half]], axis=1)


def _layer_weights(l, w_in, rwkv_mu, vres_down, vres_mu, w_uq, w_ukv, mla_out):
    wi = w_in[l]
    d = D_MODEL
    o = 3 * A_WIDTH
    lora = [wi[:, o:o + 64], wi[:, o + 64:o + 128], wi[:, o + 128:o + 256]]
    mus = [rwkv_mu[l][o:o + 64], rwkv_mu[l][o + 64:o + 128], rwkv_mu[l][o + 128:o + 256]]
    if l > 0:
        lora.append(vres_down[l - 1])
        mus.append(vres_mu[l - 1])
    else:
        lora.append(jnp.zeros((d, 0), F32))
        mus.append(jnp.zeros((0,), F32))
    w_a = jnp.concatenate([wi[:, :o]] + [_pad_cols(t, LANES) for t in lora], axis=1)
    mu_a = jnp.concatenate([rwkv_mu[l][:o]] + [jnp.pad(t, (0, LANES - t.shape[0])) for t in mus])[None, :]

    o = A_COLS
    w_kr = wi[:, o + B_Q_RANK + B_KV_RANK:o + B_Q_RANK + B_KV_RANK + B_ROPE]
    place = lambda t: jnp.pad(t, ((0, 0), (B_NOPE, LANES - B_NOPE - B_ROPE)))
    w_b = jnp.concatenate([wi[:, o:o + B_Q_RANK + B_KV_RANK], place(w_kr), place(_rope_half_swap(w_kr))], axis=1)
    o += B_Q_RANK + B_KV_RANK + B_ROPE
    w_c = wi[:, o:o + C_IN]
    w_g = wi[:, o + C_IN:o + C_IN + N_BRANCH * d]

    scale = (B_NOPE + B_ROPE) ** -0.5 * np.log2(np.e)
    uq = (w_uq[l] * scale).reshape(B_Q_RANK, B_HEADS, B_NOPE + B_ROPE)
    uq_sw = jnp.concatenate([jnp.zeros_like(uq[..., :B_NOPE]), uq[..., B_NOPE + B_ROPE // 2:],
                             uq[..., B_NOPE:B_NOPE + B_ROPE // 2]], axis=-1)
    pad_head = lambda t: jnp.pad(t, ((0, 0), (0, 0), (0, LANES - t.shape[-1]))).reshape(t.shape[0], B_HEADS * LANES)
    ukv = w_ukv[l].reshape(B_KV_RANK, B_HEADS, B_NOPE + B_V)
    mo = jnp.pad(mla_out[l].reshape(B_HEADS, B_V, d), ((0, 0), (0, LANES - B_V), (0, 0))).reshape(B_HEADS * LANES, d)
    bf = lambda t: t.astype(BF16)
    return dict(w_a=bf(w_a), mu_a=mu_a, w_b=bf(w_b), w_c=bf(w_c), w_g=bf(w_g),
                w_q=bf(pad_head(uq)), w_q_swap=bf(pad_head(uq_sw)),
                w_k=bf(pad_head(ukv[..., :B_NOPE])), w_v=bf(pad_head(ukv[..., B_NOPE:])), mla_out=bf(mo))


def _rope_tables(positions):
    inv_freq = ROPE_THETA ** (-jnp.arange(0, B_ROPE, 2, dtype=F32) / B_ROPE)
    ang = positions.astype(F32)[:, None] * inv_freq
    cos, sin = jnp.cos(ang), jnp.sin(ang)
    s = positions.shape[0]
    pad = jnp.zeros((s, LANES - B_NOPE - B_ROPE), F32)
    cos_q = jnp.concatenate([jnp.ones((s, B_NOPE), F32), cos, cos, pad], axis=1)
    sin_t = jnp.concatenate([jnp.zeros((s, B_NOPE), F32), -sin, sin, pad], axis=1)
    cos_k = jnp.concatenate([jnp.zeros((s, B_NOPE), F32), cos, cos, pad], axis=1)
    return cos_q, sin_t, cos_k, sin_t


def kernel(x, positions, ffn1_pre_g, ffn1_post_g, ffn1_w_gate, ffn1_w_up, ffn1_w_down, mix_pre_g, mix_post_g, w_in, rwkv_mu, rwkv_w0, rwkv_w_up, rwkv_a0, rwkv_a_up, rwkv_g_up, rwkv_k_k, rwkv_k_a, rwkv_r_k, rwkv_gn_g, rwkv_gn_b, rwkv_vres_down, rwkv_vres_mu, rwkv_vres_up, rwkv_v0, rwkv_out, mla_q_norm_g, mla_w_uq, mla_kv_norm_g, mla_w_ukv, mla_out, hgrn_lower_bounds, hgrn_norm_g, hgrn_out, w_o, ffn2_pre_g, ffn2_post_g, ffn2_w_gate, ffn2_w_up, ffn2_w_down):
    bsz, seq, d = x.shape
    assert bsz == 1 and d == D_MODEL and seq % max(ROW_TILE, C_CHUNK * C_CHUNKS_PER_STEP, A_CHUNK * A_CHUNKS_PER_STEP) == 0
    depth = w_in.shape[0]
    flash_tile = min(1024, seq)
    tabs = _rope_tables(positions[0])
    lb_p = jax.nn.softmax(hgrn_lower_bounds.astype(F32), axis=0)
    lower_bounds = jnp.cumsum(lb_p, axis=0) - lb_p[0]
    row = lambda t: t[None, :]
    bf = lambda t: t.astype(BF16)

    h = x[0]
    v_first = None
    for l in range(depth):
        h = _ffn(h, row(ffn1_pre_g[l]), bf(ffn1_w_gate[l]), bf(ffn1_w_up[l]), bf(ffn1_w_down[l]),
                 row(ffn1_post_g[l]))

        lw = _layer_weights(l, w_in, rwkv_mu, rwkv_vres_down, rwkv_vres_mu, mla_w_uq, mla_w_ukv, mla_out)
        pre_g = row(mix_pre_g[l])
        pa = dict(w0=row(rwkv_w0[l]), w_up=bf(_pad_rows(rwkv_w_up[l], LANES)), a0=row(rwkv_a0[l]),
                  a_up=bf(_pad_rows(rwkv_a_up[l], LANES)), g_up=bf(rwkv_g_up[l]), k_k=row(rwkv_k_k[l]),
                  k_a=row(rwkv_k_a[l]), r_k=row(rwkv_r_k[l].reshape(-1)), gn_g=row(rwkv_gn_g[l]),
                  gn_b=row(rwkv_gn_b[l]))
        if l > 0:
            pa.update(v0=row(rwkv_v0[l - 1]), vres_up=bf(_pad_rows(rwkv_vres_up[l - 1], LANES)))
        pb = dict(q_norm_g=row(mla_q_norm_g[l]), kv_norm_g=row(mla_kv_norm_g[l]), w_q=lw["w_q"],
                  w_q_swap=lw["w_q_swap"], w_k=lw["w_k"], w_v=lw["w_v"])
        y_a, v_first, y_c, q_h, k_h, vt_h = _mixers_front(h, pre_g, lw, pa, v_first, row(lower_bounds[l]),
                                                          row(hgrn_norm_g[l]), tabs, pb)
        y_b = _flash(q_h, k_h, vt_h, flash_tile)

        h = _merge(h, y_a, y_b, y_c, pre_g, lw["w_g"], bf(rwkv_out[l]), lw["mla_out"], bf(hgrn_out[l]),
                   bf(w_o[l]), row(mix_post_g[l]))

        h = _ffn(h, row(ffn2_pre_g[l]), bf(ffn2_w_gate[l]), bf(ffn2_w_up[l]), bf(ffn2_w_down[l]),
                 row(ffn2_post_g[l]))
    return h[None]
```
